```python
import jax, jax.numpy as jnp
from jax import lax
import numpy as np

D_MODEL = 1024
BATCH = 8
SEQ = 8192
DEPTH = 4

CHUNK = 64
N_A_LAYERS = DEPTH // 2
N_B_LAYERS = DEPTH - N_A_LAYERS
EXPAND = 2
D_INNER = EXPAND * D_MODEL
POOL_WINDOWS = (2, 4, 8, 16)
N_POOL_GROUPS = len(POOL_WINDOWS)
POOL_GROUP_W = D_INNER // N_POOL_GROUPS
N_HEADS = 16
HEAD_DIM = D_INNER // N_HEADS
LEFT_CHUNKS = 8
BAND = (LEFT_CHUNKS + 1) * CHUNK
REL_CLIP = 128
EPS = 1e-6

kernel_name = "yoco_pool_chunkattn_adaln_trunk"


def rms_norm(x, g):
    xf = x.astype(jnp.float32)
    y = xf * lax.rsqrt(jnp.mean(xf * xf, axis=-1, keepdims=True) + EPS)
    return (y * g.astype(jnp.float32)).astype(x.dtype)


def modulate(h, shift, scale):
    return h * (1 + scale[:, None, :]) + shift[:, None, :]


def multiscale_pool(v):
    B, S, _ = v.shape
    vf = v.astype(jnp.float32).reshape(B, S, N_POOL_GROUPS, POOL_GROUP_W)
    cs = jnp.cumsum(vf, axis=1)
    t = jnp.arange(S)
    outs = []
    for g, w in enumerate(POOL_WINDOWS):
        csg = cs[:, :, g]
        lagged = jnp.pad(csg, ((0, 0), (w, 0), (0, 0)))[:, :S]
        cnt = jnp.minimum(t + 1, w).astype(jnp.float32)[None, :, None]
        outs.append((csg - lagged) / cnt - vf[:, :, g])
    return jnp.stack(outs, axis=2)


def chunk_band_attention(q, k, v, rel_bias):
    B, S, H, Dh = q.shape
    n_chunks = S // CHUNK
    pad = LEFT_CHUNKS * CHUNK
    k_pad = jnp.pad(k, ((0, 0), (pad, 0), (0, 0), (0, 0)))
    v_pad = jnp.pad(v, ((0, 0), (pad, 0), (0, 0), (0, 0)))
    qi = jnp.arange(CHUNK)[:, None]
    kj = jnp.arange(BAND)[None, :]
    dist = pad + qi - kj
    idx = jnp.clip(dist, -REL_CLIP, REL_CLIP) + REL_CLIP
    bias = rel_bias[:, idx].astype(jnp.float32)
    sm_scale = HEAD_DIM ** -0.5
    band_pos = jnp.arange(BAND) - pad

    def one_chunk(n):
        start = n * CHUNK
        qb = lax.dynamic_slice_in_dim(q, start, CHUNK, axis=1)
        kb = lax.dynamic_slice_in_dim(k_pad, start, BAND, axis=1)
        vb = lax.dynamic_slice_in_dim(v_pad, start, BAND, axis=1)
        s = jnp.einsum('bqhd,bkhd->bhqk', qb, kb).astype(jnp.float32) * sm_scale + bias[None]
        valid = (start + band_pos) >= 0
        s = jnp.where(valid[None, None, None, :], s, -jnp.inf)
        p = jax.nn.softmax(s, axis=-1)
        return jnp.einsum('bhqk,bkhd->bqhd', p.astype(vb.dtype), vb)

    out = lax.map(one_chunk, jnp.arange(n_chunks))
    return out.transpose(1, 0, 2, 3, 4).reshape(B, S, H * Dh)


def _fwd_setup_inputs(seed: int = 0) -> dict:
    key = jax.random.key(seed)
    ks = jax.random.split(key, 20)
    D, E, Gw = D_MODEL, D_INNER, POOL_GROUP_W
    nrm = jax.random.normal
    return {
        "x": nrm(ks[0], (BATCH, SEQ, D), jnp.float32),
        "c": nrm(ks[1], (BATCH, D), jnp.float32),
        "ada_w": nrm(ks[2], (DEPTH, D, 3 * D), jnp.float32) * D ** -0.5,
        "ada_b": 0.01 * nrm(ks[3], (DEPTH, 3 * D), jnp.float32),
        "norm_g": 1.0 + 0.05 * nrm(ks[4], (DEPTH, D), jnp.float32),
        "a_w_in": nrm(ks[5], (N_A_LAYERS, D, 2 * E), jnp.float32) * D ** -0.5,
        "a_w_group": nrm(ks[6], (N_A_LAYERS, N_POOL_GROUPS, Gw, Gw), jnp.float32) * Gw ** -0.5,
        "a_scale": 1.0 + 0.1 * nrm(ks[7], (N_A_LAYERS, E), jnp.float32),
        "a_w_out": nrm(ks[8], (N_A_LAYERS, E, D), jnp.float32) * E ** -0.5,
        "kv_norm_g": 1.0 + 0.05 * nrm(ks[9], (D,), jnp.float32),
        "kv_ada_w": nrm(ks[10], (D, 2 * D), jnp.float32) * D ** -0.5,
        "kv_ada_b": 0.01 * nrm(ks[11], (2 * D,), jnp.float32),
        "w_kv": nrm(ks[12], (D, 2 * E), jnp.float32) * D ** -0.5,
        "b_w_in": nrm(ks[13], (N_B_LAYERS, D, 2 * E), jnp.float32) * D ** -0.5,
        "b_rel_bias": 0.5 * nrm(ks[14], (N_B_LAYERS, N_HEADS, 2 * REL_CLIP + 1), jnp.float32),
        "b_w_out": nrm(ks[15], (N_B_LAYERS, E, D), jnp.float32) * E ** -0.5,
        "final_g": 1.0 + 0.05 * nrm(ks[16], (D,), jnp.float32),
    }


def _fwd_reference(x, c, ada_w, ada_b, norm_g, a_w_in, a_w_group, a_scale, a_w_out,
              kv_norm_g, kv_ada_w, kv_ada_b, w_kv, b_w_in, b_rel_bias, b_w_out,
              final_g):
    B, S, _ = x.shape
    c_act = jax.nn.silu(c)
    h = x
    k = v = None
    for layer in range(DEPTH):
        shift, scale, gate = jnp.split(c_act @ ada_w[layer] + ada_b[layer], 3, axis=-1)
        u = modulate(rms_norm(h, norm_g[layer]), shift, scale)
        if layer < N_A_LAYERS:
            a = layer
            val, z = jnp.split(u @ a_w_in[a], 2, axis=-1)
            pooled = multiscale_pool(val).astype(val.dtype)
            mixed = jnp.einsum('bsgi,gio->bsgo', pooled, a_w_group[a]).reshape(B, S, D_INNER)
            mixed = mixed * a_scale[a]
            y = (mixed * jax.nn.silu(z)) @ a_w_out[a]
        else:
            if layer == N_A_LAYERS:
                kv_shift, kv_scale = jnp.split(c_act @ kv_ada_w + kv_ada_b, 2, axis=-1)
                hk = modulate(rms_norm(h, kv_norm_g), kv_shift, kv_scale)
                k, v = jnp.split(hk @ w_kv, 2, axis=-1)
                k = k.reshape(B, S, N_HEADS, HEAD_DIM)
                v = v.reshape(B, S, N_HEADS, HEAD_DIM)
            bi = layer - N_A_LAYERS
            qv, z = jnp.split(u @ b_w_in[bi], 2, axis=-1)
            q = qv.reshape(B, S, N_HEADS, HEAD_DIM)
            att = chunk_band_attention(q, k, v, b_rel_bias[bi])
            y = (att * jax.nn.silu(z)) @ b_w_out[bi]
        h = h + gate[:, None, :] * y
    return rms_norm(h, final_g)


import jax as _jax
import jax.numpy as _jnp

TWIN_FORMAT = 'train_step'
FWD_PARAMS = ['x', 'c', 'ada_w', 'ada_b', 'norm_g', 'a_w_in', 'a_w_group', 'a_scale', 'a_w_out', 'kv_norm_g', 'kv_ada_w', 'kv_ada_b', 'w_kv', 'b_w_in', 'b_rel_bias', 'b_w_out', 'final_g']
TWIN_WEIGHTS = ['ada_w', 'ada_b', 'norm_g', 'a_w_in', 'a_w_group', 'a_scale', 'a_w_out', 'kv_norm_g', 'kv_ada_w', 'kv_ada_b', 'w_kv', 'b_w_in', 'b_rel_bias', 'b_w_out', 'final_g']
TWIN_DIFF_INPUT = 'x'
TWIN_INPUTS = ['x', 'c', 'ada_w', 'ada_b', 'norm_g', 'a_w_in', 'a_w_group', 'a_scale', 'a_w_out', 'kv_norm_g', 'kv_ada_w', 'kv_ada_b', 'w_kv', 'b_w_in', 'b_rel_bias', 'b_w_out', 'final_g', 'loss_target', 'm_ada_w', 'm_ada_b', 'm_norm_g', 'm_a_w_in', 'm_a_w_group', 'm_a_scale', 'm_a_w_out', 'm_kv_norm_g', 'm_kv_ada_w', 'm_kv_ada_b', 'm_w_kv', 'm_b_w_in', 'm_b_rel_bias', 'm_b_w_out', 'm_final_g', 'v_ada_w', 'v_ada_b', 'v_norm_g', 'v_a_w_in', 'v_a_w_group', 'v_a_scale', 'v_a_w_out', 'v_kv_norm_g', 'v_kv_ada_w', 'v_kv_ada_b', 'v_w_kv', 'v_b_w_in', 'v_b_rel_bias', 'v_b_w_out', 'v_final_g']
TWIN_OUTPUTS = ['loss', 'grad_x', 'grad_ada_w', 'grad_ada_b', 'grad_norm_g', 'grad_a_w_in', 'grad_a_w_group', 'grad_a_scale', 'grad_a_w_out', 'grad_kv_norm_g', 'grad_kv_ada_w', 'grad_kv_ada_b', 'grad_w_kv', 'grad_b_w_in', 'grad_b_rel_bias', 'grad_b_w_out', 'grad_final_g', 'delta_ada_w', 'delta_ada_b', 'delta_norm_g', 'delta_a_w_in', 'delta_a_w_group', 'delta_a_scale', 'delta_a_w_out', 'delta_kv_norm_g', 'delta_kv_ada_w', 'delta_kv_ada_b', 'delta_w_kv', 'delta_b_w_in', 'delta_b_rel_bias', 'delta_b_w_out', 'delta_final_g', 'new_m_ada_w', 'new_m_ada_b', 'new_m_norm_g', 'new_m_a_w_in', 'new_m_a_w_group', 'new_m_a_scale', 'new_m_a_w_out', 'new_m_kv_norm_g', 'new_m_kv_ada_w', 'new_m_kv_ada_b', 'new_m_w_kv', 'new_m_b_w_in', 'new_m_b_rel_bias', 'new_m_b_w_out', 'new_m_final_g', 'new_v_ada_w', 'new_v_ada_b', 'new_v_norm_g', 'new_v_a_w_in', 'new_v_a_w_group', 'new_v_a_scale', 'new_v_a_w_out', 'new_v_kv_norm_g', 'new_v_kv_ada_w', 'new_v_kv_ada_b', 'new_v_w_kv', 'new_v_b_w_in', 'new_v_b_rel_bias', 'new_v_b_w_out', 'new_v_final_g']
TWIN_LEAF_KINDS = {'loss': 'loss', 'grad_x': 'grad_x', 'grad_ada_w': 'grad_w', 'grad_ada_b': 'grad_w', 'grad_norm_g': 'grad_w', 'grad_a_w_in': 'grad_w', 'grad_a_w_group': 'grad_w', 'grad_a_scale': 'grad_w', 'grad_a_w_out': 'grad_w', 'grad_kv_norm_g': 'grad_w', 'grad_kv_ada_w': 'grad_w', 'grad_kv_ada_b': 'grad_w', 'grad_w_kv': 'grad_w', 'grad_b_w_in': 'grad_w', 'grad_b_rel_bias': 'grad_w', 'grad_b_w_out': 'grad_w', 'grad_final_g': 'grad_w', 'delta_ada_w': 'delta_w', 'delta_ada_b': 'delta_w', 'delta_norm_g': 'delta_w', 'delta_a_w_in': 'delta_w', 'delta_a_w_group': 'delta_w', 'delta_a_scale': 'delta_w', 'delta_a_w_out': 'delta_w', 'delta_kv_norm_g': 'delta_w', 'delta_kv_ada_w': 'delta_w', 'delta_kv_ada_b': 'delta_w', 'delta_w_kv': 'delta_w', 'delta_b_w_in': 'delta_w', 'delta_b_rel_bias': 'delta_w', 'delta_b_w_out': 'delta_w', 'delta_final_g': 'delta_w', 'new_m_ada_w': 'new_m', 'new_m_ada_b': 'new_m', 'new_m_norm_g': 'new_m', 'new_m_a_w_in': 'new_m', 'new_m_a_w_group': 'new_m', 'new_m_a_scale': 'new_m', 'new_m_a_w_out': 'new_m', 'new_m_kv_norm_g': 'new_m', 'new_m_kv_ada_w': 'new_m', 'new_m_kv_ada_b': 'new_m', 'new_m_w_kv': 'new_m', 'new_m_b_w_in': 'new_m', 'new_m_b_rel_bias': 'new_m', 'new_m_b_w_out': 'new_m', 'new_m_final_g': 'new_m', 'new_v_ada_w': 'new_v', 'new_v_ada_b': 'new_v', 'new_v_norm_g': 'new_v', 'new_v_a_w_in': 'new_v', 'new_v_a_w_group': 'new_v', 'new_v_a_scale': 'new_v', 'new_v_a_w_out': 'new_v', 'new_v_kv_norm_g': 'new_v', 'new_v_kv_ada_w': 'new_v', 'new_v_kv_ada_b': 'new_v', 'new_v_w_kv': 'new_v', 'new_v_b_w_in': 'new_v', 'new_v_b_rel_bias': 'new_v', 'new_v_b_w_out': 'new_v', 'new_v_final_g': 'new_v'}


def _forward(args):
    return _fwd_reference(*[args[k] for k in FWD_PARAMS])


def _output_shape():
    def fwd():
        inp = _fwd_setup_inputs(0)
        return _fwd_reference(*[inp[k] for k in FWD_PARAMS])
    out = _jax.eval_shape(fwd)
    return out.shape, out.dtype

N_MICROBATCH = 1
ADAM_LR = 0.001
ADAM_B1 = 0.9
ADAM_B2 = 0.999
ADAM_EPS = 1e-08
ADAM_WD = 0.01
ADAM_STEP = 10
PER_EXAMPLE_BATCH_AXIS = {'x': 0, 'c': 0, 'loss_target': 0}
SHARED_INPUTS = []
_WEIGHT_DTYPES = {'ada_w': _jnp.float32, 'ada_b': _jnp.float32, 'norm_g': _jnp.float32, 'a_w_in': _jnp.float32, 'a_w_group': _jnp.float32, 'a_scale': _jnp.float32, 'a_w_out': _jnp.float32, 'kv_norm_g': _jnp.float32, 'kv_ada_w': _jnp.float32, 'kv_ada_b': _jnp.float32, 'w_kv': _jnp.float32, 'b_w_in': _jnp.float32, 'b_rel_bias': _jnp.float32, 'b_w_out': _jnp.float32, 'final_g': _jnp.float32}
MOMENT_SCALE = {'ada_w': 1.104178e-01, 'ada_b': 2.256744e-01, 'norm_g': 1.250620e-01, 'a_w_in': 8.539007e-02, 'a_w_group': 8.393865e-02, 'a_scale': 8.439752e-02, 'a_w_out': 1.187944e-01, 'kv_norm_g': 6.481398e-02, 'kv_ada_w': 6.423805e-02, 'kv_ada_b': 1.113073e-01, 'w_kv': 5.319115e-02, 'b_w_in': 3.821527e-02, 'b_rel_bias': 1.119486e-02, 'b_w_out': 6.645804e-02, 'final_g': 6.460660e+01}


def _to_microbatches(a, axis):
    t = _jnp.moveaxis(a, axis, 0)
    t = t.reshape((N_MICROBATCH, t.shape[0] // N_MICROBATCH) + t.shape[1:])
    return _jnp.moveaxis(t, 1, axis + 1)


def setup_inputs(seed: int = 0) -> dict:
    inp = _fwd_setup_inputs(seed)
    key = _jax.random.fold_in(_jax.random.key(seed), 7919)
    shape, _ = _output_shape()
    out = dict(inp)
    out["loss_target"] = _jax.random.normal(_jax.random.fold_in(key, 0), shape, _jnp.float32)
    for i, name in enumerate(TWIN_WEIGHTS):
        w = inp[name].astype(_jnp.float32)
        if MOMENT_SCALE is None:
            s = _jnp.sqrt(_jnp.mean(_jnp.square(w)) + 1e-30)
        else:
            s = MOMENT_SCALE[name]
        km, kv = _jax.random.split(_jax.random.fold_in(key, i + 1))
        out[name] = w
        out["m_" + name] = s * _jax.random.normal(km, w.shape, _jnp.float32)
        out["v_" + name] = (s * s) * _jax.random.uniform(kv, w.shape, _jnp.float32, 0.5, 1.5)
    if N_MICROBATCH > 1:
        for name, axis in PER_EXAMPLE_BATCH_AXIS.items():
            out[name] = _to_microbatches(out[name], axis)
    return {'x': out['x'], 'c': out['c'], 'ada_w': out['ada_w'], 'ada_b': out['ada_b'], 'norm_g': out['norm_g'], 'a_w_in': out['a_w_in'], 'a_w_group': out['a_w_group'], 'a_scale': out['a_scale'], 'a_w_out': out['a_w_out'], 'kv_norm_g': out['kv_norm_g'], 'kv_ada_w': out['kv_ada_w'], 'kv_ada_b': out['kv_ada_b'], 'w_kv': out['w_kv'], 'b_w_in': out['b_w_in'], 'b_rel_bias': out['b_rel_bias'], 'b_w_out': out['b_w_out'], 'final_g': out['final_g'], 'loss_target': out['loss_target'], 'm_ada_w': out['m_ada_w'], 'm_ada_b': out['m_ada_b'], 'm_norm_g': out['m_norm_g'], 'm_a_w_in': out['m_a_w_in'], 'm_a_w_group': out['m_a_w_group'], 'm_a_scale': out['m_a_scale'], 'm_a_w_out': out['m_a_w_out'], 'm_kv_norm_g': out['m_kv_norm_g'], 'm_kv_ada_w': out['m_kv_ada_w'], 'm_kv_ada_b': out['m_kv_ada_b'], 'm_w_kv': out['m_w_kv'], 'm_b_w_in': out['m_b_w_in'], 'm_b_rel_bias': out['m_b_rel_bias'], 'm_b_w_out': out['m_b_w_out'], 'm_final_g': out['m_final_g'], 'v_ada_w': out['v_ada_w'], 'v_ada_b': out['v_ada_b'], 'v_norm_g': out['v_norm_g'], 'v_a_w_in': out['v_a_w_in'], 'v_a_w_group': out['v_a_w_group'], 'v_a_scale': out['v_a_scale'], 'v_a_w_out': out['v_a_w_out'], 'v_kv_norm_g': out['v_kv_norm_g'], 'v_kv_ada_w': out['v_kv_ada_w'], 'v_kv_ada_b': out['v_kv_ada_b'], 'v_w_kv': out['v_w_kv'], 'v_b_w_in': out['v_b_w_in'], 'v_b_rel_bias': out['v_b_rel_bias'], 'v_b_w_out': out['v_b_w_out'], 'v_final_g': out['v_final_g']}


def _loss(weights, diff, rest, loss_target):
    with _jax.named_scope("forward"):
        args = {**rest, TWIN_DIFF_INPUT: diff, **{k: w.astype(_WEIGHT_DTYPES[k]) for k, w in weights.items()}}
        y = _forward(args)
    with _jax.named_scope("loss_head"):
        err = _jnp.square(y.astype(_jnp.float32) - loss_target)
        return 0.5 * _jnp.sum(_jnp.mean(err, axis=-1)) if err.ndim else 0.5 * err


def _adamw(w, g, m, v):
    m = ADAM_B1 * m + (1.0 - ADAM_B1) * g
    v = ADAM_B2 * v + (1.0 - ADAM_B2) * _jnp.square(g)
    m_hat = m / (1.0 - ADAM_B1 ** ADAM_STEP)
    v_hat = v / (1.0 - ADAM_B2 ** ADAM_STEP)
    delta = -ADAM_LR * (m_hat / (_jnp.sqrt(v_hat) + ADAM_EPS) + ADAM_WD * w)
    return delta, m, v


def reference(x, c, ada_w, ada_b, norm_g, a_w_in, a_w_group, a_scale, a_w_out, kv_norm_g, kv_ada_w, kv_ada_b, w_kv, b_w_in, b_rel_bias, b_w_out, final_g, loss_target, m_ada_w, m_ada_b, m_norm_g, m_a_w_in, m_a_w_group, m_a_scale, m_a_w_out, m_kv_norm_g, m_kv_ada_w, m_kv_ada_b, m_w_kv, m_b_w_in, m_b_rel_bias, m_b_w_out, m_final_g, v_ada_w, v_ada_b, v_norm_g, v_a_w_in, v_a_w_group, v_a_scale, v_a_w_out, v_kv_norm_g, v_kv_ada_w, v_kv_ada_b, v_w_kv, v_b_w_in, v_b_rel_bias, v_b_w_out, v_final_g):
    given = dict(x=x, c=c, ada_w=ada_w, ada_b=ada_b, norm_g=norm_g, a_w_in=a_w_in, a_w_group=a_w_group, a_scale=a_scale, a_w_out=a_w_out, kv_norm_g=kv_norm_g, kv_ada_w=kv_ada_w, kv_ada_b=kv_ada_b, w_kv=w_kv, b_w_in=b_w_in, b_rel_bias=b_rel_bias, b_w_out=b_w_out, final_g=final_g, loss_target=loss_target, m_ada_w=m_ada_w, m_ada_b=m_ada_b, m_norm_g=m_norm_g, m_a_w_in=m_a_w_in, m_a_w_group=m_a_w_group, m_a_scale=m_a_scale, m_a_w_out=m_a_w_out, m_kv_norm_g=m_kv_norm_g, m_kv_ada_w=m_kv_ada_w, m_kv_ada_b=m_kv_ada_b, m_w_kv=m_w_kv, m_b_w_in=m_b_w_in, m_b_rel_bias=m_b_rel_bias, m_b_w_out=m_b_w_out, m_final_g=m_final_g, v_ada_w=v_ada_w, v_ada_b=v_ada_b, v_norm_g=v_norm_g, v_a_w_in=v_a_w_in, v_a_w_group=v_a_w_group, v_a_scale=v_a_scale, v_a_w_out=v_a_w_out, v_kv_norm_g=v_kv_norm_g, v_kv_ada_w=v_kv_ada_w, v_kv_ada_b=v_kv_ada_b, v_w_kv=v_w_kv, v_b_w_in=v_b_w_in, v_b_rel_bias=v_b_rel_bias, v_b_w_out=v_b_w_out, v_final_g=v_final_g)
    weights = {n: given[n] for n in TWIN_WEIGHTS}
    shared = {n: given[n] for n in SHARED_INPUTS}
    per_example = {n: given[n] for n in ['x', 'c']}
    grad_fn = _jax.value_and_grad(_loss, argnums=(0, 1))

    def one_microbatch(ex, loss_target):
        ex = dict(ex)
        diff = ex.pop(TWIN_DIFF_INPUT)
        return grad_fn(weights, diff, {**shared, **ex}, loss_target)

    if N_MICROBATCH == 1:
        loss, (grad_w, grad_x) = one_microbatch(per_example, given["loss_target"])
    else:
        def body(carry, xs):
            loss_sum, grad_sum = carry
            l_k, (gw_k, gx_k) = one_microbatch(xs[0], xs[1])
            with _jax.named_scope("update"):
                return (loss_sum + l_k, _jax.tree.map(_jnp.add, grad_sum, gw_k)), gx_k

        init = (_jnp.zeros((), _jnp.float32), _jax.tree.map(_jnp.zeros_like, weights))
        (loss, grad_w), grad_x = _jax.lax.scan(body, init, (per_example, given["loss_target"]))
    with _jax.named_scope("update"):
        delta_w, new_m, new_v = {}, {}, {}
        for n in TWIN_WEIGHTS:
            delta_w[n], new_m[n], new_v[n] = _adamw(weights[n], grad_w[n], given["m_" + n], given["v_" + n])
    return (loss, grad_x, *[grad_w[n] for n in TWIN_WEIGHTS], *[delta_w[n] for n in TWIN_WEIGHTS],
            *[new_m[n] for n in TWIN_WEIGHTS], *[new_v[n] for n in TWIN_WEIGHTS])
```

```python
import functools

import jax
import jax.numpy as jnp
from jax import lax
from jax.experimental import pallas as pl
from jax.experimental.pallas import tpu as pltpu

F32 = jnp.float32
BF16 = jnp.bfloat16

N_DEV = 8
CHUNK = 64
LEFT_CHUNKS = 8
N_HEADS = 16
POOL_WINDOWS = (2, 4, 8, 16)
REL_CLIP = 128
EPS = 1e-6
ADAM_LR = 0.001
ADAM_B1 = 0.9
ADAM_B2 = 0.999
ADAM_EPS = 1e-08
ADAM_WD = 0.01
ADAM_STEP = 10

LANES = 128
TQ = 2 * CHUNK
LEFT = LEFT_CHUNKS * CHUNK
BAND = LEFT + TQ
FRAME = BAND + LEFT
FRAME_PAD = FRAME + TQ
POOL_HALO = 16
REL_PAD = 384
NEG = -1e30
VMEM_LIMIT = 56 * 1024 * 1024

NT = (((1,), (1,)), ((), ()))
TN = (((0,), (0,)), ((), ()))

HBM_SPEC = pl.BlockSpec(memory_space=pltpu.HBM)


def _cp(*sem):
    return pltpu.CompilerParams(dimension_semantics=sem or None, vmem_limit_bytes=VMEM_LIMIT)


def _sds(shape, dtype):
    return jax.ShapeDtypeStruct(tuple(shape), dtype)


def _row_tile(s, want):
    return min(want, s)


def _vec(d):
    return pl.BlockSpec((1, d), lambda *_: (0, 0))


def _sigmoid(z):
    return 1.0 / (1.0 + jnp.exp(-z))


def _exchange(xs, gather, name):
    n = len(xs)
    out_shapes = []
    for x, ga in zip(xs, gather):
        out_shapes.append(_sds((N_DEV,) + (x.shape if ga else x.shape[1:]), x.dtype))

    def body(*refs):
        ins, outs = refs[:n], refs[n:2 * n]
        send_sems, recv_sems, local_sems = refs[2 * n:]
        mx, my, mc = lax.axis_index("x"), lax.axis_index("y"), lax.axis_index("c")
        me = 4 * mx + 2 * my + mc
        local = []
        for k in range(n):
            src = ins[k] if gather[k] else ins[k].at[me]
            cp = pltpu.make_async_copy(src, outs[k].at[me], local_sems.at[k])
            cp.start()
            local.append(cp)
        sends, recvs = [], []
        for r in range(1, N_DEV):
            px, py, pc = (mx + (r >> 2)) % 2, (my + ((r >> 1) & 1)) % 2, (mc + (r & 1)) % 2
            peer = 4 * px + 2 * py + pc
            for k in range(n):
                sem = k * (N_DEV - 1) + r - 1
                src = ins[k] if gather[k] else ins[k].at[peer]
                send = pltpu.make_async_remote_copy(
                    src_ref=src, dst_ref=outs[k].at[me], send_sem=send_sems.at[sem], recv_sem=recv_sems.at[sem],
                    device_id=(px, py, pc), device_id_type=pl.DeviceIdType.MESH)
                send.start()
                sends.append(send)
                recvs.append(pltpu.make_async_remote_copy(
                    src_ref=src, dst_ref=outs[k].at[peer], send_sem=send_sems.at[sem], recv_sem=recv_sems.at[sem],
                    device_id=(px, py, pc), device_id_type=pl.DeviceIdType.MESH))
        for cp in recvs:
            cp.wait_recv()
        for cp in sends:
            cp.wait_send()
        for cp in local:
            cp.wait()

    return pl.pallas_call(
        body, name=name, out_shape=out_shapes,
        in_specs=[HBM_SPEC] * n, out_specs=[HBM_SPEC] * n,
        scratch_shapes=[pltpu.SemaphoreType.DMA((n * (N_DEV - 1),)), pltpu.SemaphoreType.DMA((n * (N_DEV - 1),)),
                        pltpu.SemaphoreType.DMA((n,))],
    )(*xs)


def _mods_fwd(c_all, ada_w, ada_b_loc, kv_ada_w, kv_ada_b_loc):
    nl, d, wa = ada_w.shape
    wk = kv_ada_w.shape[1]

    def body(c_ref, w_ref, b_ref, kw_ref, kb_ref, o_ref):
        c = c_ref[...]
        ca = c * _sigmoid(c)
        for l in range(nl):
            o_ref[:, l * wa:(l + 1) * wa] = jnp.dot(
                ca, w_ref[l], preferred_element_type=F32, precision=lax.Precision.HIGHEST) + b_ref[l]
        o_ref[:, nl * wa:] = jnp.dot(
            ca, kw_ref[...], preferred_element_type=F32, precision=lax.Precision.HIGHEST) + kb_ref[...]

    return pl.pallas_call(body, name="mods_fwd", out_shape=_sds((N_DEV, nl * wa + wk), F32),
                          compiler_params=_cp())(c_all, ada_w, ada_b_loc, kv_ada_w, kv_ada_b_loc)


def _mods_bwd(c_all, dm, nl, wa, wk):
    d = c_all.shape[1]

    def body(c_ref, d_ref, gw_ref, gk_ref):
        c = c_ref[...]
        ca = c * _sigmoid(c)
        for l in range(nl):
            gw_ref[l] = lax.dot_general(ca, d_ref[:, l * wa:(l + 1) * wa], TN,
                                        preferred_element_type=F32, precision=lax.Precision.HIGHEST)
        gk_ref[...] = lax.dot_general(ca, d_ref[:, nl * wa:], TN,
                                      preferred_element_type=F32, precision=lax.Precision.HIGHEST)

    return pl.pallas_call(body, name="mods_bwd", out_shape=[_sds((nl, d, wa), F32), _sds((d, wk), F32)],
                          compiler_params=_cp())(c_all, dm)


def _norm_mod(hf, g, shift, scale):
    rs = lax.rsqrt(jnp.mean(hf * hf, axis=-1, keepdims=True) + EPS)
    xhat = hf * rs
    n = xhat * g
    return rs, xhat, n, n * (1.0 + scale) + shift


def _inproj_fwd(h, g, shift, scale, w, layer, name):
    s, d = h.shape
    nj, _, _, wn = w.shape
    half = nj // 2
    tm = _row_tile(s, 512)

    def body(h_ref, g_ref, sh_ref, sc_ref, w_ref, u_ref, l_ref, r_ref):
        _, _, _, u = _norm_mod(h_ref[...], g_ref[...], sh_ref[...], sc_ref[...])
        ub = u.astype(BF16)
        u_ref[...] = ub
        for j in range(nj):
            o_ref = l_ref if j < half else r_ref
            jj = j % half
            o_ref[:, jj * wn:(jj + 1) * wn] = jnp.dot(ub, w_ref[j], preferred_element_type=F32).astype(BF16)

    e = half * wn
    return pl.pallas_call(
        body, name=name, grid=(s // tm,),
        in_specs=[pl.BlockSpec((tm, d), lambda i: (i, 0)), _vec(d), _vec(d), _vec(d),
                  pl.BlockSpec((nj, None, d, wn), lambda i: (0, layer, 0, 0))],
        out_specs=[pl.BlockSpec((tm, d), lambda i: (i, 0)), pl.BlockSpec((tm, e), lambda i: (i, 0)),
                   pl.BlockSpec((tm, e), lambda i: (i, 0))],
        out_shape=[_sds((s, d), BF16), _sds((s, e), BF16), _sds((s, e), BF16)],
        compiler_params=_cp("arbitrary"),
    )(h, g, shift, scale, w)


def _pool_fwd(val, wg, layer):
    s, e = val.shape
    ng = len(POOL_WINDOWS)
    gw = e // ng
    tm = _row_tile(s, 256)
    hb = tm // POOL_HALO

    def body(v_ref, halo_ref, w_ref, p_ref, m_ref):
        i = pl.program_id(0)
        t = i * tm + lax.broadcasted_iota(jnp.int32, (tm, 1), 0)
        for g, wdw in enumerate(POOL_WINDOWS):
            cols = slice(g * gw, (g + 1) * gw)
            v = v_ref[:, cols].astype(F32)
            halo = jnp.where(i > 0, halo_ref[:, cols].astype(F32), 0.0)
            acc = jnp.concatenate([halo, v], axis=0)
            sh = 1
            while sh < wdw:
                acc = acc + pltpu.roll(acc, sh, 0)
                sh *= 2
            cnt = jnp.minimum(t + 1, wdw).astype(F32)
            pb = (acc[POOL_HALO:, :] / cnt - v).astype(BF16)
            p_ref[:, cols] = pb
            m_ref[:, cols] = jnp.dot(pb, w_ref[:, g].reshape(gw, gw), preferred_element_type=F32).astype(BF16)

    return pl.pallas_call(
        body, name="pool_fwd", grid=(s // tm,),
        in_specs=[pl.BlockSpec((tm, e), lambda i: (i, 0)),
                  pl.BlockSpec((POOL_HALO, e), lambda i: (jnp.maximum(i * hb - 1, 0), 0)),
                  pl.BlockSpec((N_DEV, None, ng, gw // N_DEV, gw), lambda i: (0, layer, 0, 0, 0))],
        out_specs=[pl.BlockSpec((tm, e), lambda i: (i, 0)), pl.BlockSpec((tm, e), lambda i: (i, 0))],
        out_shape=[_sds((s, e), BF16), _sds((s, e), BF16)],
        compiler_params=_cp("arbitrary"),
    )(val, val, wg)


def _gate_out_fwd(m, z, ascale, w, layer, h, gate, name):
    s, e = m.shape
    d = h.shape[1]
    tm = _row_tile(s, 256)

    def body(m_ref, z_ref, a_ref, w_ref, h_ref, g_ref, gd_ref, y_ref, ho_ref):
        z = z_ref[...].astype(F32)
        gb = ((m_ref[...].astype(F32) * a_ref[...]) * (z * _sigmoid(z))).astype(BF16)
        gd_ref[...] = gb
        y = jnp.dot(gb, w_ref[...].reshape(e, d), preferred_element_type=F32)
        y_ref[...] = y.astype(BF16)
        ho_ref[...] = h_ref[...] + g_ref[...] * y

    return pl.pallas_call(
        body, name=name, grid=(s // tm,),
        in_specs=[pl.BlockSpec((tm, e), lambda i: (i, 0)), pl.BlockSpec((tm, e), lambda i: (i, 0)), _vec(e),
                  pl.BlockSpec((N_DEV, None, e // N_DEV, d), lambda i: (0, layer, 0, 0)),
                  pl.BlockSpec((tm, d), lambda i: (i, 0)), _vec(d)],
        out_specs=[pl.BlockSpec((tm, e), lambda i: (i, 0)), pl.BlockSpec((tm, d), lambda i: (i, 0)),
                   pl.BlockSpec((tm, d), lambda i: (i, 0))],
        out_shape=[_sds((s, e), BF16), _sds((s, d), BF16), _sds((s, d), F32)],
        compiler_params=_cp("arbitrary"),
    )(m, z, ascale, w, h, gate)


def _rel_onehot(shape, r_axis):
    r = lax.broadcasted_iota(jnp.int32, shape, r_axis)
    j = lax.broadcasted_iota(jnp.int32, shape, 1 - r_axis)
    dist = LEFT - (j - TQ)
    return (jnp.clip(dist, -REL_CLIP, REL_CLIP) + REL_CLIP == r).astype(F32)


def _skew(x, sign):
    row = lax.broadcasted_iota(jnp.int32, x.shape, 0)
    for b in range(TQ.bit_length() - 1):
        amt = (1 << b) if sign > 0 else FRAME_PAD - (1 << b)
        x = jnp.where(((row >> b) & 1) == 1, pltpu.roll(x, amt, 1), x)
    return x


def _bias_frames(rel):
    nh = rel.shape[0]

    def body(r_ref, f_ref, ft_ref):
        fext = jnp.dot(r_ref[...], _rel_onehot((REL_PAD, FRAME_PAD), 0), preferred_element_type=F32,
                       precision=lax.Precision.HIGHEST)
        x = _skew(jnp.broadcast_to(fext, (TQ, FRAME_PAD)), 1)[:, TQ:]
        qc = lax.broadcasted_iota(jnp.int32, (TQ, FRAME), 0) // CHUNK
        mc = lax.broadcasted_iota(jnp.int32, (TQ, FRAME), 1) // CHUNK
        x = jnp.where((mc >= qc) & (mc <= qc + LEFT_CHUNKS), x, NEG)
        f_ref[...] = x
        ft_ref[...] = x.T

    return pl.pallas_call(
        body, name="bias_frames", grid=(nh,),
        in_specs=[pl.BlockSpec((None, 1, REL_PAD), lambda h: (h, 0, 0))],
        out_specs=[pl.BlockSpec((None, TQ, FRAME), lambda h: (h, 0, 0)),
                   pl.BlockSpec((None, FRAME, TQ), lambda h: (h, 0, 0))],
        out_shape=[_sds((nh, TQ, FRAME), F32), _sds((nh, FRAME, TQ), F32)],
        compiler_params=_cp("arbitrary"),
    )(rel)


def _bias_bwd(dft):
    nh = dft.shape[0]

    def body(d_ref, o_ref):
        x = jnp.concatenate([jnp.zeros((TQ, TQ), F32), d_ref[...].T], axis=1)
        col = jnp.sum(_skew(x, -1), axis=0, keepdims=True)
        o_ref[...] = jnp.dot(col, _rel_onehot((FRAME_PAD, REL_PAD), 1), preferred_element_type=F32,
                             precision=lax.Precision.HIGHEST)

    return pl.pallas_call(
        body, name="bias_bwd", grid=(nh,),
        in_specs=[pl.BlockSpec((None, FRAME, TQ), lambda h: (h, 0, 0))],
        out_specs=pl.BlockSpec((None, 1, REL_PAD), lambda h: (h, 0, 0)),
        out_shape=_sds((nh, 1, REL_PAD), F32),
        compiler_params=_cp("arbitrary"),
    )(dft)


def _band(gi):
    bs = jnp.maximum(gi * TQ - LEFT, 0)
    off = LEFT - (gi * TQ - bs)
    return pl.multiple_of(bs, TQ), pl.multiple_of(off, TQ)


def _attn_tiles(s):
    nt = s // TQ
    ni = min(8, nt)
    return nt // ni, ni


def _attn_fwd(q, k, v, frame):
    s, e = q.shape
    dh = e // N_HEADS
    ng, ni = _attn_tiles(s)
    sm = dh ** -0.5

    def body(q_ref, k_ref, v_ref, b_ref, o_ref):
        i = pl.program_id(1)

        def tile(t, carry):
            bs, off = _band(i * ni + t)
            rows = pl.ds(pl.multiple_of(t * TQ, TQ), TQ)
            kb = k_ref[pl.ds(bs, BAND), :]
            sc = lax.dot_general(q_ref[rows, :], kb, NT, preferred_element_type=F32) * sm + b_ref[:, pl.ds(off, BAND)]
            p = jnp.exp(sc - jnp.max(sc, axis=-1, keepdims=True))
            l = jnp.sum(p, axis=-1, keepdims=True)
            o = jnp.dot(p.astype(BF16), v_ref[pl.ds(bs, BAND), :], preferred_element_type=F32)
            o_ref[rows, :] = (o / l).astype(BF16)
            return carry

        lax.fori_loop(0, ni, tile, 0)

    return pl.pallas_call(
        body, name="attn_fwd", grid=(N_HEADS, ng),
        in_specs=[pl.BlockSpec((ni * TQ, dh), lambda h, i: (i, h)),
                  pl.BlockSpec((s, dh), lambda h, i: (0, h)), pl.BlockSpec((s, dh), lambda h, i: (0, h)),
                  pl.BlockSpec((None, TQ, FRAME), lambda h, i: (h, 0, 0))],
        out_specs=pl.BlockSpec((ni * TQ, dh), lambda h, i: (i, h)),
        out_shape=_sds((s, e), BF16),
        compiler_params=_cp("arbitrary", "arbitrary"),
    )(q, k, v, frame)


def _final_loss(h, g, target):
    s, d = h.shape
    tm = _row_tile(s, 512)

    def body(h_ref, g_ref, t_ref, dh_ref, dg_ref, l_ref):
        @pl.when(pl.program_id(0) == 0)
        def _():
            dg_ref[...] = jnp.zeros_like(dg_ref)
            l_ref[...] = jnp.zeros_like(l_ref)

        hf = h_ref[...]
        gg = g_ref[...]
        rs = lax.rsqrt(jnp.mean(hf * hf, axis=-1, keepdims=True) + EPS)
        xhat = hf * rs
        diff = xhat * gg - t_ref[...]
        l_ref[...] += 0.5 * jnp.sum(jnp.mean(diff * diff, axis=-1, keepdims=True), axis=0, keepdims=True)
        dout = diff * (1.0 / d)
        dg_ref[...] += jnp.sum(dout * xhat, axis=0, keepdims=True)
        dxh = dout * gg
        dh_ref[...] = rs * (dxh - xhat * jnp.mean(dxh * xhat, axis=-1, keepdims=True))

    return pl.pallas_call(
        body, name="final_loss", grid=(s // tm,),
        in_specs=[pl.BlockSpec((tm, d), lambda i: (i, 0)), _vec(d), pl.BlockSpec((tm, d), lambda i: (i, 0))],
        out_specs=[pl.BlockSpec((tm, d), lambda i: (i, 0)), _vec(d), _vec(LANES)],
        out_shape=[_sds((s, d), F32), _sds((1, d), F32), _sds((1, LANES), F32)],
        compiler_params=_cp("arbitrary"),
    )(h, g, target)


def _out_bwd(dh, y, gate, w, layer, m, z, ascale, name):
    s, d = dh.shape
    e = m.shape[1]
    tm = _row_tile(s, 256)

    def body(dh_ref, y_ref, g_ref, w_ref, m_ref, z_ref, a_ref, dy_ref, dm_ref, dz_ref, dg_ref, da_ref):
        @pl.when(pl.program_id(0) == 0)
        def _():
            dg_ref[...] = jnp.zeros_like(dg_ref)
            da_ref[...] = jnp.zeros_like(da_ref)

        dhf = dh_ref[...]
        dg_ref[...] += jnp.sum(dhf * y_ref[...].astype(F32), axis=0, keepdims=True)
        dyb = (g_ref[...] * dhf).astype(BF16)
        dy_ref[...] = dyb
        dgated = lax.dot_general(dyb, w_ref[...].reshape(e, d), NT, preferred_element_type=F32)
        z = z_ref[...].astype(F32)
        sig = _sigmoid(z)
        mf = m_ref[...].astype(F32)
        a = a_ref[...]
        dms = dgated * (z * sig)
        da_ref[...] += jnp.sum(dms * mf, axis=0, keepdims=True)
        dm_ref[...] = (dms * a).astype(BF16)
        dz_ref[...] = (dgated * (mf * a) * (sig * (1.0 + z * (1.0 - sig)))).astype(BF16)

    return pl.pallas_call(
        body, name=name, grid=(s // tm,),
        in_specs=[pl.BlockSpec((tm, d), lambda i: (i, 0)), pl.BlockSpec((tm, d), lambda i: (i, 0)), _vec(d),
                  pl.BlockSpec((N_DEV, None, e // N_DEV, d), lambda i: (0, layer, 0, 0)),
                  pl.BlockSpec((tm, e), lambda i: (i, 0)), pl.BlockSpec((tm, e), lambda i: (i, 0)), _vec(e)],
        out_specs=[pl.BlockSpec((tm, d), lambda i: (i, 0)), pl.BlockSpec((tm, e), lambda i: (i, 0)),
                   pl.BlockSpec((tm, e), lambda i: (i, 0)), _vec(d), _vec(e)],
        out_shape=[_sds((s, d), BF16), _sds((s, e), BF16), _sds((s, e), BF16), _sds((1, d), F32), _sds((1, e), F32)],
        compiler_params=_cp("arbitrary"),
    )(dh, y, gate, w, m, z, ascale)


def _pool_bwd(dm, wg, layer):
    s, e = dm.shape
    ng = len(POOL_WINDOWS)
    gw = e // ng
    tm = _row_tile(s, 256)
    hb = tm // POOL_HALO
    nsteps = s // tm

    def body(d_ref, halo_ref, w_ref, o_ref):
        i = pl.program_id(0)
        t = i * tm + lax.broadcasted_iota(jnp.int32, (tm + POOL_HALO, 1), 0)
        for g, wdw in enumerate(POOL_WINDOWS):
            cols = slice(g * gw, (g + 1) * gw)
            dmx = jnp.concatenate([d_ref[:, cols], halo_ref[:, cols]], axis=0)
            dp = lax.dot_general(dmx, w_ref[:, g].reshape(gw, gw), NT, preferred_element_type=F32)
            dp = jnp.where(t < s, dp, 0.0)
            acc = dp / jnp.minimum(t + 1, wdw).astype(F32)
            sh = 1
            while sh < wdw:
                acc = acc + pltpu.roll(acc, tm + POOL_HALO - sh, 0)
                sh *= 2
            o_ref[:, cols] = (acc[:tm, :] - dp[:tm, :]).astype(BF16)

    return pl.pallas_call(
        body, name="pool_bwd", grid=(nsteps,),
        in_specs=[pl.BlockSpec((tm, e), lambda i: (i, 0)),
                  pl.BlockSpec((POOL_HALO, e), lambda i: (jnp.minimum((i + 1) * hb, s // POOL_HALO - 1), 0)),
                  pl.BlockSpec((N_DEV, None, ng, gw // N_DEV, gw), lambda i: (0, layer, 0, 0, 0))],
        out_specs=pl.BlockSpec((tm, e), lambda i: (i, 0)),
        out_shape=_sds((s, e), BF16),
        compiler_params=_cp("arbitrary"),
    )(dm, dm, wg)


def _attn_bwd(q, k, v, do, frame_t):
    s, e = q.shape
    dh = e // N_HEADS
    ng, ni = _attn_tiles(s)
    sm = dh ** -0.5

    def body(q_ref, k_ref, v_ref, do_ref, b_ref, dq_ref, dk_ref, dv_ref, db_ref):
        i = pl.program_id(1)

        @pl.when(i == 0)
        def _():
            dk_ref[...] = jnp.zeros_like(dk_ref)
            dv_ref[...] = jnp.zeros_like(dv_ref)
            db_ref[...] = jnp.zeros_like(db_ref)

        def tile(t, carry):
            bs, off = _band(i * ni + t)
            rows = pl.ds(pl.multiple_of(t * TQ, TQ), TQ)
            keys = pl.ds(bs, BAND)
            qb, dob = q_ref[rows, :], do_ref[rows, :]
            kb = k_ref[keys, :]
            st = lax.dot_general(kb, qb, NT, preferred_element_type=F32) * sm + b_ref[pl.ds(off, BAND), :]
            p = jnp.exp(st - jnp.max(st, axis=0, keepdims=True))
            p = p * (1.0 / jnp.sum(p, axis=0, keepdims=True))
            dpt = lax.dot_general(v_ref[keys, :], dob, NT, preferred_element_type=F32)
            dst = p * (dpt - jnp.sum(dpt * p, axis=0, keepdims=True))
            dsb = dst.astype(BF16)
            dv_ref[keys, :] += jnp.dot(p.astype(BF16), dob, preferred_element_type=F32)
            dk_ref[keys, :] += jnp.dot(dsb, qb, preferred_element_type=F32) * sm
            dq_ref[rows, :] = (lax.dot_general(dsb, kb, TN, preferred_element_type=F32) * sm).astype(BF16)
            db_ref[pl.ds(off, BAND), :] += dst
            return carry

        lax.fori_loop(0, ni, tile, 0)

    return pl.pallas_call(
        body, name="attn_bwd", grid=(N_HEADS, ng),
        in_specs=[pl.BlockSpec((ni * TQ, dh), lambda h, i: (i, h)),
                  pl.BlockSpec((s, dh), lambda h, i: (0, h)), pl.BlockSpec((s, dh), lambda h, i: (0, h)),
                  pl.BlockSpec((ni * TQ, dh), lambda h, i: (i, h)),
                  pl.BlockSpec((None, FRAME, TQ), lambda h, i: (h, 0, 0))],
        out_specs=[pl.BlockSpec((ni * TQ, dh), lambda h, i: (i, h)),
                   pl.BlockSpec((s, dh), lambda h, i: (0, h)), pl.BlockSpec((s, dh), lambda h, i: (0, h)),
                   pl.BlockSpec((None, FRAME, TQ), lambda h, i: (h, 0, 0))],
        out_shape=[_sds((s, e), BF16), _sds((s, e), F32), _sds((s, e), F32), _sds((N_HEADS, FRAME, TQ), F32)],
        compiler_params=_cp("arbitrary", "arbitrary"),
    )(q, k, v, do, frame_t)


def _add_cast(a, b):
    s, e = a.shape
    tm = _row_tile(s, 512)

    def body(a_ref, b_ref, o_ref):
        o_ref[...] = (a_ref[...] + b_ref[...]).astype(BF16)

    return pl.pallas_call(
        body, name="add_cast", grid=(s // tm,),
        in_specs=[pl.BlockSpec((tm, e), lambda i: (i, 0))] * 2,
        out_specs=pl.BlockSpec((tm, e), lambda i: (i, 0)), out_shape=_sds((s, e), BF16),
        compiler_params=_cp("arbitrary"),
    )(a, b)


def _in_bwd(dl, dr, w, layer, h, g, scale, dres, name):
    s, d = h.shape
    nj, _, _, wn = w.shape
    half = nj // 2
    e = half * wn
    tm = _row_tile(s, 256)

    def body(dl_ref, dr_ref, w_ref, h_ref, g_ref, sc_ref, res_ref, dh_ref, st_ref):
        @pl.when(pl.program_id(0) == 0)
        def _():
            st_ref[...] = jnp.zeros_like(st_ref)

        du = jnp.zeros((tm, d), F32)
        for j in range(nj):
            src = dl_ref if j < half else dr_ref
            jj = j % half
            du = du + lax.dot_general(src[:, jj * wn:(jj + 1) * wn], w_ref[j], NT, preferred_element_type=F32)
        gg = g_ref[...]
        rs, xhat, n, _ = _norm_mod(h_ref[...], gg, 0.0, 0.0)
        dn = du * (1.0 + sc_ref[...])
        st_ref[0:1, :] += jnp.sum(du, axis=0, keepdims=True)
        st_ref[1:2, :] += jnp.sum(du * n, axis=0, keepdims=True)
        st_ref[2:3, :] += jnp.sum(dn * xhat, axis=0, keepdims=True)
        dxh = dn * gg
        dh_ref[...] = rs * (dxh - xhat * jnp.mean(dxh * xhat, axis=-1, keepdims=True)) + res_ref[...]

    return pl.pallas_call(
        body, name=name, grid=(s // tm,),
        in_specs=[pl.BlockSpec((tm, e), lambda i: (i, 0)), pl.BlockSpec((tm, e), lambda i: (i, 0)),
                  pl.BlockSpec((nj, None, d, wn), lambda i: (0, layer, 0, 0)),
                  pl.BlockSpec((tm, d), lambda i: (i, 0)), _vec(d), _vec(d), pl.BlockSpec((tm, d), lambda i: (i, 0))],
        out_specs=[pl.BlockSpec((tm, d), lambda i: (i, 0)), pl.BlockSpec((8, d), lambda i: (0, 0))],
        out_shape=[_sds((s, d), F32), _sds((8, d), F32)],
        compiler_params=_cp("arbitrary"),
    )(dl, dr, w, h, g, scale, dres)


def _tn_matmul(x, ys, xw, yw, xmap, ymap, out_shape, out_block, out_map, nb, into, name):
    s = x.shape[0]
    ts = _row_tile(s, 512)
    ys = list(ys)
    half = nb // len(ys)

    def body(*refs):
        x_ref, y_refs, o_ref = refs[0], refs[1:1 + len(ys)], refs[-1]
        b = pl.program_id(0)

        @pl.when(pl.program_id(1) == 0)
        def _():
            o_ref[...] = jnp.zeros_like(o_ref)

        for n, y_ref in enumerate(y_refs):
            @pl.when((b >= n * half) & (b < (n + 1) * half))
            def _():
                acc = lax.dot_general(x_ref[...], y_ref[...], TN, preferred_element_type=F32)
                o_ref[...] += acc.reshape(o_ref.shape)

    in_specs = [pl.BlockSpec((ts, xw), lambda b, k: (k, xmap(b)))]
    for n in range(len(ys)):
        in_specs.append(pl.BlockSpec(
            (ts, yw), lambda b, k, n=n: (k, ymap(jnp.clip(b - n * half, 0, half - 1)))))
    args = [x] + ys
    aliases = {}
    if into is not None:
        in_specs.append(HBM_SPEC)
        args.append(into)
        aliases = {len(args) - 1: 0}

        def body_into(*refs):
            body(*refs[:1 + len(ys)], refs[-1])
        fn = body_into
    else:
        fn = body
    return pl.pallas_call(
        fn, name=name, grid=(nb, s // ts), in_specs=in_specs,
        out_specs=pl.BlockSpec(out_block, lambda b, k: out_map(b)), out_shape=_sds(out_shape, F32),
        input_output_aliases=aliases, compiler_params=_cp("arbitrary", "arbitrary"),
    )(*args)


def _grad_w_in(u, dl, dr, layer, nl, into, name):
    d = u.shape[1]
    wn = 2 * dl.shape[1] // N_DEV
    return _tn_matmul(u, (dl, dr), d, wn, lambda b: 0, lambda b: b, (N_DEV, nl, d, wn), (None, None, d, wn),
                      lambda b: (b, layer, 0, 0), N_DEV, into, name)


def _grad_w_out(gated, dy, layer, nl, into, name):
    e, d = gated.shape[1], dy.shape[1]
    return _tn_matmul(gated, (dy,), e // N_DEV, d, lambda b: b, lambda b: 0, (N_DEV, nl, e // N_DEV, d),
                      (None, None, e // N_DEV, d), lambda b: (b, layer, 0, 0), N_DEV, into, name)


def _grad_w_group(pooled, dm, layer, nl, into, name):
    ng = len(POOL_WINDOWS)
    gw = pooled.shape[1] // ng
    return _tn_matmul(pooled, (dm,), gw, gw, lambda b: b, lambda b: b, (N_DEV, nl, ng, gw // N_DEV, gw),
                      (N_DEV, None, None, gw // N_DEV, gw), lambda b: (0, layer, b, 0, 0), ng, into, name)


def _adamw(staged, w, m, v, name):
    shape = w.shape
    n = staged.shape[0]
    cdim = shape[-1]
    rows = 1
    for a in shape[:-1]:
        rows *= a
    st = staged.reshape(n, rows, cdim)
    tr = rows if rows * cdim <= 128 * 1024 else max(8, (128 * 1024 // cdim) // 8 * 8)
    while rows % tr:
        tr -= 8

    def body(s_ref, w_ref, m_ref, v_ref, g_ref, d_ref, mo_ref, vo_ref):
        g = s_ref[0]
        for j in range(1, n):
            g = g + s_ref[j]
        mn = ADAM_B1 * m_ref[...] + (1.0 - ADAM_B1) * g
        vn = ADAM_B2 * v_ref[...] + (1.0 - ADAM_B2) * (g * g)
        m_hat = mn / (1.0 - ADAM_B1 ** ADAM_STEP)
        v_hat = vn / (1.0 - ADAM_B2 ** ADAM_STEP)
        g_ref[...] = g
        d_ref[...] = -ADAM_LR * (m_hat / (jnp.sqrt(v_hat) + ADAM_EPS) + ADAM_WD * w_ref[...])
        mo_ref[...] = mn
        vo_ref[...] = vn

    blk = pl.BlockSpec((tr, cdim), lambda i: (i, 0))
    outs = pl.pallas_call(
        body, name=name, grid=(rows // tr,),
        in_specs=[pl.BlockSpec((n, tr, cdim), lambda i: (0, i, 0)), blk, blk, blk],
        out_specs=[blk] * 4, out_shape=[_sds((rows, cdim), F32)] * 4,
        compiler_params=_cp("arbitrary"),
    )(st, w.reshape(rows, cdim), m.reshape(rows, cdim), v.reshape(rows, cdim))
    return [o.reshape(shape) for o in outs]


def _pack(parts, total):
    flat = jnp.concatenate([p.reshape(-1) for p in parts])
    return jnp.pad(flat, (0, total - flat.shape[0])).reshape(1, total)


def kernel(x, c, ada_w, ada_b, norm_g, a_w_in, a_w_group, a_scale, a_w_out, kv_norm_g, kv_ada_w, kv_ada_b, w_kv, b_w_in, b_rel_bias, b_w_out, final_g, loss_target, m_ada_w, m_ada_b, m_norm_g, m_a_w_in, m_a_w_group, m_a_scale, m_a_w_out, m_kv_norm_g, m_kv_ada_w, m_kv_ada_b, m_w_kv, m_b_w_in, m_b_rel_bias, m_b_w_out, m_final_g, v_ada_w, v_ada_b, v_norm_g, v_a_w_in, v_a_w_group, v_a_scale, v_a_w_out, v_kv_norm_g, v_kv_ada_w, v_kv_ada_b, v_w_kv, v_b_w_in, v_b_rel_bias, v_b_w_out, v_final_g):
    s, d = x.shape[1], x.shape[2]
    depth, _, wa = ada_w.shape
    wk = kv_ada_w.shape[1]
    n_a, n_b = a_w_in.shape[0], b_w_in.shape[0]
    e = a_w_out.shape[1] * N_DEV
    nrel = b_rel_bias.shape[-1]
    me = 4 * lax.axis_index("x") + 2 * lax.axis_index("y") + lax.axis_index("c")
    h0 = x[0]
    target = loss_target[0]

    c_all = _exchange([c.reshape(1, 1, d)], [True], "gather_c")[0].reshape(N_DEV, d)
    ada_b_loc = lax.dynamic_slice(ada_b, (0, me * wa), (depth, wa)).reshape(depth, 1, wa)
    kv_ada_b_loc = lax.dynamic_slice(kv_ada_b, (me * wk,), (wk,)).reshape(1, wk)
    mods_cols = _mods_fwd(c_all, ada_w, ada_b_loc, kv_ada_w, kv_ada_b_loc)
    nm = depth * wa + wk
    got = _exchange(
        [mods_cols.reshape(N_DEV, 1, nm), a_w_in.astype(BF16), a_w_group.astype(BF16), a_w_out.astype(BF16),
         w_kv.astype(BF16).reshape(1, d, -1), b_w_in.astype(BF16), b_w_out.astype(BF16), a_scale],
        [False, True, True, True, True, True, True, True], "gather_weights")
    mods_me, wa_in, wa_group, wa_out, wkv, wb_in, wb_out, a_scale_all = got
    mods_me = mods_me.reshape(N_DEV, nm)
    mods = mods_me[:, :depth * wa].reshape(N_DEV, depth, wa).transpose(1, 0, 2).reshape(depth, 3, 1, d)
    kv_mods = mods_me[:, depth * wa:].reshape(2, 1, d)
    a_scale_full = a_scale_all.transpose(1, 0, 2).reshape(n_a, 1, e)
    ones_e = jnp.ones((1, e), F32)

    saved = []
    h = h0
    k = v = hk = None
    frames = []
    for layer in range(depth):
        shift, scale, gate = mods[layer, 0], mods[layer, 1], mods[layer, 2]
        g = norm_g[layer].reshape(1, d)
        if layer < n_a:
            u, left, z = _inproj_fwd(h, g, shift, scale, wa_in, layer, "inproj_fwd")
            pooled, mixed = _pool_fwd(left, wa_group, layer)
            gated, y, hn = _gate_out_fwd(mixed, z, a_scale_full[layer], wa_out, layer, h, gate, "gate_out_fwd")
            saved.append(dict(h=h, u=u, left=left, z=z, pooled=pooled, m=mixed, gated=gated, y=y))
        else:
            bi = layer - n_a
            if bi == 0:
                hk, k, v = _inproj_fwd(h, kv_norm_g.reshape(1, d), kv_mods[0], kv_mods[1], wkv, 0, "inproj_fwd")
            u, left, z = _inproj_fwd(h, g, shift, scale, wb_in, bi, "inproj_fwd")
            rel = jnp.pad(b_rel_bias[bi], ((0, 0), (0, REL_PAD - nrel))).reshape(N_HEADS, 1, REL_PAD)
            frame, frame_t = _bias_frames(rel)
            att = _attn_fwd(left, k, v, frame)
            gated, y, hn = _gate_out_fwd(att, z, ones_e, wb_out, bi, h, gate, "gate_out_fwd")
            saved.append(dict(h=h, u=u, left=left, z=z, m=att, gated=gated, y=y, frame_t=frame_t))
        h = hn
    dh, d_final_g, loss_part = _final_loss(h, final_g.reshape(1, d), target)

    d_mods = [None] * depth
    d_norm_g = [None] * depth
    d_rel = [None] * n_b
    d_ascale = [None] * n_a
    dk_parts, dv_parts = [], []
    g_b_in = g_b_out = g_a_in = g_a_out = g_a_group = None
    for layer in range(depth - 1, n_a - 1, -1):
        bi = layer - n_a
        sv = saved[layer]
        scale, gate = mods[layer, 1], mods[layer, 2]
        dy, datt, dz, d_gate, _ = _out_bwd(dh, sv["y"], gate, wb_out, bi, sv["m"], sv["z"], ones_e, "out_bwd")
        dq, dk, dv, dframe_t = _attn_bwd(sv["left"], k, v, datt, sv["frame_t"])
        dk_parts.append(dk)
        dv_parts.append(dv)
        d_rel[bi] = _bias_bwd(dframe_t).reshape(N_HEADS, REL_PAD)[:, :nrel]
        g_b_out = _grad_w_out(sv["gated"], dy, bi, n_b, g_b_out, "grad_w_out")
        g_b_in = _grad_w_in(sv["u"], dq, dz, bi, n_b, g_b_in, "grad_w_in")
        dh, st = _in_bwd(dq, dz, wb_in, bi, sv["h"], norm_g[layer].reshape(1, d), scale, dh, "in_bwd")
        d_mods[layer] = jnp.concatenate([st[0], st[1], d_gate[0]])
        d_norm_g[layer] = st[2]
    dkb = _add_cast(dk_parts[0], dk_parts[1])
    dvb = _add_cast(dv_parts[0], dv_parts[1])
    h_kv = saved[n_a]["h"]
    g_w_kv = _grad_w_in(hk, dkb, dvb, 0, 1, None, "grad_w_in")
    dh, st = _in_bwd(dkb, dvb, wkv, 0, h_kv, kv_norm_g.reshape(1, d), kv_mods[1], dh, "in_bwd")
    d_kv_mods = jnp.concatenate([st[0], st[1]])
    d_kv_norm_g = st[2]
    for layer in range(n_a - 1, -1, -1):
        sv = saved[layer]
        scale, gate = mods[layer, 1], mods[layer, 2]
        dy, dm, dz, d_gate, da = _out_bwd(dh, sv["y"], gate, wa_out, layer, sv["m"], sv["z"], a_scale_full[layer],
                                          "out_bwd")
        d_ascale[layer] = da.reshape(N_DEV, e // N_DEV)
        dval = _pool_bwd(dm, wa_group, layer)
        g_a_group = _grad_w_group(sv["pooled"], dm, layer, n_a, g_a_group, "grad_w_group")
        g_a_out = _grad_w_out(sv["gated"], dy, layer, n_a, g_a_out, "grad_w_out")
        g_a_in = _grad_w_in(sv["u"], dval, dz, layer, n_a, g_a_in, "grad_w_in")
        dh, st = _in_bwd(dval, dz, wa_in, layer, sv["h"], norm_g[layer].reshape(1, d), scale, dh, "in_bwd")
        d_mods[layer] = jnp.concatenate([st[0], st[1], d_gate[0]])
        d_norm_g[layer] = st[2]
    grad_x = dh.reshape(1, s, d)

    d_mods = jnp.stack(d_mods)
    dm_slots = jnp.concatenate(
        [d_mods.reshape(depth, N_DEV, wa).transpose(1, 0, 2).reshape(N_DEV, depth * wa), d_kv_mods.reshape(N_DEV, wk)],
        axis=1).reshape(N_DEV, 1, nm)
    small_parts = [d_mods, d_kv_mods, jnp.stack(d_norm_g), d_kv_norm_g, d_final_g, jnp.stack(d_rel), loss_part[0, :1]]
    n_small = sum(int(p.size) for p in small_parts)
    n_small_pad = -(-n_small // LANES) * LANES
    small = _pack(small_parts, n_small_pad).reshape(1, 1, n_small_pad)
    d_ascale_slots = jnp.stack(d_ascale, axis=1)
    got = _exchange(
        [g_a_in, g_a_group, d_ascale_slots, g_a_out, g_w_kv, g_b_in, g_b_out, dm_slots, small],
        [False] * 8 + [True], "exchange_grads")
    s_a_in, s_a_group, s_a_scale, s_a_out, s_w_kv, s_b_in, s_b_out, dm_cols, small_all = got

    g_ada_w, g_kv_ada_w = _mods_bwd(c_all, dm_cols.reshape(N_DEV, nm), depth, wa, wk)
    res = {}
    res["ada_w"] = _adamw(g_ada_w[None], ada_w, m_ada_w, v_ada_w, "adamw")
    res["kv_ada_w"] = _adamw(g_kv_ada_w[None], kv_ada_w, m_kv_ada_w, v_kv_ada_w, "adamw")
    res["a_w_in"] = _adamw(s_a_in, a_w_in, m_a_w_in, v_a_w_in, "adamw")
    res["a_w_group"] = _adamw(s_a_group, a_w_group, m_a_w_group, v_a_w_group, "adamw")
    res["a_scale"] = _adamw(s_a_scale, a_scale, m_a_scale, v_a_scale, "adamw")
    res["a_w_out"] = _adamw(s_a_out, a_w_out, m_a_w_out, v_a_w_out, "adamw")
    res["w_kv"] = _adamw(s_w_kv.reshape((N_DEV,) + w_kv.shape), w_kv, m_w_kv, v_w_kv, "adamw")
    res["b_w_in"] = _adamw(s_b_in, b_w_in, m_b_w_in, v_b_w_in, "adamw")
    res["b_w_out"] = _adamw(s_b_out, b_w_out, m_b_w_out, v_b_w_out, "adamw")
    small_names = ["ada_b", "kv_ada_b", "norm_g", "kv_norm_g", "final_g", "b_rel_bias"]
    small_w = dict(ada_b=ada_b, kv_ada_b=kv_ada_b, norm_g=norm_g, kv_norm_g=kv_norm_g, final_g=final_g,
                   b_rel_bias=b_rel_bias)
    small_m = dict(ada_b=m_ada_b, kv_ada_b=m_kv_ada_b, norm_g=m_norm_g, kv_norm_g=m_kv_norm_g, final_g=m_final_g,
                   b_rel_bias=m_b_rel_bias)
    small_v = dict(ada_b=v_ada_b, kv_ada_b=v_kv_ada_b, norm_g=v_norm_g, kv_norm_g=v_kv_norm_g, final_g=v_final_g,
                   b_rel_bias=v_b_rel_bias)
    sw = _pack([small_w[n] for n in small_names], n_small_pad)
    smm = _pack([small_m[n] for n in small_names], n_small_pad)
    svv = _pack([small_v[n] for n in small_names] + [jnp.ones((n_small_pad - n_small + 1,), F32)], n_small_pad)
    small_out = _adamw(small_all.reshape(N_DEV, 1, n_small_pad), sw, smm, svv, "adamw")
    off = 0
    for n in small_names:
        size = int(small_w[n].size)
        res[n] = [o[0, off:off + size].reshape(small_w[n].shape) for o in small_out]
        off += size
    loss = small_out[0][0, n_small - 1]

    order = ["ada_w", "ada_b", "norm_g", "a_w_in", "a_w_group", "a_scale", "a_w_out", "kv_norm_g", "kv_ada_w",
             "kv_ada_b", "w_kv", "b_w_in", "b_rel_bias", "b_w_out", "final_g"]
    outs = [loss, grad_x]
    for part in range(4):
        outs += [res[n][part] for n in order]
    return tuple(outs)
```

```python
import functools

import jax
import jax.numpy as jnp
from jax import lax
from jax.experimental import pallas as pl
from jax.experimental.pallas import tpu as pltpu

F32 = jnp.float32
BF16 = jnp.bfloat16

N_DEV = 8
CHUNK = 64
LEFT_CHUNKS = 8
N_HEADS = 16
POOL_WINDOWS = (2, 4, 8, 16)
REL_CLIP = 128
EPS = 1e-6
ADAM_LR = 0.001
ADAM_B1 = 0.9
ADAM_B2 = 0.999
ADAM_EPS = 1e-08
ADAM_WD = 0.01
ADAM_STEP = 10

LANES = 128
TQ = 2 * CHUNK
LEFT = LEFT_CHUNKS * CHUNK
BAND = LEFT + TQ
FRAME_PAD = BAND + TQ
NFRAME = LEFT // TQ + 1
POOL_HALO = 16
REL_PAD = 384
NEG = -1e30
ATTN_GROUP = 8
ATTN_BATCH = 4
VMEM_LIMIT = 56 * 1024 * 1024

NT = (((1,), (1,)), ((), ()))
TN = (((0,), (0,)), ((), ()))

HBM_SPEC = pl.BlockSpec(memory_space=pltpu.HBM)


def _cp(*sem):
    return pltpu.CompilerParams(dimension_semantics=sem or None, vmem_limit_bytes=VMEM_LIMIT)


def _sds(shape, dtype):
    return jax.ShapeDtypeStruct(tuple(shape), dtype)


def _row_tile(s, want):
    return min(want, s)


def _vec(d):
    return pl.BlockSpec((1, d), lambda *_: (0, 0))


def _sigmoid(z):
    return 1.0 / (1.0 + jnp.exp(-z))


def _exchange(xs, gather, name):
    n = len(xs)
    out_shapes = []
    for x, ga in zip(xs, gather):
        out_shapes.append(_sds((N_DEV,) + (x.shape if ga else x.shape[1:]), x.dtype))

    def body(*refs):
        ins, outs = refs[:n], refs[n:2 * n]
        send_sems, recv_sems, local_sems = refs[2 * n:]
        mx, my, mc = lax.axis_index("x"), lax.axis_index("y"), lax.axis_index("c")
        me = 4 * mx + 2 * my + mc
        local = []
        for k in range(n):
            src = ins[k] if gather[k] else ins[k].at[me]
            cp = pltpu.make_async_copy(src, outs[k].at[me], local_sems.at[k])
            cp.start()
            local.append(cp)
        sends, recvs = [], []
        for r in range(1, N_DEV):
            px, py, pc = (mx + (r >> 2)) % 2, (my + ((r >> 1) & 1)) % 2, (mc + (r & 1)) % 2
            peer = 4 * px + 2 * py + pc
            for k in range(n):
                sem = k * (N_DEV - 1) + r - 1
                src = ins[k] if gather[k] else ins[k].at[peer]
                send = pltpu.make_async_remote_copy(
                    src_ref=src, dst_ref=outs[k].at[me], send_sem=send_sems.at[sem], recv_sem=recv_sems.at[sem],
                    device_id=(px, py, pc), device_id_type=pl.DeviceIdType.MESH)
                send.start()
                sends.append(send)
                recvs.append(pltpu.make_async_remote_copy(
                    src_ref=src, dst_ref=outs[k].at[peer], send_sem=send_sems.at[sem], recv_sem=recv_sems.at[sem],
                    device_id=(px, py, pc), device_id_type=pl.DeviceIdType.MESH))
        for cp in recvs:
            cp.wait_recv()
        for cp in sends:
            cp.wait_send()
        for cp in local:
            cp.wait()

    return pl.pallas_call(
        body, name=name, out_shape=out_shapes,
        in_specs=[HBM_SPEC] * n, out_specs=[HBM_SPEC] * n,
        scratch_shapes=[pltpu.SemaphoreType.DMA((n * (N_DEV - 1),)), pltpu.SemaphoreType.DMA((n * (N_DEV - 1),)),
                        pltpu.SemaphoreType.DMA((n,))],
    )(*xs)


def _mods_fwd(c_all, ada_w, ada_b_loc, kv_ada_w, kv_ada_b_loc):
    nl, d, wa = ada_w.shape
    wk = kv_ada_w.shape[1]

    def body(c_ref, w_ref, b_ref, kw_ref, kb_ref, o_ref):
        c = c_ref[...]
        ca = c * _sigmoid(c)
        for l in range(nl):
            o_ref[:, l * wa:(l + 1) * wa] = jnp.dot(
                ca, w_ref[l], preferred_element_type=F32, precision=lax.Precision.HIGHEST) + b_ref[l]
        o_ref[:, nl * wa:] = jnp.dot(
            ca, kw_ref[...], preferred_element_type=F32, precision=lax.Precision.HIGHEST) + kb_ref[...]

    return pl.pallas_call(body, name="mods_fwd", out_shape=_sds((N_DEV, nl * wa + wk), F32),
                          compiler_params=_cp())(c_all, ada_w, ada_b_loc, kv_ada_w, kv_ada_b_loc)


def _mods_bwd(c_all, dm, nl, wa, wk):
    d = c_all.shape[1]

    def body(c_ref, d_ref, gw_ref, gk_ref):
        c = c_ref[...]
        ca = c * _sigmoid(c)
        for l in range(nl):
            gw_ref[l] = lax.dot_general(ca, d_ref[:, l * wa:(l + 1) * wa], TN,
                                        preferred_element_type=F32, precision=lax.Precision.HIGHEST)
        gk_ref[...] = lax.dot_general(ca, d_ref[:, nl * wa:], TN,
                                      preferred_element_type=F32, precision=lax.Precision.HIGHEST)

    return pl.pallas_call(body, name="mods_bwd", out_shape=[_sds((nl, d, wa), F32), _sds((d, wk), F32)],
                          compiler_params=_cp())(c_all, dm)


def _norm_mod(hf, g, shift, scale):
    rs = lax.rsqrt(jnp.mean(hf * hf, axis=-1, keepdims=True) + EPS)
    xhat = hf * rs
    n = xhat * g
    return rs, xhat, n, n * (1.0 + scale) + shift


def _inproj_fwd(h, g, shift, scale, w, layer, name, pad=0):
    s, d = h.shape
    nj, _, _, wn = w.shape
    half = nj // 2
    tm = _row_tile(s, 512)
    assert pad in (0, tm)
    pb = pad // tm

    def body(h_ref, g_ref, sh_ref, sc_ref, w_ref, u_ref, l_ref, r_ref):
        _, _, _, u = _norm_mod(h_ref[...], g_ref[...], sh_ref[...], sc_ref[...])
        ub = u.astype(BF16)
        u_ref[...] = ub
        for j in range(nj):
            o_ref = l_ref if j < half else r_ref
            jj = j % half
            o_ref[:, jj * wn:(jj + 1) * wn] = jnp.dot(ub, w_ref[j], preferred_element_type=F32).astype(BF16)
        if pb:
            @pl.when(pl.program_id(0) == 0)
            def _():
                l_ref[...] = jnp.zeros_like(l_ref)
                r_ref[...] = jnp.zeros_like(r_ref)

    e = half * wn
    src = lambda i: (jnp.maximum(i - pb, 0), 0)
    return pl.pallas_call(
        body, name=name, grid=(s // tm + pb,),
        in_specs=[pl.BlockSpec((tm, d), src), _vec(d), _vec(d), _vec(d),
                  pl.BlockSpec((nj, None, d, wn), lambda i: (0, layer, 0, 0))],
        out_specs=[pl.BlockSpec((tm, d), src), pl.BlockSpec((tm, e), lambda i: (i, 0)),
                   pl.BlockSpec((tm, e), lambda i: (i, 0))],
        out_shape=[_sds((s, d), BF16), _sds((s + pad, e), BF16), _sds((s + pad, e), BF16)],
        compiler_params=_cp("arbitrary"),
    )(h, g, shift, scale, w)


def _pool_fwd(val, wg, layer):
    s, e = val.shape
    ng = len(POOL_WINDOWS)
    gw = e // ng
    tm = _row_tile(s, 256)
    hb = tm // POOL_HALO

    def body(v_ref, halo_ref, w_ref, p_ref, m_ref):
        i = pl.program_id(0)
        t = i * tm + lax.broadcasted_iota(jnp.int32, (tm, 1), 0)
        for g, wdw in enumerate(POOL_WINDOWS):
            cols = slice(g * gw, (g + 1) * gw)
            v = v_ref[:, cols].astype(F32)
            halo = jnp.where(i > 0, halo_ref[:, cols].astype(F32), 0.0)
            acc = jnp.concatenate([halo, v], axis=0)
            sh = 1
            while sh < wdw:
                acc = acc + pltpu.roll(acc, sh, 0)
                sh *= 2
            cnt = jnp.minimum(t + 1, wdw).astype(F32)
            pb = (acc[POOL_HALO:, :] / cnt - v).astype(BF16)
            p_ref[:, cols] = pb
            m_ref[:, cols] = jnp.dot(pb, w_ref[:, g].reshape(gw, gw), preferred_element_type=F32).astype(BF16)

    return pl.pallas_call(
        body, name="pool_fwd", grid=(s // tm,),
        in_specs=[pl.BlockSpec((tm, e), lambda i: (i, 0)),
                  pl.BlockSpec((POOL_HALO, e), lambda i: (jnp.maximum(i * hb - 1, 0), 0)),
                  pl.BlockSpec((N_DEV, None, ng, gw // N_DEV, gw), lambda i: (0, layer, 0, 0, 0))],
        out_specs=[pl.BlockSpec((tm, e), lambda i: (i, 0)), pl.BlockSpec((tm, e), lambda i: (i, 0))],
        out_shape=[_sds((s, e), BF16), _sds((s, e), BF16)],
        compiler_params=_cp("arbitrary"),
    )(val, val, wg)


def _gate_out_fwd(m, z, ascale, w, layer, h, gate, name):
    s, e = m.shape
    d = h.shape[1]
    tm = _row_tile(s, 256)

    def body(m_ref, z_ref, a_ref, w_ref, h_ref, g_ref, gd_ref, y_ref, ho_ref):
        z = z_ref[...].astype(F32)
        gb = ((m_ref[...].astype(F32) * a_ref[...]) * (z * _sigmoid(z))).astype(BF16)
        gd_ref[...] = gb
        y = jnp.dot(gb, w_ref[...].reshape(e, d), preferred_element_type=F32)
        y_ref[...] = y.astype(BF16)
        ho_ref[...] = h_ref[...] + g_ref[...] * y

    return pl.pallas_call(
        body, name=name, grid=(s // tm,),
        in_specs=[pl.BlockSpec((tm, e), lambda i: (i, 0)), pl.BlockSpec((tm, e), lambda i: (i, 0)), _vec(e),
                  pl.BlockSpec((N_DEV, None, e // N_DEV, d), lambda i: (0, layer, 0, 0)),
                  pl.BlockSpec((tm, d), lambda i: (i, 0)), _vec(d)],
        out_specs=[pl.BlockSpec((tm, e), lambda i: (i, 0)), pl.BlockSpec((tm, d), lambda i: (i, 0)),
                   pl.BlockSpec((tm, d), lambda i: (i, 0))],
        out_shape=[_sds((s, e), BF16), _sds((s, d), BF16), _sds((s, d), F32)],
        compiler_params=_cp("arbitrary"),
    )(m, z, ascale, w, h, gate)


def _rel_onehot(shape, r_axis):
    r = lax.broadcasted_iota(jnp.int32, shape, r_axis)
    j = lax.broadcasted_iota(jnp.int32, shape, 1 - r_axis)
    dist = LEFT - (j - TQ)
    return (jnp.clip(dist, -REL_CLIP, REL_CLIP) + REL_CLIP == r).astype(F32)


def _skew(x, sign):
    row = lax.broadcasted_iota(jnp.int32, x.shape, 0)
    for b in range(TQ.bit_length() - 1):
        amt = (1 << b) if sign > 0 else FRAME_PAD - (1 << b)
        x = jnp.where(((row >> b) & 1) == 1, pltpu.roll(x, amt, 1), x)
    return x


def _bias_frames(rel):
    nh = rel.shape[0]

    def body(r_ref, f_ref, ft_ref):
        fext = jnp.dot(r_ref[...], _rel_onehot((REL_PAD, FRAME_PAD), 0), preferred_element_type=F32,
                       precision=lax.Precision.HIGHEST)
        x = _skew(jnp.broadcast_to(fext, (TQ, FRAME_PAD)), 1)[:, TQ:]
        qc = lax.broadcasted_iota(jnp.int32, (TQ, BAND), 0) // CHUNK
        m = lax.broadcasted_iota(jnp.int32, (TQ, BAND), 1)
        mc = m // CHUNK
        x = jnp.where((mc >= qc) & (mc <= qc + LEFT_CHUNKS), x, NEG)
        for f in range(NFRAME):
            xf = jnp.where(m >= LEFT - f * TQ, x, NEG)
            f_ref[f] = xf
            ft_ref[f] = xf.T

    return pl.pallas_call(
        body, name="bias_frames", grid=(nh,),
        in_specs=[pl.BlockSpec((None, 1, REL_PAD), lambda h: (h, 0, 0))],
        out_specs=[pl.BlockSpec((None, NFRAME, TQ, BAND), lambda h: (h, 0, 0, 0)),
                   pl.BlockSpec((None, NFRAME, BAND, TQ), lambda h: (h, 0, 0, 0))],
        out_shape=[_sds((nh, NFRAME, TQ, BAND), F32), _sds((nh, NFRAME, BAND, TQ), F32)],
        compiler_params=_cp("arbitrary"),
    )(rel)


def _bias_bwd(dft):
    nh = dft.shape[0]

    def body(d_ref, o_ref):
        d = d_ref[0]
        for f in range(1, NFRAME):
            d = d + d_ref[f]
        x = jnp.concatenate([jnp.zeros((TQ, TQ), F32), d.T], axis=1)
        col = jnp.sum(_skew(x, -1), axis=0, keepdims=True)
        o_ref[...] = jnp.dot(col, _rel_onehot((FRAME_PAD, REL_PAD), 1), preferred_element_type=F32,
                             precision=lax.Precision.HIGHEST)

    return pl.pallas_call(
        body, name="bias_bwd", grid=(nh,),
        in_specs=[pl.BlockSpec((None, NFRAME, BAND, TQ), lambda h: (h, 0, 0, 0))],
        out_specs=pl.BlockSpec((None, 1, REL_PAD), lambda h: (h, 0, 0)),
        out_shape=_sds((nh, 1, REL_PAD), F32),
        compiler_params=_cp("arbitrary"),
    )(dft)


def _attn_tiles(s):
    nt = s // TQ
    ni = min(ATTN_GROUP, nt)
    return nt // ni, ni


def _attn_fwd(q, k, v, frame):
    s, e = q.shape
    dh = e // N_HEADS
    ng, ni = _attn_tiles(s)
    sm = dh ** -0.5
    nb = min(ATTN_BATCH, ni)

    def body(q_ref, k_ref, v_ref, b_ref, o_ref):
        g0 = pl.program_id(1) * ni
        for tb in range(0, ni, nb):
            ts = range(tb, tb + nb)
            keys = [pl.ds(pl.multiple_of((g0 + t) * TQ, TQ), BAND) for t in ts]
            rows = [pl.ds(t * TQ, TQ) for t in ts]
            scs = [lax.dot_general(q_ref[r, :], k_ref[kk, :], NT, preferred_element_type=F32)
                   for r, kk in zip(rows, keys)]
            ps, ls = [], []
            for t, sc in zip(ts, scs):
                sc = sc * sm + b_ref[jnp.minimum(g0 + t, NFRAME - 1)]
                p = jnp.exp(sc - jnp.max(sc, axis=-1, keepdims=True))
                ls.append(jnp.sum(p, axis=-1, keepdims=True))
                ps.append(p.astype(BF16))
            for r, p, l, kk in zip(rows, ps, ls, keys):
                o = jnp.dot(p, v_ref[kk, :], preferred_element_type=F32)
                o_ref[r, :] = (o / l).astype(BF16)

    return pl.pallas_call(
        body, name="attn_fwd", grid=(N_HEADS, ng),
        in_specs=[pl.BlockSpec((ni * TQ, dh), lambda h, i: (i, h)),
                  pl.BlockSpec((s + LEFT, dh), lambda h, i: (0, h)), pl.BlockSpec((s + LEFT, dh), lambda h, i: (0, h)),
                  pl.BlockSpec((None, NFRAME, TQ, BAND), lambda h, i: (h, 0, 0, 0))],
        out_specs=pl.BlockSpec((ni * TQ, dh), lambda h, i: (i, h)),
        out_shape=_sds((s, e), BF16),
        compiler_params=_cp("arbitrary", "arbitrary"),
    )(q, k, v, frame)


def _final_loss(h, g, target):
    s, d = h.shape
    tm = _row_tile(s, 512)

    def body(h_ref, g_ref, t_ref, dh_ref, dg_ref, l_ref):
        @pl.when(pl.program_id(0) == 0)
        def _():
            dg_ref[...] = jnp.zeros_like(dg_ref)
            l_ref[...] = jnp.zeros_like(l_ref)

        hf = h_ref[...]
        gg = g_ref[...]
        rs = lax.rsqrt(jnp.mean(hf * hf, axis=-1, keepdims=True) + EPS)
        xhat = hf * rs
        diff = xhat * gg - t_ref[...]
        l_ref[...] += 0.5 * jnp.sum(jnp.mean(diff * diff, axis=-1, keepdims=True), axis=0, keepdims=True)
        dout = diff * (1.0 / d)
        dg_ref[...] += jnp.sum(dout * xhat, axis=0, keepdims=True)
        dxh = dout * gg
        dh_ref[...] = rs * (dxh - xhat * jnp.mean(dxh * xhat, axis=-1, keepdims=True))

    return pl.pallas_call(
        body, name="final_loss", grid=(s // tm,),
        in_specs=[pl.BlockSpec((tm, d), lambda i: (i, 0)), _vec(d), pl.BlockSpec((tm, d), lambda i: (i, 0))],
        out_specs=[pl.BlockSpec((tm, d), lambda i: (i, 0)), _vec(d), _vec(LANES)],
        out_shape=[_sds((s, d), F32), _sds((1, d), F32), _sds((1, LANES), F32)],
        compiler_params=_cp("arbitrary"),
    )(h, g, target)


def _out_bwd(dh, y, gate, w, layer, m, z, ascale, name):
    s, d = dh.shape
    e = m.shape[1]
    tm = _row_tile(s, 256)

    def body(dh_ref, y_ref, g_ref, w_ref, m_ref, z_ref, a_ref, dy_ref, dm_ref, dz_ref, dg_ref, da_ref):
        @pl.when(pl.program_id(0) == 0)
        def _():
            dg_ref[...] = jnp.zeros_like(dg_ref)
            da_ref[...] = jnp.zeros_like(da_ref)

        dhf = dh_ref[...]
        dg_ref[...] += jnp.sum(dhf * y_ref[...].astype(F32), axis=0, keepdims=True)
        dyb = (g_ref[...] * dhf).astype(BF16)
        dy_ref[...] = dyb
        dgated = lax.dot_general(dyb, w_ref[...].reshape(e, d), NT, preferred_element_type=F32)
        z = z_ref[...].astype(F32)
        sig = _sigmoid(z)
        mf = m_ref[...].astype(F32)
        a = a_ref[...]
        dms = dgated * (z * sig)
        da_ref[...] += jnp.sum(dms * mf, axis=0, keepdims=True)
        dm_ref[...] = (dms * a).astype(BF16)
        dz_ref[...] = (dgated * (mf * a) * (sig * (1.0 + z * (1.0 - sig)))).astype(BF16)

    return pl.pallas_call(
        body, name=name, grid=(s // tm,),
        in_specs=[pl.BlockSpec((tm, d), lambda i: (i, 0)), pl.BlockSpec((tm, d), lambda i: (i, 0)), _vec(d),
                  pl.BlockSpec((N_DEV, None, e // N_DEV, d), lambda i: (0, layer, 0, 0)),
                  pl.BlockSpec((tm, e), lambda i: (i, 0)), pl.BlockSpec((tm, e), lambda i: (i, 0)), _vec(e)],
        out_specs=[pl.BlockSpec((tm, d), lambda i: (i, 0)), pl.BlockSpec((tm, e), lambda i: (i, 0)),
                   pl.BlockSpec((tm, e), lambda i: (i, 0)), _vec(d), _vec(e)],
        out_shape=[_sds((s, d), BF16), _sds((s, e), BF16), _sds((s, e), BF16), _sds((1, d), F32), _sds((1, e), F32)],
        compiler_params=_cp("arbitrary"),
    )(dh, y, gate, w, m, z, ascale)


def _pool_bwd(dm, wg, layer):
    s, e = dm.shape
    ng = len(POOL_WINDOWS)
    gw = e // ng
    tm = _row_tile(s, 256)
    hb = tm // POOL_HALO
    nsteps = s // tm

    def body(d_ref, halo_ref, w_ref, o_ref):
        i = pl.program_id(0)
        t = i * tm + lax.broadcasted_iota(jnp.int32, (tm + POOL_HALO, 1), 0)
        for g, wdw in enumerate(POOL_WINDOWS):
            cols = slice(g * gw, (g + 1) * gw)
            dmx = jnp.concatenate([d_ref[:, cols], halo_ref[:, cols]], axis=0)
            dp = lax.dot_general(dmx, w_ref[:, g].reshape(gw, gw), NT, preferred_element_type=F32)
            dp = jnp.where(t < s, dp, 0.0)
            acc = dp / jnp.minimum(t + 1, wdw).astype(F32)
            sh = 1
            while sh < wdw:
                acc = acc + pltpu.roll(acc, tm + POOL_HALO - sh, 0)
                sh *= 2
            o_ref[:, cols] = (acc[:tm, :] - dp[:tm, :]).astype(BF16)

    return pl.pallas_call(
        body, name="pool_bwd", grid=(nsteps,),
        in_specs=[pl.BlockSpec((tm, e), lambda i: (i, 0)),
                  pl.BlockSpec((POOL_HALO, e), lambda i: (jnp.minimum((i + 1) * hb, s // POOL_HALO - 1), 0)),
                  pl.BlockSpec((N_DEV, None, ng, gw // N_DEV, gw), lambda i: (0, layer, 0, 0, 0))],
        out_specs=pl.BlockSpec((tm, e), lambda i: (i, 0)),
        out_shape=_sds((s, e), BF16),
        compiler_params=_cp("arbitrary"),
    )(dm, dm, wg)


def _attn_bwd(q, k, v, do, frame_t):
    s, e = q.shape
    dh = e // N_HEADS
    ng, ni = _attn_tiles(s)
    sm = dh ** -0.5
    nleft = LEFT // TQ

    def body(q_ref, k_ref, v_ref, do_ref, b_ref, dq_ref, dk_ref, dv_ref, db_ref):
        i = pl.program_id(1)
        g0 = i * ni

        @pl.when(i == 0)
        def _():
            dk_ref[...] = jnp.zeros_like(dk_ref)
            dv_ref[...] = jnp.zeros_like(dv_ref)
            db_ref[...] = jnp.zeros_like(db_ref)

        keys = [pl.ds(pl.multiple_of((g0 + t) * TQ, TQ), BAND) for t in range(ni)]
        rows = [pl.ds(t * TQ, TQ) for t in range(ni)]
        sts = [lax.dot_general(k_ref[keys[t], :], q_ref[rows[t], :], NT, preferred_element_type=F32)
               for t in range(ni)]
        dpts = [lax.dot_general(v_ref[keys[t], :], do_ref[rows[t], :], NT, preferred_element_type=F32)
                for t in range(ni)]
        pbs, dsbs, dsts = [], [], []
        for t in range(ni):
            st = sts[t] * sm + b_ref[jnp.minimum(g0 + t, NFRAME - 1)]
            p = jnp.exp(st - jnp.max(st, axis=0, keepdims=True))
            p = p * (1.0 / jnp.sum(p, axis=0, keepdims=True))
            dst = p * (dpts[t] - jnp.sum(dpts[t] * p, axis=0, keepdims=True))
            pbs.append(p.astype(BF16))
            dsbs.append(dst.astype(BF16))
            dsts.append(dst)
        for t in range(min(ni, nleft)):
            db_ref[jnp.minimum(g0 + t, NFRAME - 1)] += dsts[t]
        if ni > nleft:
            rest = dsts[nleft]
            for t in range(nleft + 1, ni):
                rest = rest + dsts[t]
            db_ref[NFRAME - 1] += rest
        for t in range(ni):
            dq = lax.dot_general(dsbs[t], k_ref[keys[t], :], TN, preferred_element_type=F32)
            dq_ref[rows[t], :] = (dq * sm).astype(BF16)
        for r in range(ni + nleft):
            ts = [t for t in range(ni) if 0 <= r - t <= nleft]
            blk = lambda xs: jnp.concatenate([xs[t][(r - t) * TQ:(r - t + 1) * TQ, :] for t in ts], axis=1)
            qrows = slice(ts[0] * TQ, (ts[-1] + 1) * TQ)
            krows = pl.ds(pl.multiple_of((g0 + r) * TQ, TQ), TQ)
            dk_ref[krows, :] += jnp.dot(blk(dsbs), q_ref[qrows, :], preferred_element_type=F32) * sm
            dv_ref[krows, :] += jnp.dot(blk(pbs), do_ref[qrows, :], preferred_element_type=F32)

    kv_spec = pl.BlockSpec((s + LEFT, dh), lambda h, i: (0, h))
    fr_spec = pl.BlockSpec((None, NFRAME, BAND, TQ), lambda h, i: (h, 0, 0, 0))
    return pl.pallas_call(
        body, name="attn_bwd", grid=(N_HEADS, ng),
        in_specs=[pl.BlockSpec((ni * TQ, dh), lambda h, i: (i, h)), kv_spec, kv_spec,
                  pl.BlockSpec((ni * TQ, dh), lambda h, i: (i, h)), fr_spec],
        out_specs=[pl.BlockSpec((ni * TQ, dh), lambda h, i: (i, h)), kv_spec, kv_spec, fr_spec],
        out_shape=[_sds((s, e), BF16), _sds((s + LEFT, e), F32), _sds((s + LEFT, e), F32),
                   _sds((N_HEADS, NFRAME, BAND, TQ), F32)],
        compiler_params=_cp("arbitrary", "arbitrary"),
    )(q, k, v, do, frame_t)


def _add_cast(a, b, skip):
    s, e = a.shape[0] - skip, a.shape[1]
    tm = _row_tile(s, 512)
    assert skip % tm == 0
    off = skip // tm

    def body(a_ref, b_ref, o_ref):
        o_ref[...] = (a_ref[...] + b_ref[...]).astype(BF16)

    return pl.pallas_call(
        body, name="add_cast", grid=(s // tm,),
        in_specs=[pl.BlockSpec((tm, e), lambda i: (i + off, 0))] * 2,
        out_specs=pl.BlockSpec((tm, e), lambda i: (i, 0)), out_shape=_sds((s, e), BF16),
        compiler_params=_cp("arbitrary"),
    )(a, b)


def _in_bwd(dl, dr, w, layer, h, g, scale, dres, name):
    s, d = h.shape
    nj, _, _, wn = w.shape
    half = nj // 2
    e = half * wn
    tm = _row_tile(s, 256)

    def body(dl_ref, dr_ref, w_ref, h_ref, g_ref, sc_ref, res_ref, dh_ref, st_ref):
        @pl.when(pl.program_id(0) == 0)
        def _():
            st_ref[...] = jnp.zeros_like(st_ref)

        du = jnp.zeros((tm, d), F32)
        for j in range(nj):
            src = dl_ref if j < half else dr_ref
            jj = j % half
            du = du + lax.dot_general(src[:, jj * wn:(jj + 1) * wn], w_ref[j], NT, preferred_element_type=F32)
        gg = g_ref[...]
        rs, xhat, n, _ = _norm_mod(h_ref[...], gg, 0.0, 0.0)
        dn = du * (1.0 + sc_ref[...])
        st_ref[0:1, :] += jnp.sum(du, axis=0, keepdims=True)
        st_ref[1:2, :] += jnp.sum(du * n, axis=0, keepdims=True)
        st_ref[2:3, :] += jnp.sum(dn * xhat, axis=0, keepdims=True)
        dxh = dn * gg
        dh_ref[...] = rs * (dxh - xhat * jnp.mean(dxh * xhat, axis=-1, keepdims=True)) + res_ref[...]

    return pl.pallas_call(
        body, name=name, grid=(s // tm,),
        in_specs=[pl.BlockSpec((tm, e), lambda i: (i, 0)), pl.BlockSpec((tm, e), lambda i: (i, 0)),
                  pl.BlockSpec((nj, None, d, wn), lambda i: (0, layer, 0, 0)),
                  pl.BlockSpec((tm, d), lambda i: (i, 0)), _vec(d), _vec(d), pl.BlockSpec((tm, d), lambda i: (i, 0))],
        out_specs=[pl.BlockSpec((tm, d), lambda i: (i, 0)), pl.BlockSpec((8, d), lambda i: (0, 0))],
        out_shape=[_sds((s, d), F32), _sds((8, d), F32)],
        compiler_params=_cp("arbitrary"),
    )(dl, dr, w, h, g, scale, dres)


def _tn_matmul(x, ys, xw, yw, xmap, ymap, out_shape, out_block, out_map, nb, into, name):
    s = x.shape[0]
    ts = _row_tile(s, 512)
    ys = list(ys)
    half = nb // len(ys)
    nk = s // ts

    def body(*refs):
        x_ref, y_refs, o_ref, acc_ref = refs[0], refs[1:1 + len(ys)], refs[-2], refs[-1]
        b = pl.program_id(0)

        @pl.when(pl.program_id(1) == 0)
        def _():
            acc_ref[...] = jnp.zeros_like(acc_ref)

        for n, y_ref in enumerate(y_refs):
            @pl.when((b >= n * half) & (b < (n + 1) * half))
            def _():
                acc_ref[...] += lax.dot_general(x_ref[...], y_ref[...], TN, preferred_element_type=F32)

        @pl.when(pl.program_id(1) == nk - 1)
        def _():
            o_ref[...] = acc_ref[...].reshape(o_ref.shape).astype(BF16)

    in_specs = [pl.BlockSpec((ts, xw), lambda b, k: (k, xmap(b)))]
    for n in range(len(ys)):
        in_specs.append(pl.BlockSpec(
            (ts, yw), lambda b, k, n=n: (k, ymap(jnp.clip(b - n * half, 0, half - 1)))))
    args = [x] + ys
    aliases = {}
    if into is not None:
        in_specs.append(HBM_SPEC)
        args.append(into)
        aliases = {len(args) - 1: 0}

        def body_into(*refs):
            body(*refs[:1 + len(ys)], refs[-2], refs[-1])
        fn = body_into
    else:
        fn = body
    return pl.pallas_call(
        fn, name=name, grid=(nb, nk), in_specs=in_specs,
        out_specs=pl.BlockSpec(out_block, lambda b, k: out_map(b)), out_shape=_sds(out_shape, BF16),
        scratch_shapes=[pltpu.VMEM((xw, yw), F32)],
        input_output_aliases=aliases, compiler_params=_cp("arbitrary", "arbitrary"),
    )(*args)


def _grad_w_in(u, dl, dr, layer, nl, into, name):
    d = u.shape[1]
    wn = 2 * dl.shape[1] // N_DEV
    return _tn_matmul(u, (dl, dr), d, wn, lambda b: 0, lambda b: b, (N_DEV, nl, d, wn), (None, None, d, wn),
                      lambda b: (b, layer, 0, 0), N_DEV, into, name)


def _grad_w_out(gated, dy, layer, nl, into, name):
    e, d = gated.shape[1], dy.shape[1]
    return _tn_matmul(gated, (dy,), e // N_DEV, d, lambda b: b, lambda b: 0, (N_DEV, nl, e // N_DEV, d),
                      (None, None, e // N_DEV, d), lambda b: (b, layer, 0, 0), N_DEV, into, name)


def _grad_w_group(pooled, dm, layer, nl, into, name):
    ng = len(POOL_WINDOWS)
    gw = pooled.shape[1] // ng
    return _tn_matmul(pooled, (dm,), gw, gw, lambda b: b, lambda b: b, (N_DEV, nl, ng, gw // N_DEV, gw),
                      (N_DEV, None, None, gw // N_DEV, gw), lambda b: (0, layer, b, 0, 0), ng, into, name)


def _adamw(staged, w, m, v, name):
    shape = w.shape
    n = staged.shape[0]
    cdim = shape[-1]
    rows = 1
    for a in shape[:-1]:
        rows *= a
    st = staged.reshape(n, rows, cdim)
    tr = rows if rows * cdim <= 128 * 1024 else max(8, (128 * 1024 // cdim) // 8 * 8)
    while rows % tr:
        tr -= 8

    def body(s_ref, w_ref, m_ref, v_ref, g_ref, d_ref, mo_ref, vo_ref):
        g = s_ref[0].astype(F32)
        for j in range(1, n):
            g = g + s_ref[j].astype(F32)
        mn = ADAM_B1 * m_ref[...] + (1.0 - ADAM_B1) * g
        vn = ADAM_B2 * v_ref[...] + (1.0 - ADAM_B2) * (g * g)
        m_hat = mn / (1.0 - ADAM_B1 ** ADAM_STEP)
        v_hat = vn / (1.0 - ADAM_B2 ** ADAM_STEP)
        g_ref[...] = g
        d_ref[...] = -ADAM_LR * (m_hat / (jnp.sqrt(v_hat) + ADAM_EPS) + ADAM_WD * w_ref[...])
        mo_ref[...] = mn
        vo_ref[...] = vn

    blk = pl.BlockSpec((tr, cdim), lambda i: (i, 0))
    outs = pl.pallas_call(
        body, name=name, grid=(rows // tr,),
        in_specs=[pl.BlockSpec((n, tr, cdim), lambda i: (0, i, 0)), blk, blk, blk],
        out_specs=[blk] * 4, out_shape=[_sds((rows, cdim), F32)] * 4,
        compiler_params=_cp("arbitrary"),
    )(st, w.reshape(rows, cdim), m.reshape(rows, cdim), v.reshape(rows, cdim))
    return [o.reshape(shape) for o in outs]


def _pack(parts, total):
    flat = jnp.concatenate([p.reshape(-1) for p in parts])
    return jnp.pad(flat, (0, total - flat.shape[0])).reshape(1, total)


def kernel(x, c, ada_w, ada_b, norm_g, a_w_in, a_w_group, a_scale, a_w_out, kv_norm_g, kv_ada_w, kv_ada_b, w_kv, b_w_in, b_rel_bias, b_w_out, final_g, loss_target, m_ada_w, m_ada_b, m_norm_g, m_a_w_in, m_a_w_group, m_a_scale, m_a_w_out, m_kv_norm_g, m_kv_ada_w, m_kv_ada_b, m_w_kv, m_b_w_in, m_b_rel_bias, m_b_w_out, m_final_g, v_ada_w, v_ada_b, v_norm_g, v_a_w_in, v_a_w_group, v_a_scale, v_a_w_out, v_kv_norm_g, v_kv_ada_w, v_kv_ada_b, v_w_kv, v_b_w_in, v_b_rel_bias, v_b_w_out, v_final_g):
    s, d = x.shape[1], x.shape[2]
    depth, _, wa = ada_w.shape
    wk = kv_ada_w.shape[1]
    n_a, n_b = a_w_in.shape[0], b_w_in.shape[0]
    e = a_w_out.shape[1] * N_DEV
    nrel = b_rel_bias.shape[-1]
    me = 4 * lax.axis_index("x") + 2 * lax.axis_index("y") + lax.axis_index("c")
    h0 = x[0]
    target = loss_target[0]

    c_all = _exchange([c.reshape(1, 1, d)], [True], "gather_c")[0].reshape(N_DEV, d)
    ada_b_loc = lax.dynamic_slice(ada_b, (0, me * wa), (depth, wa)).reshape(depth, 1, wa)
    kv_ada_b_loc = lax.dynamic_slice(kv_ada_b, (me * wk,), (wk,)).reshape(1, wk)
    mods_cols = _mods_fwd(c_all, ada_w, ada_b_loc, kv_ada_w, kv_ada_b_loc)
    nm = depth * wa + wk
    got = _exchange(
        [mods_cols.reshape(N_DEV, 1, nm), a_w_in.astype(BF16), a_w_group.astype(BF16), a_w_out.astype(BF16),
         w_kv.astype(BF16).reshape(1, d, -1), b_w_in.astype(BF16), b_w_out.astype(BF16), a_scale],
        [False, True, True, True, True, True, True, True], "gather_weights")
    mods_me, wa_in, wa_group, wa_out, wkv, wb_in, wb_out, a_scale_all = got
    mods_me = mods_me.reshape(N_DEV, nm)
    mods = mods_me[:, :depth * wa].reshape(N_DEV, depth, wa).transpose(1, 0, 2).reshape(depth, 3, 1, d)
    kv_mods = mods_me[:, depth * wa:].reshape(2, 1, d)
    a_scale_full = a_scale_all.transpose(1, 0, 2).reshape(n_a, 1, e)
    ones_e = jnp.ones((1, e), F32)

    saved = []
    h = h0
    k = v = hk = None
    frames = []
    for layer in range(depth):
        shift, scale, gate = mods[layer, 0], mods[layer, 1], mods[layer, 2]
        g = norm_g[layer].reshape(1, d)
        if layer < n_a:
            u, left, z = _inproj_fwd(h, g, shift, scale, wa_in, layer, "inproj_fwd")
            pooled, mixed = _pool_fwd(left, wa_group, layer)
            gated, y, hn = _gate_out_fwd(mixed, z, a_scale_full[layer], wa_out, layer, h, gate, "gate_out_fwd")
            saved.append(dict(h=h, u=u, left=left, z=z, pooled=pooled, m=mixed, gated=gated, y=y))
        else:
            bi = layer - n_a
            if bi == 0:
                hk, k, v = _inproj_fwd(h, kv_norm_g.reshape(1, d), kv_mods[0], kv_mods[1], wkv, 0, "inproj_kv",
                                       pad=LEFT)
            u, left, z = _inproj_fwd(h, g, shift, scale, wb_in, bi, "inproj_fwd")
            rel = jnp.pad(b_rel_bias[bi], ((0, 0), (0, REL_PAD - nrel))).reshape(N_HEADS, 1, REL_PAD)
            frame, frame_t = _bias_frames(rel)
            att = _attn_fwd(left, k, v, frame)
            gated, y, hn = _gate_out_fwd(att, z, ones_e, wb_out, bi, h, gate, "gate_out_fwd")
            saved.append(dict(h=h, u=u, left=left, z=z, m=att, gated=gated, y=y, frame_t=frame_t))
        h = hn
    dh, d_final_g, loss_part = _final_loss(h, final_g.reshape(1, d), target)

    d_mods = [None] * depth
    d_norm_g = [None] * depth
    d_rel = [None] * n_b
    d_ascale = [None] * n_a
    dk_parts, dv_parts = [], []
    g_b_in = g_b_out = g_a_in = g_a_out = g_a_group = None
    for layer in range(depth - 1, n_a - 1, -1):
        bi = layer - n_a
        sv = saved[layer]
        scale, gate = mods[layer, 1], mods[layer, 2]
        dy, datt, dz, d_gate, _ = _out_bwd(dh, sv["y"], gate, wb_out, bi, sv["m"], sv["z"], ones_e, "out_bwd")
        dq, dk, dv, dframe_t = _attn_bwd(sv["left"], k, v, datt, sv["frame_t"])
        dk_parts.append(dk)
        dv_parts.append(dv)
        d_rel[bi] = _bias_bwd(dframe_t).reshape(N_HEADS, REL_PAD)[:, :nrel]
        g_b_out = _grad_w_out(sv["gated"], dy, bi, n_b, g_b_out, "grad_w_out")
        g_b_in = _grad_w_in(sv["u"], dq, dz, bi, n_b, g_b_in, "grad_w_in")
        dh, st = _in_bwd(dq, dz, wb_in, bi, sv["h"], norm_g[layer].reshape(1, d), scale, dh, "in_bwd")
        d_mods[layer] = jnp.concatenate([st[0], st[1], d_gate[0]])
        d_norm_g[layer] = st[2]
    dkb = _add_cast(dk_parts[0], dk_parts[1], LEFT)
    dvb = _add_cast(dv_parts[0], dv_parts[1], LEFT)
    h_kv = saved[n_a]["h"]
    g_w_kv = _grad_w_in(hk, dkb, dvb, 0, 1, None, "grad_w_in")
    dh, st = _in_bwd(dkb, dvb, wkv, 0, h_kv, kv_norm_g.reshape(1, d), kv_mods[1], dh, "in_bwd")
    d_kv_mods = jnp.concatenate([st[0], st[1]])
    d_kv_norm_g = st[2]
    for layer in range(n_a - 1, -1, -1):
        sv = saved[layer]
        scale, gate = mods[layer, 1], mods[layer, 2]
        dy, dm, dz, d_gate, da = _out_bwd(dh, sv["y"], gate, wa_out, layer, sv["m"], sv["z"], a_scale_full[layer],
                                          "out_bwd")
        d_ascale[layer] = da.reshape(N_DEV, e // N_DEV)
        dval = _pool_bwd(dm, wa_group, layer)
        g_a_group = _grad_w_group(sv["pooled"], dm, layer, n_a, g_a_group, "grad_w_group")
        g_a_out = _grad_w_out(sv["gated"], dy, layer, n_a, g_a_out, "grad_w_out")
        g_a_in = _grad_w_in(sv["u"], dval, dz, layer, n_a, g_a_in, "grad_w_in")
        dh, st = _in_bwd(dval, dz, wa_in, layer, sv["h"], norm_g[layer].reshape(1, d), scale, dh, "in_bwd")
        d_mods[layer] = jnp.concatenate([st[0], st[1], d_gate[0]])
        d_norm_g[layer] = st[2]
    grad_x = dh.reshape(1, s, d)

    d_mods = jnp.stack(d_mods)
    dm_slots = jnp.concatenate(
        [d_mods.reshape(depth, N_DEV, wa).transpose(1, 0, 2).reshape(N_DEV, depth * wa), d_kv_mods.reshape(N_DEV, wk)],
        axis=1).reshape(N_DEV, 1, nm)
    small_parts = [d_mods, d_kv_mods, jnp.stack(d_norm_g), d_kv_norm_g, d_final_g, jnp.stack(d_rel), loss_part[0, :1]]
    n_small = sum(int(p.size) for p in small_parts)
    n_small_pad = -(-n_small // LANES) * LANES
    small = _pack(small_parts, n_small_pad).reshape(1, 1, n_small_pad)
    d_ascale_slots = jnp.stack(d_ascale, axis=1)
    got = _exchange(
        [g_a_in, g_a_group, d_ascale_slots, g_a_out, g_w_kv, g_b_in, g_b_out, dm_slots, small],
        [False] * 8 + [True], "exchange_grads")
    s_a_in, s_a_group, s_a_scale, s_a_out, s_w_kv, s_b_in, s_b_out, dm_cols, small_all = got

    g_ada_w, g_kv_ada_w = _mods_bwd(c_all, dm_cols.reshape(N_DEV, nm), depth, wa, wk)
    res = {}
    res["ada_w"] = _adamw(g_ada_w[None], ada_w, m_ada_w, v_ada_w, "adamw")
    res["kv_ada_w"] = _adamw(g_kv_ada_w[None], kv_ada_w, m_kv_ada_w, v_kv_ada_w, "adamw")
    res["a_w_in"] = _adamw(s_a_in, a_w_in, m_a_w_in, v_a_w_in, "adamw")
    res["a_w_group"] = _adamw(s_a_group, a_w_group, m_a_w_group, v_a_w_group, "adamw")
    res["a_scale"] = _adamw(s_a_scale, a_scale, m_a_scale, v_a_scale, "adamw")
    res["a_w_out"] = _adamw(s_a_out, a_w_out, m_a_w_out, v_a_w_out, "adamw")
    res["w_kv"] = _adamw(s_w_kv.reshape((N_DEV,) + w_kv.shape), w_kv, m_w_kv, v_w_kv, "adamw")
    res["b_w_in"] = _adamw(s_b_in, b_w_in, m_b_w_in, v_b_w_in, "adamw")
    res["b_w_out"] = _adamw(s_b_out, b_w_out, m_b_w_out, v_b_w_out, "adamw")
    small_names = ["ada_b", "kv_ada_b", "norm_g", "kv_norm_g", "final_g", "b_rel_bias"]
    small_w = dict(ada_b=ada_b, kv_ada_b=kv_ada_b, norm_g=norm_g, kv_norm_g=kv_norm_g, final_g=final_g,
                   b_rel_bias=b_rel_bias)
    small_m = dict(ada_b=m_ada_b, kv_ada_b=m_kv_ada_b, norm_g=m_norm_g, kv_norm_g=m_kv_norm_g, final_g=m_final_g,
                   b_rel_bias=m_b_rel_bias)
    small_v = dict(ada_b=v_ada_b, kv_ada_b=v_kv_ada_b, norm_g=v_norm_g, kv_norm_g=v_kv_norm_g, final_g=v_final_g,
                   b_rel_bias=v_b_rel_bias)
    sw = _pack([small_w[n] for n in small_names], n_small_pad)
    smm = _pack([small_m[n] for n in small_names], n_small_pad)
    svv = _pack([small_v[n] for n in small_names] + [jnp.ones((n_small_pad - n_small + 1,), F32)], n_small_pad)
    small_out = _adamw(small_all.reshape(N_DEV, 1, n_small_pad), sw, smm, svv, "adamw")
    off = 0
    for n in small_names:
        size = int(small_w[n].size)
        res[n] = [o[0, off:off + size].reshape(small_w[n].shape) for o in small_out]
        off += size
    loss = small_out[0][0, n_small - 1]

    order = ["ada_w", "ada_b", "norm_g", "a_w_in", "a_w_group", "a_scale", "a_w_out", "kv_norm_g", "kv_ada_w",
             "kv_ada_b", "w_kv", "b_w_in", "b_rel_bias", "b_w_out", "final_g"]
    outs = [loss, grad_x]
    for part in range(4):
        outs += [res[n][part] for n in order]
    return tuple(outs)
```

```python
import functools

import jax
import jax.numpy as jnp
from jax import lax
from jax.experimental import pallas as pl
from jax.experimental.pallas import tpu as pltpu

F32 = jnp.float32
BF16 = jnp.bfloat16

N_DEV = 8
CHUNK = 64
LEFT_CHUNKS = 8
N_HEADS = 16
POOL_WINDOWS = (2, 4, 8, 16)
REL_CLIP = 128
EPS = 1e-6
ADAM_LR = 0.001
ADAM_B1 = 0.9
ADAM_B2 = 0.999
ADAM_EPS = 1e-08
ADAM_WD = 0.01
ADAM_STEP = 10

LANES = 128
TQ = 2 * CHUNK
LEFT = LEFT_CHUNKS * CHUNK
BAND = LEFT + TQ
FRAME_PAD = BAND + TQ
NFRAME = LEFT // TQ + 1
POOL_HALO = 16
REL_PAD = 384
NEG = -1e30
ATTN_GROUP = 8
ATTN_BATCH = 4
VMEM_LIMIT = 56 * 1024 * 1024

NT = (((1,), (1,)), ((), ()))
TN = (((0,), (0,)), ((), ()))

HBM_SPEC = pl.BlockSpec(memory_space=pltpu.HBM)
ANY_SPEC = pl.BlockSpec(memory_space=pl.ANY)
SEM_SPEC = pl.BlockSpec(memory_space=pltpu.SEMAPHORE)


def _cp(*sem):
    return pltpu.CompilerParams(dimension_semantics=sem or None, vmem_limit_bytes=VMEM_LIMIT)


def _sds(shape, dtype):
    return jax.ShapeDtypeStruct(tuple(shape), dtype)


def _row_tile(s, want):
    return min(want, s)


def _vec(d):
    return pl.BlockSpec((1, d), lambda *_: (0, 0))


def _sigmoid(z):
    return 1.0 / (1.0 + jnp.exp(-z))


def _exchange(xs, gather, name):
    n = len(xs)
    out_shapes = []
    for x, ga in zip(xs, gather):
        out_shapes.append(_sds((N_DEV,) + (x.shape if ga else x.shape[1:]), x.dtype))

    def body(*refs):
        ins, outs = refs[:n], refs[n:2 * n]
        send_sems, recv_sems, local_sems = refs[2 * n:]
        mx, my, mc = lax.axis_index("x"), lax.axis_index("y"), lax.axis_index("c")
        me = 4 * mx + 2 * my + mc
        local = []
        for k in range(n):
            src = ins[k] if gather[k] else ins[k].at[me]
            cp = pltpu.make_async_copy(src, outs[k].at[me], local_sems.at[k])
            cp.start()
            local.append(cp)
        sends, recvs = [], []
        for r in range(1, N_DEV):
            px, py, pc = (mx + (r >> 2)) % 2, (my + ((r >> 1) & 1)) % 2, (mc + (r & 1)) % 2
            peer = 4 * px + 2 * py + pc
            for k in range(n):
                sem = k * (N_DEV - 1) + r - 1
                src = ins[k] if gather[k] else ins[k].at[peer]
                send = pltpu.make_async_remote_copy(
                    src_ref=src, dst_ref=outs[k].at[me], send_sem=send_sems.at[sem], recv_sem=recv_sems.at[sem],
                    device_id=(px, py, pc), device_id_type=pl.DeviceIdType.MESH)
                send.start()
                sends.append(send)
                recvs.append(pltpu.make_async_remote_copy(
                    src_ref=src, dst_ref=outs[k].at[peer], send_sem=send_sems.at[sem], recv_sem=recv_sems.at[sem],
                    device_id=(px, py, pc), device_id_type=pl.DeviceIdType.MESH))
        for cp in recvs:
            cp.wait_recv()
        for cp in sends:
            cp.wait_send()
        for cp in local:
            cp.wait()

    return pl.pallas_call(
        body, name=name, out_shape=out_shapes,
        in_specs=[HBM_SPEC] * n, out_specs=[HBM_SPEC] * n,
        scratch_shapes=[pltpu.SemaphoreType.DMA((n * (N_DEV - 1),)), pltpu.SemaphoreType.DMA((n * (N_DEV - 1),)),
                        pltpu.SemaphoreType.DMA((n,))],
    )(*xs)


def _peer(mx, my, mc, r):
    px, py, pc = (mx + (r >> 2)) % 2, (my + ((r >> 1) & 1)) % 2, (mc + (r & 1)) % 2
    return (px, py, pc), 4 * px + 2 * py + pc


def _push_start(xs, gather, dep, name):
    n = len(xs)
    nsem = n * (N_DEV - 1)
    me = 4 * lax.axis_index("x") + 2 * lax.axis_index("y") + lax.axis_index("c")
    lands = []
    for x, ga in zip(xs, gather):
        own = x[None] if ga else lax.dynamic_index_in_dim(x, me, 0, keepdims=True)
        empty = lax.empty((N_DEV,) + own.shape[1:], x.dtype)
        lands.append(lax.dynamic_update_slice(empty, own, (me,) + (0,) * (own.ndim - 1)))

    def body(*refs):
        ins, lands_in = refs[:n], refs[n:2 * n]
        send_sems, recv_sems = refs[2 * n + 1], refs[2 * n + 2]
        token = refs[-1]
        mx, my, mc = lax.axis_index("x"), lax.axis_index("y"), lax.axis_index("c")
        mine = 4 * mx + 2 * my + mc
        for k in range(n):
            for r in range(1, N_DEV):
                dev, peer = _peer(mx, my, mc, r)
                sem = k * (N_DEV - 1) + r - 1
                pltpu.make_async_remote_copy(
                    src_ref=ins[k] if gather[k] else ins[k].at[peer], dst_ref=lands_in[k].at[mine],
                    send_sem=send_sems.at[sem], recv_sem=recv_sems.at[sem],
                    device_id=dev, device_id_type=pl.DeviceIdType.MESH).start()
        token[...] = jnp.zeros_like(token)

    hbm = lambda a: pltpu.HBM(a.shape, a.dtype)
    outs = pl.pallas_call(
        body, name=name,
        out_shape=(pltpu.SemaphoreType.DMA((nsem,)), pltpu.SemaphoreType.DMA((nsem,)),
                   *[hbm(x) for x in xs], *[hbm(a) for a in lands], _sds((8, LANES), F32)),
        in_specs=[HBM_SPEC] * (2 * n) + [ANY_SPEC],
        out_specs=(SEM_SPEC, SEM_SPEC, *[HBM_SPEC] * (2 * n), pl.BlockSpec(memory_space=pltpu.VMEM)),
        input_output_aliases={k: 2 + k for k in range(2 * n)},
        compiler_params=pltpu.CompilerParams(has_side_effects=pltpu.SideEffectType.DATAFLOW_SIDE_EFFECTING),
    )(*[pltpu.with_memory_space_constraint(x, pltpu.HBM) for x in xs],
      *[pltpu.with_memory_space_constraint(a, pltpu.HBM) for a in lands], dep)
    return (outs[0], outs[1], outs[2:2 + n], outs[2 + n:2 + 2 * n], tuple(gather)), outs[-1]


def _push_wait(handle, after, name):
    send_sems, recv_sems, srcs, lands, gather = handle
    n = len(srcs)

    def body(*refs):
        ins, lands_in = refs[:n], refs[n:2 * n]
        send_sems, recv_sems = refs[2 * n], refs[2 * n + 1]
        mx, my, mc = lax.axis_index("x"), lax.axis_index("y"), lax.axis_index("c")
        for k in range(n):
            for r in range(1, N_DEV):
                dev, peer = _peer(mx, my, mc, r)
                sem = k * (N_DEV - 1) + r - 1
                cp = pltpu.make_async_remote_copy(
                    src_ref=ins[k] if gather[k] else ins[k].at[peer], dst_ref=lands_in[k].at[peer],
                    send_sem=send_sems.at[sem], recv_sem=recv_sems.at[sem],
                    device_id=dev, device_id_type=pl.DeviceIdType.MESH)
                cp.wait_send()
                cp.wait_recv()

    hbm = lambda a: pltpu.HBM(a.shape, a.dtype)
    outs = pl.pallas_call(
        body, name=name,
        out_shape=(*[hbm(x) for x in srcs], *[hbm(a) for a in lands]),
        in_specs=[HBM_SPEC] * (2 * n) + [SEM_SPEC, SEM_SPEC, ANY_SPEC],
        out_specs=tuple([HBM_SPEC] * (2 * n)),
        input_output_aliases={k: k for k in range(2 * n)},
        compiler_params=pltpu.CompilerParams(has_side_effects=pltpu.SideEffectType.DATAFLOW_SIDE_EFFECTING),
    )(*srcs, *lands, send_sems, recv_sems, after)
    return list(outs[n:])


def _mods_fwd(c_all, ada_w, ada_b_loc, kv_ada_w, kv_ada_b_loc):
    nl, d, wa = ada_w.shape
    wk = kv_ada_w.shape[1]

    def body(c_ref, w_ref, b_ref, kw_ref, kb_ref, o_ref):
        c = c_ref[...]
        ca = c * _sigmoid(c)
        for l in range(nl):
            o_ref[:, l * wa:(l + 1) * wa] = jnp.dot(
                ca, w_ref[l], preferred_element_type=F32, precision=lax.Precision.HIGHEST) + b_ref[l]
        o_ref[:, nl * wa:] = jnp.dot(
            ca, kw_ref[...], preferred_element_type=F32, precision=lax.Precision.HIGHEST) + kb_ref[...]

    return pl.pallas_call(body, name="mods_fwd", out_shape=_sds((N_DEV, nl * wa + wk), F32),
                          compiler_params=_cp())(c_all, ada_w, ada_b_loc, kv_ada_w, kv_ada_b_loc)


def _mods_bwd(c_all, dm, nl, wa, wk):
    d = c_all.shape[1]

    def body(c_ref, d_ref, gw_ref, gk_ref):
        c = c_ref[...]
        ca = c * _sigmoid(c)
        for l in range(nl):
            gw_ref[l] = lax.dot_general(ca, d_ref[:, l * wa:(l + 1) * wa], TN,
                                        preferred_element_type=F32, precision=lax.Precision.HIGHEST)
        gk_ref[...] = lax.dot_general(ca, d_ref[:, nl * wa:], TN,
                                      preferred_element_type=F32, precision=lax.Precision.HIGHEST)

    return pl.pallas_call(body, name="mods_bwd", out_shape=[_sds((nl, d, wa), F32), _sds((d, wk), F32)],
                          compiler_params=_cp())(c_all, dm)


def _norm_mod(hf, g, shift, scale):
    rs = lax.rsqrt(jnp.mean(hf * hf, axis=-1, keepdims=True) + EPS)
    xhat = hf * rs
    n = xhat * g
    return rs, xhat, n, n * (1.0 + scale) + shift


def _inproj_fwd(h, g, shift, scale, w, name, pad=0):
    s, d = h.shape
    nj, _, wn = w.shape
    half = nj // 2
    tm = _row_tile(s, 512)
    assert pad in (0, tm)
    pb = pad // tm

    def body(h_ref, g_ref, sh_ref, sc_ref, w_ref, u_ref, l_ref, r_ref):
        _, _, _, u = _norm_mod(h_ref[...], g_ref[...], sh_ref[...], sc_ref[...])
        ub = u.astype(BF16)
        u_ref[...] = ub
        for j in range(nj):
            o_ref = l_ref if j < half else r_ref
            jj = j % half
            o_ref[:, jj * wn:(jj + 1) * wn] = jnp.dot(ub, w_ref[j], preferred_element_type=F32).astype(BF16)
        if pb:
            @pl.when(pl.program_id(0) == 0)
            def _():
                l_ref[...] = jnp.zeros_like(l_ref)
                r_ref[...] = jnp.zeros_like(r_ref)

    e = half * wn
    src = lambda i: (jnp.maximum(i - pb, 0), 0)
    return pl.pallas_call(
        body, name=name, grid=(s // tm + pb,),
        in_specs=[pl.BlockSpec((tm, d), src), _vec(d), _vec(d), _vec(d),
                  pl.BlockSpec((nj, d, wn), lambda i: (0, 0, 0))],
        out_specs=[pl.BlockSpec((tm, d), src), pl.BlockSpec((tm, e), lambda i: (i, 0)),
                   pl.BlockSpec((tm, e), lambda i: (i, 0))],
        out_shape=[_sds((s, d), BF16), _sds((s + pad, e), BF16), _sds((s + pad, e), BF16)],
        compiler_params=_cp("arbitrary"),
    )(h, g, shift, scale, w)


def _pool_fwd(val, wg):
    s, e = val.shape
    ng = len(POOL_WINDOWS)
    gw = e // ng
    tm = _row_tile(s, 256)
    hb = tm // POOL_HALO

    def body(v_ref, halo_ref, w_ref, p_ref, m_ref):
        i = pl.program_id(0)
        t = i * tm + lax.broadcasted_iota(jnp.int32, (tm, 1), 0)
        for g, wdw in enumerate(POOL_WINDOWS):
            cols = slice(g * gw, (g + 1) * gw)
            v = v_ref[:, cols].astype(F32)
            halo = jnp.where(i > 0, halo_ref[:, cols].astype(F32), 0.0)
            acc = jnp.concatenate([halo, v], axis=0)
            sh = 1
            while sh < wdw:
                acc = acc + pltpu.roll(acc, sh, 0)
                sh *= 2
            cnt = jnp.minimum(t + 1, wdw).astype(F32)
            pb = (acc[POOL_HALO:, :] / cnt - v).astype(BF16)
            p_ref[:, cols] = pb
            m_ref[:, cols] = jnp.dot(pb, w_ref[:, g].reshape(gw, gw), preferred_element_type=F32).astype(BF16)

    return pl.pallas_call(
        body, name="pool_fwd", grid=(s // tm,),
        in_specs=[pl.BlockSpec((tm, e), lambda i: (i, 0)),
                  pl.BlockSpec((POOL_HALO, e), lambda i: (jnp.maximum(i * hb - 1, 0), 0)),
                  pl.BlockSpec((N_DEV, ng, gw // N_DEV, gw), lambda i: (0, 0, 0, 0))],
        out_specs=[pl.BlockSpec((tm, e), lambda i: (i, 0)), pl.BlockSpec((tm, e), lambda i: (i, 0))],
        out_shape=[_sds((s, e), BF16), _sds((s, e), BF16)],
        compiler_params=_cp("arbitrary"),
    )(val, val, wg)


def _gate_out_fwd(m, z, ascale, w, h, gate, name):
    s, e = m.shape
    d = h.shape[1]
    tm = _row_tile(s, 256)

    def body(m_ref, z_ref, a_ref, w_ref, h_ref, g_ref, gd_ref, y_ref, ho_ref):
        z = z_ref[...].astype(F32)
        gb = ((m_ref[...].astype(F32) * a_ref[...]) * (z * _sigmoid(z))).astype(BF16)
        gd_ref[...] = gb
        y = jnp.dot(gb, w_ref[...].reshape(e, d), preferred_element_type=F32)
        y_ref[...] = y.astype(BF16)
        ho_ref[...] = h_ref[...] + g_ref[...] * y

    return pl.pallas_call(
        body, name=name, grid=(s // tm,),
        in_specs=[pl.BlockSpec((tm, e), lambda i: (i, 0)), pl.BlockSpec((tm, e), lambda i: (i, 0)), _vec(e),
                  pl.BlockSpec((N_DEV, e // N_DEV, d), lambda i: (0, 0, 0)),
                  pl.BlockSpec((tm, d), lambda i: (i, 0)), _vec(d)],
        out_specs=[pl.BlockSpec((tm, e), lambda i: (i, 0)), pl.BlockSpec((tm, d), lambda i: (i, 0)),
                   pl.BlockSpec((tm, d), lambda i: (i, 0))],
        out_shape=[_sds((s, e), BF16), _sds((s, d), BF16), _sds((s, d), F32)],
        compiler_params=_cp("arbitrary"),
    )(m, z, ascale, w, h, gate)


def _rel_onehot(shape, r_axis):
    r = lax.broadcasted_iota(jnp.int32, shape, r_axis)
    j = lax.broadcasted_iota(jnp.int32, shape, 1 - r_axis)
    dist = LEFT - (j - TQ)
    return (jnp.clip(dist, -REL_CLIP, REL_CLIP) + REL_CLIP == r).astype(F32)


def _skew(x, sign):
    row = lax.broadcasted_iota(jnp.int32, x.shape, 0)
    for b in range(TQ.bit_length() - 1):
        amt = (1 << b) if sign > 0 else FRAME_PAD - (1 << b)
        x = jnp.where(((row >> b) & 1) == 1, pltpu.roll(x, amt, 1), x)
    return x


def _bias_frames(rel):
    nh = rel.shape[0]

    def body(r_ref, f_ref, ft_ref):
        fext = jnp.dot(r_ref[...], _rel_onehot((REL_PAD, FRAME_PAD), 0), preferred_element_type=F32,
                       precision=lax.Precision.HIGHEST)
        x = _skew(jnp.broadcast_to(fext, (TQ, FRAME_PAD)), 1)[:, TQ:]
        qc = lax.broadcasted_iota(jnp.int32, (TQ, BAND), 0) // CHUNK
        m = lax.broadcasted_iota(jnp.int32, (TQ, BAND), 1)
        mc = m // CHUNK
        x = jnp.where((mc >= qc) & (mc <= qc + LEFT_CHUNKS), x, NEG)
        for f in range(NFRAME):
            xf = jnp.where(m >= LEFT - f * TQ, x, NEG)
            f_ref[f] = xf
            ft_ref[f] = xf.T

    return pl.pallas_call(
        body, name="bias_frames", grid=(nh,),
        in_specs=[pl.BlockSpec((None, 1, REL_PAD), lambda h: (h, 0, 0))],
        out_specs=[pl.BlockSpec((None, NFRAME, TQ, BAND), lambda h: (h, 0, 0, 0)),
                   pl.BlockSpec((None, NFRAME, BAND, TQ), lambda h: (h, 0, 0, 0))],
        out_shape=[_sds((nh, NFRAME, TQ, BAND), F32), _sds((nh, NFRAME, BAND, TQ), F32)],
        compiler_params=_cp("arbitrary"),
    )(rel)


def _bias_bwd(dft):
    nh = dft.shape[0]

    def body(d_ref, o_ref):
        d = d_ref[0]
        for f in range(1, NFRAME):
            d = d + d_ref[f]
        x = jnp.concatenate([jnp.zeros((TQ, TQ), F32), d.T], axis=1)
        col = jnp.sum(_skew(x, -1), axis=0, keepdims=True)
        o_ref[...] = jnp.dot(col, _rel_onehot((FRAME_PAD, REL_PAD), 1), preferred_element_type=F32,
                             precision=lax.Precision.HIGHEST)

    return pl.pallas_call(
        body, name="bias_bwd", grid=(nh,),
        in_specs=[pl.BlockSpec((None, NFRAME, BAND, TQ), lambda h: (h, 0, 0, 0))],
        out_specs=pl.BlockSpec((None, 1, REL_PAD), lambda h: (h, 0, 0)),
        out_shape=_sds((nh, 1, REL_PAD), F32),
        compiler_params=_cp("arbitrary"),
    )(dft)


def _attn_tiles(s):
    nt = s // TQ
    ni = min(ATTN_GROUP, nt)
    return nt // ni, ni


def _attn_fwd(q, k, v, frame):
    s, e = q.shape
    dh = e // N_HEADS
    ng, ni = _attn_tiles(s)
    sm = dh ** -0.5
    nb = min(ATTN_BATCH, ni)

    def body(q_ref, k_ref, v_ref, b_ref, o_ref):
        g0 = pl.program_id(1) * ni
        for tb in range(0, ni, nb):
            ts = range(tb, tb + nb)
            keys = [pl.ds(pl.multiple_of((g0 + t) * TQ, TQ), BAND) for t in ts]
            rows = [pl.ds(t * TQ, TQ) for t in ts]
            scs = [lax.dot_general(q_ref[r, :], k_ref[kk, :], NT, preferred_element_type=F32)
                   for r, kk in zip(rows, keys)]
            ps, ls = [], []
            for t, sc in zip(ts, scs):
                sc = sc * sm + b_ref[jnp.minimum(g0 + t, NFRAME - 1)]
                p = jnp.exp(sc - jnp.max(sc, axis=-1, keepdims=True))
                ls.append(jnp.sum(p, axis=-1, keepdims=True))
                ps.append(p.astype(BF16))
            for r, p, l, kk in zip(rows, ps, ls, keys):
                o = jnp.dot(p, v_ref[kk, :], preferred_element_type=F32)
                o_ref[r, :] = (o / l).astype(BF16)

    return pl.pallas_call(
        body, name="attn_fwd", grid=(N_HEADS, ng),
        in_specs=[pl.BlockSpec((ni * TQ, dh), lambda h, i: (i, h)),
                  pl.BlockSpec((s + LEFT, dh), lambda h, i: (0, h)), pl.BlockSpec((s + LEFT, dh), lambda h, i: (0, h)),
                  pl.BlockSpec((None, NFRAME, TQ, BAND), lambda h, i: (h, 0, 0, 0))],
        out_specs=pl.BlockSpec((ni * TQ, dh), lambda h, i: (i, h)),
        out_shape=_sds((s, e), BF16),
        compiler_params=_cp("arbitrary", "arbitrary"),
    )(q, k, v, frame)


def _final_loss(h, g, target):
    s, d = h.shape
    tm = _row_tile(s, 512)

    def body(h_ref, g_ref, t_ref, dh_ref, dg_ref, l_ref):
        @pl.when(pl.program_id(0) == 0)
        def _():
            dg_ref[...] = jnp.zeros_like(dg_ref)
            l_ref[...] = jnp.zeros_like(l_ref)

        hf = h_ref[...]
        gg = g_ref[...]
        rs = lax.rsqrt(jnp.mean(hf * hf, axis=-1, keepdims=True) + EPS)
        xhat = hf * rs
        diff = xhat * gg - t_ref[...]
        l_ref[...] += 0.5 * jnp.sum(jnp.mean(diff * diff, axis=-1, keepdims=True), axis=0, keepdims=True)
        dout = diff * (1.0 / d)
        dg_ref[...] += jnp.sum(dout * xhat, axis=0, keepdims=True)
        dxh = dout * gg
        dh_ref[...] = rs * (dxh - xhat * jnp.mean(dxh * xhat, axis=-1, keepdims=True))

    return pl.pallas_call(
        body, name="final_loss", grid=(s // tm,),
        in_specs=[pl.BlockSpec((tm, d), lambda i: (i, 0)), _vec(d), pl.BlockSpec((tm, d), lambda i: (i, 0))],
        out_specs=[pl.BlockSpec((tm, d), lambda i: (i, 0)), _vec(d), _vec(LANES)],
        out_shape=[_sds((s, d), F32), _sds((1, d), F32), _sds((1, LANES), F32)],
        compiler_params=_cp("arbitrary"),
    )(h, g, target)


def _out_bwd(dh, y, gate, w, m, z, ascale, name):
    s, d = dh.shape
    e = m.shape[1]
    tm = _row_tile(s, 256)

    def body(dh_ref, y_ref, g_ref, w_ref, m_ref, z_ref, a_ref, dy_ref, dm_ref, dz_ref, dg_ref, da_ref):
        @pl.when(pl.program_id(0) == 0)
        def _():
            dg_ref[...] = jnp.zeros_like(dg_ref)
            da_ref[...] = jnp.zeros_like(da_ref)

        dhf = dh_ref[...]
        dg_ref[...] += jnp.sum(dhf * y_ref[...].astype(F32), axis=0, keepdims=True)
        dyb = (g_ref[...] * dhf).astype(BF16)
        dy_ref[...] = dyb
        dgated = lax.dot_general(dyb, w_ref[...].reshape(e, d), NT, preferred_element_type=F32)
        z = z_ref[...].astype(F32)
        sig = _sigmoid(z)
        mf = m_ref[...].astype(F32)
        a = a_ref[...]
        dms = dgated * (z * sig)
        da_ref[...] += jnp.sum(dms * mf, axis=0, keepdims=True)
        dm_ref[...] = (dms * a).astype(BF16)
        dz_ref[...] = (dgated * (mf * a) * (sig * (1.0 + z * (1.0 - sig)))).astype(BF16)

    return pl.pallas_call(
        body, name=name, grid=(s // tm,),
        in_specs=[pl.BlockSpec((tm, d), lambda i: (i, 0)), pl.BlockSpec((tm, d), lambda i: (i, 0)), _vec(d),
                  pl.BlockSpec((N_DEV, e // N_DEV, d), lambda i: (0, 0, 0)),
                  pl.BlockSpec((tm, e), lambda i: (i, 0)), pl.BlockSpec((tm, e), lambda i: (i, 0)), _vec(e)],
        out_specs=[pl.BlockSpec((tm, d), lambda i: (i, 0)), pl.BlockSpec((tm, e), lambda i: (i, 0)),
                   pl.BlockSpec((tm, e), lambda i: (i, 0)), _vec(d), _vec(e)],
        out_shape=[_sds((s, d), BF16), _sds((s, e), BF16), _sds((s, e), BF16), _sds((1, d), F32), _sds((1, e), F32)],
        compiler_params=_cp("arbitrary"),
    )(dh, y, gate, w, m, z, ascale)


def _pool_bwd(dm, wg):
    s, e = dm.shape
    ng = len(POOL_WINDOWS)
    gw = e // ng
    tm = _row_tile(s, 256)
    hb = tm // POOL_HALO
    nsteps = s // tm

    def body(d_ref, halo_ref, w_ref, o_ref):
        i = pl.program_id(0)
        t = i * tm + lax.broadcasted_iota(jnp.int32, (tm + POOL_HALO, 1), 0)
        for g, wdw in enumerate(POOL_WINDOWS):
            cols = slice(g * gw, (g + 1) * gw)
            dmx = jnp.concatenate([d_ref[:, cols], halo_ref[:, cols]], axis=0)
            dp = lax.dot_general(dmx, w_ref[:, g].reshape(gw, gw), NT, preferred_element_type=F32)
            dp = jnp.where(t < s, dp, 0.0)
            acc = dp / jnp.minimum(t + 1, wdw).astype(F32)
            sh = 1
            while sh < wdw:
                acc = acc + pltpu.roll(acc, tm + POOL_HALO - sh, 0)
                sh *= 2
            o_ref[:, cols] = (acc[:tm, :] - dp[:tm, :]).astype(BF16)

    return pl.pallas_call(
        body, name="pool_bwd", grid=(nsteps,),
        in_specs=[pl.BlockSpec((tm, e), lambda i: (i, 0)),
                  pl.BlockSpec((POOL_HALO, e), lambda i: (jnp.minimum((i + 1) * hb, s // POOL_HALO - 1), 0)),
                  pl.BlockSpec((N_DEV, ng, gw // N_DEV, gw), lambda i: (0, 0, 0, 0))],
        out_specs=pl.BlockSpec((tm, e), lambda i: (i, 0)),
        out_shape=_sds((s, e), BF16),
        compiler_params=_cp("arbitrary"),
    )(dm, dm, wg)


def _attn_bwd(q, k, v, do, frame_t):
    s, e = q.shape
    dh = e // N_HEADS
    ng, ni = _attn_tiles(s)
    sm = dh ** -0.5
    nleft = LEFT // TQ

    def body(q_ref, k_ref, v_ref, do_ref, b_ref, dq_ref, dk_ref, dv_ref, db_ref):
        i = pl.program_id(1)
        g0 = i * ni

        @pl.when(i == 0)
        def _():
            dk_ref[...] = jnp.zeros_like(dk_ref)
            dv_ref[...] = jnp.zeros_like(dv_ref)
            db_ref[...] = jnp.zeros_like(db_ref)

        keys = [pl.ds(pl.multiple_of((g0 + t) * TQ, TQ), BAND) for t in range(ni)]
        rows = [pl.ds(t * TQ, TQ) for t in range(ni)]
        sts = [lax.dot_general(k_ref[keys[t], :], q_ref[rows[t], :], NT, preferred_element_type=F32)
               for t in range(ni)]
        dpts = [lax.dot_general(v_ref[keys[t], :], do_ref[rows[t], :], NT, preferred_element_type=F32)
                for t in range(ni)]
        pbs, dsbs, dsts = [], [], []
        for t in range(ni):
            st = sts[t] * sm + b_ref[jnp.minimum(g0 + t, NFRAME - 1)]
            p = jnp.exp(st - jnp.max(st, axis=0, keepdims=True))
            p = p * (1.0 / jnp.sum(p, axis=0, keepdims=True))
            dst = p * (dpts[t] - jnp.sum(dpts[t] * p, axis=0, keepdims=True))
            pbs.append(p.astype(BF16))
            dsbs.append(dst.astype(BF16))
            dsts.append(dst)
        for t in range(min(ni, nleft)):
            db_ref[jnp.minimum(g0 + t, NFRAME - 1)] += dsts[t]
        if ni > nleft:
            rest = dsts[nleft]
            for t in range(nleft + 1, ni):
                rest = rest + dsts[t]
            db_ref[NFRAME - 1] += rest
        for t in range(ni):
            dq = lax.dot_general(dsbs[t], k_ref[keys[t], :], TN, preferred_element_type=F32)
            dq_ref[rows[t], :] = (dq * sm).astype(BF16)
        for r in range(ni + nleft):
            ts = [t for t in range(ni) if 0 <= r - t <= nleft]
            blk = lambda xs: jnp.concatenate([xs[t][(r - t) * TQ:(r - t + 1) * TQ, :] for t in ts], axis=1)
            qrows = slice(ts[0] * TQ, (ts[-1] + 1) * TQ)
            krows = pl.ds(pl.multiple_of((g0 + r) * TQ, TQ), TQ)
            dk_ref[krows, :] += jnp.dot(blk(dsbs), q_ref[qrows, :], preferred_element_type=F32) * sm
            dv_ref[krows, :] += jnp.dot(blk(pbs), do_ref[qrows, :], preferred_element_type=F32)

    kv_spec = pl.BlockSpec((s + LEFT, dh), lambda h, i: (0, h))
    fr_spec = pl.BlockSpec((None, NFRAME, BAND, TQ), lambda h, i: (h, 0, 0, 0))
    return pl.pallas_call(
        body, name="attn_bwd", grid=(N_HEADS, ng),
        in_specs=[pl.BlockSpec((ni * TQ, dh), lambda h, i: (i, h)), kv_spec, kv_spec,
                  pl.BlockSpec((ni * TQ, dh), lambda h, i: (i, h)), fr_spec],
        out_specs=[pl.BlockSpec((ni * TQ, dh), lambda h, i: (i, h)), kv_spec, kv_spec, fr_spec],
        out_shape=[_sds((s, e), BF16), _sds((s + LEFT, e), F32), _sds((s + LEFT, e), F32),
                   _sds((N_HEADS, NFRAME, BAND, TQ), F32)],
        compiler_params=_cp("arbitrary", "arbitrary"),
    )(q, k, v, do, frame_t)


def _add_cast(a, b, skip):
    s, e = a.shape[0] - skip, a.shape[1]
    tm = _row_tile(s, 512)
    assert skip % tm == 0
    off = skip // tm

    def body(a_ref, b_ref, o_ref):
        o_ref[...] = (a_ref[...] + b_ref[...]).astype(BF16)

    return pl.pallas_call(
        body, name="add_cast", grid=(s // tm,),
        in_specs=[pl.BlockSpec((tm, e), lambda i: (i + off, 0))] * 2,
        out_specs=pl.BlockSpec((tm, e), lambda i: (i, 0)), out_shape=_sds((s, e), BF16),
        compiler_params=_cp("arbitrary"),
    )(a, b)


def _in_bwd(dl, dr, w, h, g, scale, dres, name):
    s, d = h.shape
    nj, _, wn = w.shape
    half = nj // 2
    e = half * wn
    tm = _row_tile(s, 256)

    def body(dl_ref, dr_ref, w_ref, h_ref, g_ref, sc_ref, res_ref, dh_ref, st_ref):
        @pl.when(pl.program_id(0) == 0)
        def _():
            st_ref[...] = jnp.zeros_like(st_ref)

        du = jnp.zeros((tm, d), F32)
        for j in range(nj):
            src = dl_ref if j < half else dr_ref
            jj = j % half
            du = du + lax.dot_general(src[:, jj * wn:(jj + 1) * wn], w_ref[j], NT, preferred_element_type=F32)
        gg = g_ref[...]
        rs, xhat, n, _ = _norm_mod(h_ref[...], gg, 0.0, 0.0)
        dn = du * (1.0 + sc_ref[...])
        st_ref[0:1, :] += jnp.sum(du, axis=0, keepdims=True)
        st_ref[1:2, :] += jnp.sum(du * n, axis=0, keepdims=True)
        st_ref[2:3, :] += jnp.sum(dn * xhat, axis=0, keepdims=True)
        dxh = dn * gg
        dh_ref[...] = rs * (dxh - xhat * jnp.mean(dxh * xhat, axis=-1, keepdims=True)) + res_ref[...]

    return pl.pallas_call(
        body, name=name, grid=(s // tm,),
        in_specs=[pl.BlockSpec((tm, e), lambda i: (i, 0)), pl.BlockSpec((tm, e), lambda i: (i, 0)),
                  pl.BlockSpec((nj, d, wn), lambda i: (0, 0, 0)),
                  pl.BlockSpec((tm, d), lambda i: (i, 0)), _vec(d), _vec(d), pl.BlockSpec((tm, d), lambda i: (i, 0))],
        out_specs=[pl.BlockSpec((tm, d), lambda i: (i, 0)), pl.BlockSpec((8, d), lambda i: (0, 0))],
        out_shape=[_sds((s, d), F32), _sds((8, d), F32)],
        compiler_params=_cp("arbitrary"),
    )(dl, dr, w, h, g, scale, dres)


def _tn_matmul(x, ys, xw, yw, ymap, nb, out_shape, stage_shape, out_at, name):
    s, xfull = x.shape
    ts = _row_tile(s, 2048)
    ys = list(ys)
    half = nb // len(ys)
    nk = s // ts

    def body(*refs):
        x_ref, y_refs = refs[0], refs[1:1 + len(ys)]
        o_hbm, xt_ref, acc_ref, stage_ref, sem = refs[1 + len(ys):]
        k, b = pl.program_id(0), pl.program_id(1)

        @pl.when(b == 0)
        def _():
            xt_ref[...] = x_ref[...].T

        for n, y_ref in enumerate(y_refs):
            @pl.when((b >= n * half) & (b < (n + 1) * half))
            def _():
                xt = xt_ref[...] if xw == xfull else xt_ref[pl.ds(pl.multiple_of(b * xw, xw), xw), :]
                part = jnp.dot(xt, y_ref[...], preferred_element_type=F32)

                @pl.when(k == 0)
                def _():
                    acc_ref[b] = part

                @pl.when(k > 0)
                def _():
                    acc_ref[b] += part

        @pl.when(k == nk - 1)
        def _():
            stage_ref[...] = acc_ref[b].astype(BF16).reshape(stage_shape)
            cp = pltpu.make_async_copy(stage_ref, out_at(o_hbm, b), sem)
            cp.start()
            cp.wait()

    in_specs = [pl.BlockSpec((ts, xfull), lambda k, b: (k, 0))]
    for n in range(len(ys)):
        in_specs.append(pl.BlockSpec(
            (ts, yw), lambda k, b, n=n: (k, ymap(jnp.clip(b - n * half, 0, half - 1)))))
    return pl.pallas_call(
        body, name=name, grid=(nk, nb), in_specs=in_specs,
        out_specs=HBM_SPEC, out_shape=_sds(out_shape, BF16),
        scratch_shapes=[pltpu.VMEM((xfull, ts), BF16), pltpu.VMEM((nb, xw, yw), F32), pltpu.VMEM(stage_shape, BF16),
                        pltpu.SemaphoreType.DMA(())],
        compiler_params=_cp("arbitrary", "arbitrary"),
    )(x, *ys)


def _grad_w_in(u, dl, dr, name):
    d = u.shape[1]
    wn = 2 * dl.shape[1] // N_DEV
    return _tn_matmul(u, (dl, dr), d, wn, lambda b: b, N_DEV, (N_DEV, d, wn), (d, wn), lambda o, b: o.at[b], name)


def _grad_w_out(gated, dy, name):
    e, d = gated.shape[1], dy.shape[1]
    return _tn_matmul(gated, (dy,), e // N_DEV, d, lambda b: 0, N_DEV, (N_DEV, e // N_DEV, d), (e // N_DEV, d),
                      lambda o, b: o.at[b], name)


def _grad_w_group(pooled, dm, name):
    ng = len(POOL_WINDOWS)
    gw = pooled.shape[1] // ng
    return _tn_matmul(pooled, (dm,), gw, gw, lambda b: b, ng, (N_DEV, ng, gw // N_DEV, gw),
                      (N_DEV, gw // N_DEV, gw), lambda o, b: o.at[:, b], name)


def _adamw(staged, w, m, v, name):
    shape = w.shape
    nl = len(staged)
    n = staged[0].shape[0]
    cdim = shape[-1]
    total = 1
    for a in shape[:-1]:
        total *= a
    rows = total // nl
    sts = [st.reshape(n, rows, cdim) for st in staged]
    tr = rows if rows * cdim <= 128 * 1024 else max(8, (128 * 1024 // cdim) // 8 * 8)
    while rows % tr:
        tr -= 8
    nblk = rows // tr

    def body(*refs):
        s_refs = refs[:nl]
        w_ref, m_ref, v_ref, g_ref, d_ref, mo_ref, vo_ref = refs[nl:]
        for ll in range(nl):
            @pl.when(pl.program_id(0) == ll)
            def _():
                g = s_refs[ll][0].astype(F32)
                for j in range(1, n):
                    g = g + s_refs[ll][j].astype(F32)
                mn = ADAM_B1 * m_ref[...] + (1.0 - ADAM_B1) * g
                vn = ADAM_B2 * v_ref[...] + (1.0 - ADAM_B2) * (g * g)
                m_hat = mn / (1.0 - ADAM_B1 ** ADAM_STEP)
                v_hat = vn / (1.0 - ADAM_B2 ** ADAM_STEP)
                g_ref[...] = g
                d_ref[...] = -ADAM_LR * (m_hat / (jnp.sqrt(v_hat) + ADAM_EPS) + ADAM_WD * w_ref[...])
                mo_ref[...] = mn
                vo_ref[...] = vn

    blk = pl.BlockSpec((None, tr, cdim), lambda l, i: (l, i, 0))
    st_specs = [pl.BlockSpec((n, tr, cdim), lambda l, i, ll=ll: (0, jnp.clip(i + (l - ll) * nblk, 0, nblk - 1), 0))
                for ll in range(nl)]
    outs = pl.pallas_call(
        body, name=name, grid=(nl, nblk),
        in_specs=st_specs + [blk, blk, blk],
        out_specs=[blk] * 4, out_shape=[_sds((nl, rows, cdim), F32)] * 4,
        compiler_params=_cp("arbitrary", "arbitrary"),
    )(*sts, w.reshape(nl, rows, cdim), m.reshape(nl, rows, cdim), v.reshape(nl, rows, cdim))
    return [o.reshape(shape) for o in outs]


def _pack(parts, total):
    flat = jnp.concatenate([p.reshape(-1) for p in parts])
    return jnp.pad(flat, (0, total - flat.shape[0])).reshape(1, total)


def kernel(x, c, ada_w, ada_b, norm_g, a_w_in, a_w_group, a_scale, a_w_out, kv_norm_g, kv_ada_w, kv_ada_b, w_kv, b_w_in, b_rel_bias, b_w_out, final_g, loss_target, m_ada_w, m_ada_b, m_norm_g, m_a_w_in, m_a_w_group, m_a_scale, m_a_w_out, m_kv_norm_g, m_kv_ada_w, m_kv_ada_b, m_w_kv, m_b_w_in, m_b_rel_bias, m_b_w_out, m_final_g, v_ada_w, v_ada_b, v_norm_g, v_a_w_in, v_a_w_group, v_a_scale, v_a_w_out, v_kv_norm_g, v_kv_ada_w, v_kv_ada_b, v_w_kv, v_b_w_in, v_b_rel_bias, v_b_w_out, v_final_g):
    s, d = x.shape[1], x.shape[2]
    depth, _, wa = ada_w.shape
    wk = kv_ada_w.shape[1]
    n_a, n_b = a_w_in.shape[0], b_w_in.shape[0]
    e = a_w_out.shape[1] * N_DEV
    nrel = b_rel_bias.shape[-1]
    me = 4 * lax.axis_index("x") + 2 * lax.axis_index("y") + lax.axis_index("c")
    h0 = x[0]
    target = loss_target[0]

    c_all = _exchange([c.reshape(1, 1, d)], [True], "gather_c")[0].reshape(N_DEV, d)
    ada_b_loc = lax.dynamic_slice(ada_b, (0, me * wa), (depth, wa)).reshape(depth, 1, wa)
    kv_ada_b_loc = lax.dynamic_slice(kv_ada_b, (me * wk,), (wk,)).reshape(1, wk)
    mods_cols = _mods_fwd(c_all, ada_w, ada_b_loc, kv_ada_w, kv_ada_b_loc)
    nm = depth * wa + wk
    mods_me, a_scale_all = _exchange([mods_cols.reshape(N_DEV, 1, nm), a_scale], [False, True], "exchange_mods")
    bf = lambda a: a.astype(BF16)
    groups = [[bf(a_w_in[l]), bf(a_w_group[l]), bf(a_w_out[l])] for l in range(n_a)]
    groups.append([bf(w_kv), bf(b_w_in[0]), bf(b_w_out[0])])
    groups += [[bf(b_w_in[l]), bf(b_w_out[l])] for l in range(1, n_b)]
    handles = []
    token = mods_me
    for gi, grp in enumerate(groups):
        hd, token = _push_start(grp, [True] * len(grp), token, f"gather_start_{gi}")
        handles.append(hd)
    mods_me = mods_me.reshape(N_DEV, nm)
    mods = mods_me[:, :depth * wa].reshape(N_DEV, depth, wa).transpose(1, 0, 2).reshape(depth, 3, 1, d)
    kv_mods = mods_me[:, depth * wa:].reshape(2, 1, d)
    a_scale_full = a_scale_all.transpose(1, 0, 2).reshape(n_a, 1, e)
    ones_e = jnp.ones((1, e), F32)

    saved = []
    h = h0
    k = v = hk = wkv = None
    after = token
    for layer in range(depth):
        shift, scale, gate = mods[layer, 0], mods[layer, 1], mods[layer, 2]
        g = norm_g[layer].reshape(1, d)
        if layer < n_a:
            w_in, w_group, w_out = _push_wait(handles[layer], after, f"gather_wait_{layer}")
            u, left, z = _inproj_fwd(h, g, shift, scale, w_in, "inproj_fwd")
            pooled, mixed = _pool_fwd(left, w_group)
            gated, y, hn = _gate_out_fwd(mixed, z, a_scale_full[layer], w_out, h, gate, "gate_out_fwd")
            saved.append(dict(h=h, u=u, left=left, z=z, pooled=pooled, m=mixed, gated=gated, y=y,
                              w_in=w_in, w_group=w_group, w_out=w_out))
        else:
            bi = layer - n_a
            got = _push_wait(handles[layer], after, f"gather_wait_{layer}")
            if bi == 0:
                wkv, w_in, w_out = got
                hk, k, v = _inproj_fwd(h, kv_norm_g.reshape(1, d), kv_mods[0], kv_mods[1], wkv, "inproj_kv", pad=LEFT)
            else:
                w_in, w_out = got
            u, left, z = _inproj_fwd(h, g, shift, scale, w_in, "inproj_fwd")
            rel = jnp.pad(b_rel_bias[bi], ((0, 0), (0, REL_PAD - nrel))).reshape(N_HEADS, 1, REL_PAD)
            frame, frame_t = _bias_frames(rel)
            att = _attn_fwd(left, k, v, frame)
            gated, y, hn = _gate_out_fwd(att, z, ones_e, w_out, h, gate, "gate_out_fwd")
            saved.append(dict(h=h, u=u, left=left, z=z, m=att, gated=gated, y=y, frame_t=frame_t,
                              w_in=w_in, w_out=w_out))
        h = hn
        after = h
    dh, d_final_g, loss_part = _final_loss(h, final_g.reshape(1, d), target)

    d_mods = [None] * depth
    d_norm_g = [None] * depth
    d_rel = [None] * n_b
    d_ascale = [None] * n_a
    dk_parts, dv_parts = [], []
    tie = lambda val, tok: lax.optimization_barrier((val, tok))[0]
    grad_handles = [None] * (depth + 1)
    for layer in range(depth - 1, n_a - 1, -1):
        bi = layer - n_a
        sv = saved[layer]
        scale, gate = mods[layer, 1], mods[layer, 2]
        dy, datt, dz, d_gate, _ = _out_bwd(dh, sv["y"], gate, sv["w_out"], sv["m"], sv["z"], ones_e, "out_bwd")
        dq, dk, dv, dframe_t = _attn_bwd(sv["left"], k, v, datt, sv["frame_t"])
        dk_parts.append(dk)
        dv_parts.append(dv)
        d_rel[bi] = _bias_bwd(dframe_t).reshape(N_HEADS, REL_PAD)[:, :nrel]
        g_out = _grad_w_out(sv["gated"], dy, "grad_w_out")
        g_in = _grad_w_in(sv["u"], dq, dz, "grad_w_in")
        grad_handles[layer], tok = _push_start([g_in, g_out], [False, False], d_gate, f"grads_start_{layer}")
        dh, st = _in_bwd(tie(dq, tok), dz, sv["w_in"], sv["h"], norm_g[layer].reshape(1, d), scale, dh, "in_bwd")
        d_mods[layer] = jnp.concatenate([st[0], st[1], d_gate[0]])
        d_norm_g[layer] = st[2]
    dkb = _add_cast(dk_parts[0], dk_parts[1], LEFT)
    dvb = _add_cast(dv_parts[0], dv_parts[1], LEFT)
    h_kv = saved[n_a]["h"]
    g_w_kv = _grad_w_in(hk, dkb, dvb, "grad_w_in")
    grad_handles[depth], tok = _push_start([g_w_kv], [False], dkb, "grads_start_kv")
    dh, st = _in_bwd(tie(dkb, tok), dvb, wkv, h_kv, kv_norm_g.reshape(1, d), kv_mods[1], dh, "in_bwd")
    d_kv_mods = jnp.concatenate([st[0], st[1]])
    d_kv_norm_g = st[2]
    for layer in range(n_a - 1, -1, -1):
        sv = saved[layer]
        scale, gate = mods[layer, 1], mods[layer, 2]
        dy, dm, dz, d_gate, da = _out_bwd(dh, sv["y"], gate, sv["w_out"], sv["m"], sv["z"], a_scale_full[layer],
                                          "out_bwd")
        d_ascale[layer] = da.reshape(N_DEV, e // N_DEV)
        dval = _pool_bwd(dm, sv["w_group"])
        g_group = _grad_w_group(sv["pooled"], dm, "grad_w_group")
        g_out = _grad_w_out(sv["gated"], dy, "grad_w_out")
        g_in = _grad_w_in(sv["u"], dval, dz, "grad_w_in")
        grad_handles[layer], tok = _push_start([g_in, g_group, g_out], [False] * 3, d_gate, f"grads_start_{layer}")
        dh, st = _in_bwd(tie(dval, tok), dz, sv["w_in"], sv["h"], norm_g[layer].reshape(1, d), scale, dh, "in_bwd")
        d_mods[layer] = jnp.concatenate([st[0], st[1], d_gate[0]])
        d_norm_g[layer] = st[2]
    grad_x = dh.reshape(1, s, d)

    d_mods = jnp.stack(d_mods)
    dm_slots = jnp.concatenate(
        [d_mods.reshape(depth, N_DEV, wa).transpose(1, 0, 2).reshape(N_DEV, depth * wa), d_kv_mods.reshape(N_DEV, wk)],
        axis=1).reshape(N_DEV, 1, nm)
    small_parts = [d_mods, d_kv_mods, jnp.stack(d_norm_g), d_kv_norm_g, d_final_g, jnp.stack(d_rel), loss_part[0, :1]]
    n_small = sum(int(p.size) for p in small_parts)
    n_small_pad = -(-n_small // LANES) * LANES
    small = _pack(small_parts, n_small_pad).reshape(1, 1, n_small_pad)
    d_ascale_slots = jnp.stack(d_ascale, axis=1)
    s_a_scale, dm_cols, small_all = _exchange([d_ascale_slots, dm_slots, small], [False, False, True],
                                              "exchange_small")

    g_ada_w, g_kv_ada_w = _mods_bwd(c_all, dm_cols.reshape(N_DEV, nm), depth, wa, wk)
    res = {}
    res["ada_w"] = _adamw([g_ada_w[None]], ada_w, m_ada_w, v_ada_w, "adamw")
    res["kv_ada_w"] = _adamw([g_kv_ada_w[None]], kv_ada_w, m_kv_ada_w, v_kv_ada_w, "adamw")
    res["a_scale"] = _adamw([s_a_scale[:, l] for l in range(n_a)], a_scale, m_a_scale, v_a_scale, "adamw")
    after = res["ada_w"][1]
    s_b = [None] * n_b
    for layer in range(depth - 1, n_a - 1, -1):
        s_b[layer - n_a] = _push_wait(grad_handles[layer], after, f"grads_wait_{layer}")
        after = s_b[layer - n_a][0]
    res["b_w_in"] = _adamw([sb[0] for sb in s_b], b_w_in, m_b_w_in, v_b_w_in, "adamw")
    res["b_w_out"] = _adamw([sb[1] for sb in s_b], b_w_out, m_b_w_out, v_b_w_out, "adamw")
    s_w_kv, = _push_wait(grad_handles[depth], res["b_w_out"][1], "grads_wait_kv")
    res["w_kv"] = _adamw([s_w_kv], w_kv, m_w_kv, v_w_kv, "adamw")
    after = res["w_kv"][1]
    s_a = [None] * n_a
    for layer in range(n_a - 1, -1, -1):
        s_a[layer] = _push_wait(grad_handles[layer], after, f"grads_wait_{layer}")
        after = s_a[layer][0]
    res["a_w_in"] = _adamw([sa[0] for sa in s_a], a_w_in, m_a_w_in, v_a_w_in, "adamw")
    res["a_w_group"] = _adamw([sa[1] for sa in s_a], a_w_group, m_a_w_group, v_a_w_group, "adamw")
    res["a_w_out"] = _adamw([sa[2] for sa in s_a], a_w_out, m_a_w_out, v_a_w_out, "adamw")
    small_names = ["ada_b", "kv_ada_b", "norm_g", "kv_norm_g", "final_g", "b_rel_bias"]
    small_w = dict(ada_b=ada_b, kv_ada_b=kv_ada_b, norm_g=norm_g, kv_norm_g=kv_norm_g, final_g=final_g,
                   b_rel_bias=b_rel_bias)
    small_m = dict(ada_b=m_ada_b, kv_ada_b=m_kv_ada_b, norm_g=m_norm_g, kv_norm_g=m_kv_norm_g, final_g=m_final_g,
                   b_rel_bias=m_b_rel_bias)
    small_v = dict(ada_b=v_ada_b, kv_ada_b=v_kv_ada_b, norm_g=v_norm_g, kv_norm_g=v_kv_norm_g, final_g=v_final_g,
                   b_rel_bias=v_b_rel_bias)
    sw = _pack([small_w[n] for n in small_names], n_small_pad)
    smm = _pack([small_m[n] for n in small_names], n_small_pad)
    svv = _pack([small_v[n] for n in small_names] + [jnp.ones((n_small_pad - n_small + 1,), F32)], n_small_pad)
    small_out = _adamw([small_all.reshape(N_DEV, 1, n_small_pad)], sw, smm, svv, "adamw")
    off = 0
    for n in small_names:
        size = int(small_w[n].size)
        res[n] = [o[0, off:off + size].reshape(small_w[n].shape) for o in small_out]
        off += size
    loss = small_out[0][0, n_small - 1]

    order = ["ada_w", "ada_b", "norm_g", "a_w_in", "a_w_group", "a_scale", "a_w_out", "kv_norm_g", "kv_ada_w",
             "kv_ada_b", "w_kv", "b_w_in", "b_rel_bias", "b_w_out", "final_g"]
    outs = [loss, grad_x]
    for part in range(4):
        outs += [res[n][part] for n in order]
    return tuple(outs)
```

```python
import functools

import jax
import jax.numpy as jnp
from jax import lax
from jax.experimental import pallas as pl
from jax.experimental.pallas import tpu as pltpu

F32 = jnp.float32
BF16 = jnp.bfloat16

N_DEV = 8
CHUNK = 64
LEFT_CHUNKS = 8
N_HEADS = 16
POOL_WINDOWS = (2, 4, 8, 16)
REL_CLIP = 128
EPS = 1e-6
ADAM_LR = 0.001
ADAM_B1 = 0.9
ADAM_B2 = 0.999
ADAM_EPS = 1e-08
ADAM_WD = 0.01
ADAM_STEP = 10

LANES = 128
TQ = 2 * CHUNK
LEFT = LEFT_CHUNKS * CHUNK
BAND = LEFT + TQ
FRAME_PAD = BAND + TQ
NFRAME = LEFT // TQ + 1
POOL_HALO = 16
REL_PAD = 384
NEG = -1e30
ATTN_GROUP = 8
ATTN_BATCH = 4
VMEM_LIMIT = 56 * 1024 * 1024

NT = (((1,), (1,)), ((), ()))
TN = (((0,), (0,)), ((), ()))

HBM_SPEC = pl.BlockSpec(memory_space=pltpu.HBM)
ANY_SPEC = pl.BlockSpec(memory_space=pl.ANY)
SEM_SPEC = pl.BlockSpec(memory_space=pltpu.SEMAPHORE)


def _cp(*sem):
    return pltpu.CompilerParams(dimension_semantics=sem or None, vmem_limit_bytes=VMEM_LIMIT)


def _sds(shape, dtype):
    return jax.ShapeDtypeStruct(tuple(shape), dtype)


def _row_tile(s, want):
    return min(want, s)


def _vec(d):
    return pl.BlockSpec((1, d), lambda *_: (0, 0))


def _sigmoid(z):
    return 1.0 / (1.0 + jnp.exp(-z))


def _exchange(xs, gather, name):
    n = len(xs)
    out_shapes = []
    for x, ga in zip(xs, gather):
        out_shapes.append(_sds((N_DEV,) + (x.shape if ga else x.shape[1:]), x.dtype))

    def body(*refs):
        ins, outs = refs[:n], refs[n:2 * n]
        send_sems, recv_sems, local_sems = refs[2 * n:]
        mx, my, mc = lax.axis_index("x"), lax.axis_index("y"), lax.axis_index("c")
        me = 4 * mx + 2 * my + mc
        local = []
        for k in range(n):
            src = ins[k] if gather[k] else ins[k].at[me]
            cp = pltpu.make_async_copy(src, outs[k].at[me], local_sems.at[k])
            cp.start()
            local.append(cp)
        sends, recvs = [], []
        for r in range(1, N_DEV):
            px, py, pc = (mx + (r >> 2)) % 2, (my + ((r >> 1) & 1)) % 2, (mc + (r & 1)) % 2
            peer = 4 * px + 2 * py + pc
            for k in range(n):
                sem = k * (N_DEV - 1) + r - 1
                src = ins[k] if gather[k] else ins[k].at[peer]
                send = pltpu.make_async_remote_copy(
                    src_ref=src, dst_ref=outs[k].at[me], send_sem=send_sems.at[sem], recv_sem=recv_sems.at[sem],
                    device_id=(px, py, pc), device_id_type=pl.DeviceIdType.MESH)
                send.start()
                sends.append(send)
                recvs.append(pltpu.make_async_remote_copy(
                    src_ref=src, dst_ref=outs[k].at[peer], send_sem=send_sems.at[sem], recv_sem=recv_sems.at[sem],
                    device_id=(px, py, pc), device_id_type=pl.DeviceIdType.MESH))
        for cp in recvs:
            cp.wait_recv()
        for cp in sends:
            cp.wait_send()
        for cp in local:
            cp.wait()

    return pl.pallas_call(
        body, name=name, out_shape=out_shapes,
        in_specs=[HBM_SPEC] * n, out_specs=[HBM_SPEC] * n,
        scratch_shapes=[pltpu.SemaphoreType.DMA((n * (N_DEV - 1),)), pltpu.SemaphoreType.DMA((n * (N_DEV - 1),)),
                        pltpu.SemaphoreType.DMA((n,))],
    )(*xs)


def _peer(mx, my, mc, r):
    px, py, pc = (mx + (r >> 2)) % 2, (my + ((r >> 1) & 1)) % 2, (mc + (r & 1)) % 2
    return (px, py, pc), 4 * px + 2 * py + pc


def _push_start(xs, gather, dep, name):
    n = len(xs)
    nsem = n * (N_DEV - 1)
    me = 4 * lax.axis_index("x") + 2 * lax.axis_index("y") + lax.axis_index("c")
    lands = []
    for x, ga in zip(xs, gather):
        own = x[None] if ga else lax.dynamic_index_in_dim(x, me, 0, keepdims=True)
        empty = lax.empty((N_DEV,) + own.shape[1:], x.dtype)
        lands.append(lax.dynamic_update_slice(empty, own, (me,) + (0,) * (own.ndim - 1)))

    def body(*refs):
        ins, lands_in = refs[:n], refs[n:2 * n]
        send_sems, recv_sems = refs[2 * n + 1], refs[2 * n + 2]
        token = refs[-1]
        mx, my, mc = lax.axis_index("x"), lax.axis_index("y"), lax.axis_index("c")
        mine = 4 * mx + 2 * my + mc
        for k in range(n):
            for r in range(1, N_DEV):
                dev, peer = _peer(mx, my, mc, r)
                sem = k * (N_DEV - 1) + r - 1
                pltpu.make_async_remote_copy(
                    src_ref=ins[k] if gather[k] else ins[k].at[peer], dst_ref=lands_in[k].at[mine],
                    send_sem=send_sems.at[sem], recv_sem=recv_sems.at[sem],
                    device_id=dev, device_id_type=pl.DeviceIdType.MESH).start()
        token[...] = jnp.zeros_like(token)

    hbm = lambda a: pltpu.HBM(a.shape, a.dtype)
    outs = pl.pallas_call(
        body, name=name,
        out_shape=(pltpu.SemaphoreType.DMA((nsem,)), pltpu.SemaphoreType.DMA((nsem,)),
                   *[hbm(x) for x in xs], *[hbm(a) for a in lands], _sds((8, LANES), F32)),
        in_specs=[HBM_SPEC] * (2 * n) + [ANY_SPEC],
        out_specs=(SEM_SPEC, SEM_SPEC, *[HBM_SPEC] * (2 * n), pl.BlockSpec(memory_space=pltpu.VMEM)),
        input_output_aliases={k: 2 + k for k in range(2 * n)},
        compiler_params=pltpu.CompilerParams(has_side_effects=pltpu.SideEffectType.DATAFLOW_SIDE_EFFECTING),
    )(*[pltpu.with_memory_space_constraint(x, pltpu.HBM) for x in xs],
      *[pltpu.with_memory_space_constraint(a, pltpu.HBM) for a in lands], dep)
    return (outs[0], outs[1], outs[2:2 + n], outs[2 + n:2 + 2 * n], tuple(gather)), outs[-1]


def _push_wait(handle, after, name):
    send_sems, recv_sems, srcs, lands, gather = handle
    n = len(srcs)

    def body(*refs):
        ins, lands_in = refs[:n], refs[n:2 * n]
        send_sems, recv_sems = refs[2 * n], refs[2 * n + 1]
        mx, my, mc = lax.axis_index("x"), lax.axis_index("y"), lax.axis_index("c")
        for k in range(n):
            for r in range(1, N_DEV):
                dev, peer = _peer(mx, my, mc, r)
                sem = k * (N_DEV - 1) + r - 1
                cp = pltpu.make_async_remote_copy(
                    src_ref=ins[k] if gather[k] else ins[k].at[peer], dst_ref=lands_in[k].at[peer],
                    send_sem=send_sems.at[sem], recv_sem=recv_sems.at[sem],
                    device_id=dev, device_id_type=pl.DeviceIdType.MESH)
                cp.wait_send()
                cp.wait_recv()

    hbm = lambda a: pltpu.HBM(a.shape, a.dtype)
    outs = pl.pallas_call(
        body, name=name,
        out_shape=(*[hbm(x) for x in srcs], *[hbm(a) for a in lands]),
        in_specs=[HBM_SPEC] * (2 * n) + [SEM_SPEC, SEM_SPEC, ANY_SPEC],
        out_specs=tuple([HBM_SPEC] * (2 * n)),
        input_output_aliases={k: k for k in range(2 * n)},
        compiler_params=pltpu.CompilerParams(has_side_effects=pltpu.SideEffectType.DATAFLOW_SIDE_EFFECTING),
    )(*srcs, *lands, send_sems, recv_sems, after)
    return list(outs[n:])


def _mods_fwd(c_all, ada_w, ada_b_loc, kv_ada_w, kv_ada_b_loc):
    nl, d, wa = ada_w.shape
    wk = kv_ada_w.shape[1]

    def body(c_ref, w_ref, b_ref, kw_ref, kb_ref, o_ref):
        c = c_ref[...]
        ca = c * _sigmoid(c)
        for l in range(nl):
            o_ref[:, l * wa:(l + 1) * wa] = jnp.dot(
                ca, w_ref[l], preferred_element_type=F32, precision=lax.Precision.HIGHEST) + b_ref[l]
        o_ref[:, nl * wa:] = jnp.dot(
            ca, kw_ref[...], preferred_element_type=F32, precision=lax.Precision.HIGHEST) + kb_ref[...]

    return pl.pallas_call(body, name="mods_fwd", out_shape=_sds((N_DEV, nl * wa + wk), F32),
                          compiler_params=_cp())(c_all, ada_w, ada_b_loc, kv_ada_w, kv_ada_b_loc)


def _mods_bwd(c_all, dm, nl, wa, wk):
    d = c_all.shape[1]

    def body(c_ref, d_ref, gw_ref, gk_ref):
        c = c_ref[...]
        ca = c * _sigmoid(c)
        for l in range(nl):
            gw_ref[l] = lax.dot_general(ca, d_ref[:, l * wa:(l + 1) * wa], TN,
                                        preferred_element_type=F32, precision=lax.Precision.HIGHEST)
        gk_ref[...] = lax.dot_general(ca, d_ref[:, nl * wa:], TN,
                                      preferred_element_type=F32, precision=lax.Precision.HIGHEST)

    return pl.pallas_call(body, name="mods_bwd", out_shape=[_sds((nl, d, wa), F32), _sds((d, wk), F32)],
                          compiler_params=_cp())(c_all, dm)


def _norm_mod(hf, g, shift, scale):
    rs = lax.rsqrt(jnp.mean(hf * hf, axis=-1, keepdims=True) + EPS)
    xhat = hf * rs
    n = xhat * g
    return rs, xhat, n, n * (1.0 + scale) + shift


def _inproj_fwd(h, g, shift, scale, w, name, pad=0):
    s, d = h.shape
    nj, _, wn = w.shape
    half = nj // 2
    tm = _row_tile(s, 512)
    assert pad in (0, tm)
    pb = pad // tm

    def body(h_ref, g_ref, sh_ref, sc_ref, w_ref, u_ref, l_ref, r_ref):
        _, _, _, u = _norm_mod(h_ref[...], g_ref[...], sh_ref[...], sc_ref[...])
        ub = u.astype(BF16)
        u_ref[...] = ub
        for j in range(nj):
            o_ref = l_ref if j < half else r_ref
            jj = j % half
            o_ref[:, jj * wn:(jj + 1) * wn] = jnp.dot(ub, w_ref[j], preferred_element_type=F32).astype(BF16)
        if pb:
            @pl.when(pl.program_id(0) == 0)
            def _():
                l_ref[...] = jnp.zeros_like(l_ref)
                r_ref[...] = jnp.zeros_like(r_ref)

    e = half * wn
    src = lambda i: (jnp.maximum(i - pb, 0), 0)
    return pl.pallas_call(
        body, name=name, grid=(s // tm + pb,),
        in_specs=[pl.BlockSpec((tm, d), src), _vec(d), _vec(d), _vec(d),
                  pl.BlockSpec((nj, d, wn), lambda i: (0, 0, 0))],
        out_specs=[pl.BlockSpec((tm, d), src), pl.BlockSpec((tm, e), lambda i: (i, 0)),
                   pl.BlockSpec((tm, e), lambda i: (i, 0))],
        out_shape=[_sds((s, d), BF16), _sds((s + pad, e), BF16), _sds((s + pad, e), BF16)],
        compiler_params=_cp("arbitrary"),
    )(h, g, shift, scale, w)


def _pool_fwd(val, wg):
    s, e = val.shape
    ng = len(POOL_WINDOWS)
    gw = e // ng
    tm = _row_tile(s, 256)
    hb = tm // POOL_HALO

    def body(v_ref, halo_ref, w_ref, p_ref, m_ref):
        i = pl.program_id(0)
        t = i * tm + lax.broadcasted_iota(jnp.int32, (tm, 1), 0)
        for g, wdw in enumerate(POOL_WINDOWS):
            cols = slice(g * gw, (g + 1) * gw)
            v = v_ref[:, cols].astype(F32)
            halo = jnp.where(i > 0, halo_ref[:, cols].astype(F32), 0.0)
            acc = jnp.concatenate([halo, v], axis=0)
            sh = 1
            while sh < wdw:
                acc = acc + pltpu.roll(acc, sh, 0)
                sh *= 2
            cnt = jnp.minimum(t + 1, wdw).astype(F32)
            pb = (acc[POOL_HALO:, :] / cnt - v).astype(BF16)
            p_ref[:, cols] = pb
            m_ref[:, cols] = jnp.dot(pb, w_ref[:, g].reshape(gw, gw), preferred_element_type=F32).astype(BF16)

    return pl.pallas_call(
        body, name="pool_fwd", grid=(s // tm,),
        in_specs=[pl.BlockSpec((tm, e), lambda i: (i, 0)),
                  pl.BlockSpec((POOL_HALO, e), lambda i: (jnp.maximum(i * hb - 1, 0), 0)),
                  pl.BlockSpec((N_DEV, ng, gw // N_DEV, gw), lambda i: (0, 0, 0, 0))],
        out_specs=[pl.BlockSpec((tm, e), lambda i: (i, 0)), pl.BlockSpec((tm, e), lambda i: (i, 0))],
        out_shape=[_sds((s, e), BF16), _sds((s, e), BF16)],
        compiler_params=_cp("arbitrary"),
    )(val, val, wg)


def _gate_out_fwd(m, z, ascale, w, h, gate, name):
    s, e = m.shape
    d = h.shape[1]
    tm = _row_tile(s, 512)

    def body(m_ref, z_ref, a_ref, w_ref, h_ref, g_ref, gd_ref, y_ref, ho_ref):
        z = z_ref[...].astype(F32)
        gb = ((m_ref[...].astype(F32) * a_ref[...]) * (z * _sigmoid(z))).astype(BF16)
        gd_ref[...] = gb
        y = jnp.dot(gb, w_ref[...].reshape(e, d), preferred_element_type=F32)
        y_ref[...] = y.astype(BF16)
        ho_ref[...] = h_ref[...] + g_ref[...] * y

    return pl.pallas_call(
        body, name=name, grid=(s // tm,),
        in_specs=[pl.BlockSpec((tm, e), lambda i: (i, 0)), pl.BlockSpec((tm, e), lambda i: (i, 0)), _vec(e),
                  pl.BlockSpec((N_DEV, e // N_DEV, d), lambda i: (0, 0, 0)),
                  pl.BlockSpec((tm, d), lambda i: (i, 0)), _vec(d)],
        out_specs=[pl.BlockSpec((tm, e), lambda i: (i, 0)), pl.BlockSpec((tm, d), lambda i: (i, 0)),
                   pl.BlockSpec((tm, d), lambda i: (i, 0))],
        out_shape=[_sds((s, e), BF16), _sds((s, d), BF16), _sds((s, d), F32)],
        compiler_params=_cp("arbitrary"),
    )(m, z, ascale, w, h, gate)


def _rel_onehot(shape, r_axis):
    r = lax.broadcasted_iota(jnp.int32, shape, r_axis)
    j = lax.broadcasted_iota(jnp.int32, shape, 1 - r_axis)
    dist = LEFT - (j - TQ)
    return (jnp.clip(dist, -REL_CLIP, REL_CLIP) + REL_CLIP == r).astype(F32)


def _skew(x, sign):
    row = lax.broadcasted_iota(jnp.int32, x.shape, 0)
    for b in range(TQ.bit_length() - 1):
        amt = (1 << b) if sign > 0 else FRAME_PAD - (1 << b)
        x = jnp.where(((row >> b) & 1) == 1, pltpu.roll(x, amt, 1), x)
    return x


def _bias_frames(rel):
    nh = rel.shape[0]

    def body(r_ref, f_ref, ft_ref):
        fext = jnp.dot(r_ref[...], _rel_onehot((REL_PAD, FRAME_PAD), 0), preferred_element_type=F32,
                       precision=lax.Precision.HIGHEST)
        x = _skew(jnp.broadcast_to(fext, (TQ, FRAME_PAD)), 1)[:, TQ:]
        qc = lax.broadcasted_iota(jnp.int32, (TQ, BAND), 0) // CHUNK
        m = lax.broadcasted_iota(jnp.int32, (TQ, BAND), 1)
        mc = m // CHUNK
        x = jnp.where((mc >= qc) & (mc <= qc + LEFT_CHUNKS), x, NEG)
        for f in range(NFRAME):
            xf = jnp.where(m >= LEFT - f * TQ, x, NEG)
            f_ref[f] = xf
            ft_ref[f] = xf.T

    return pl.pallas_call(
        body, name="bias_frames", grid=(nh,),
        in_specs=[pl.BlockSpec((None, 1, REL_PAD), lambda h: (h, 0, 0))],
        out_specs=[pl.BlockSpec((None, NFRAME, TQ, BAND), lambda h: (h, 0, 0, 0)),
                   pl.BlockSpec((None, NFRAME, BAND, TQ), lambda h: (h, 0, 0, 0))],
        out_shape=[_sds((nh, NFRAME, TQ, BAND), F32), _sds((nh, NFRAME, BAND, TQ), F32)],
        compiler_params=_cp("arbitrary"),
    )(rel)


def _bias_bwd(dft):
    nh = dft.shape[0]

    def body(d_ref, o_ref):
        d = d_ref[0]
        for f in range(1, NFRAME):
            d = d + d_ref[f]
        x = jnp.concatenate([jnp.zeros((TQ, TQ), F32), d.T], axis=1)
        col = jnp.sum(_skew(x, -1), axis=0, keepdims=True)
        o_ref[...] = jnp.dot(col, _rel_onehot((FRAME_PAD, REL_PAD), 1), preferred_element_type=F32,
                             precision=lax.Precision.HIGHEST)

    return pl.pallas_call(
        body, name="bias_bwd", grid=(nh,),
        in_specs=[pl.BlockSpec((None, NFRAME, BAND, TQ), lambda h: (h, 0, 0, 0))],
        out_specs=pl.BlockSpec((None, 1, REL_PAD), lambda h: (h, 0, 0)),
        out_shape=_sds((nh, 1, REL_PAD), F32),
        compiler_params=_cp("arbitrary"),
    )(dft)


def _attn_tiles(s):
    nt = s // TQ
    ni = min(ATTN_GROUP, nt)
    return nt // ni, ni


def _attn_fwd(q, k, v, frame):
    s, e = q.shape
    dh = e // N_HEADS
    ng, ni = _attn_tiles(s)
    sm = dh ** -0.5
    nb = min(ATTN_BATCH, ni)

    def body(q_ref, k_ref, v_ref, b_ref, o_ref):
        g0 = pl.program_id(1) * ni
        for tb in range(0, ni, nb):
            ts = range(tb, tb + nb)
            keys = [pl.ds(pl.multiple_of((g0 + t) * TQ, TQ), BAND) for t in ts]
            rows = [pl.ds(t * TQ, TQ) for t in ts]
            scs = [lax.dot_general(q_ref[r, :], k_ref[kk, :], NT, preferred_element_type=F32)
                   for r, kk in zip(rows, keys)]
            ps, ls = [], []
            for t, sc in zip(ts, scs):
                sc = sc * sm + b_ref[jnp.minimum(g0 + t, NFRAME - 1)]
                p = jnp.exp(sc - jnp.max(sc, axis=-1, keepdims=True))
                ls.append(jnp.sum(p, axis=-1, keepdims=True))
                ps.append(p.astype(BF16))
            for r, p, l, kk in zip(rows, ps, ls, keys):
                o = jnp.dot(p, v_ref[kk, :], preferred_element_type=F32)
                o_ref[r, :] = (o / l).astype(BF16)

    return pl.pallas_call(
        body, name="attn_fwd", grid=(N_HEADS, ng),
        in_specs=[pl.BlockSpec((ni * TQ, dh), lambda h, i: (i, h)),
                  pl.BlockSpec((s + LEFT, dh), lambda h, i: (0, h)), pl.BlockSpec((s + LEFT, dh), lambda h, i: (0, h)),
                  pl.BlockSpec((None, NFRAME, TQ, BAND), lambda h, i: (h, 0, 0, 0))],
        out_specs=pl.BlockSpec((ni * TQ, dh), lambda h, i: (i, h)),
        out_shape=_sds((s, e), BF16),
        compiler_params=_cp("arbitrary", "arbitrary"),
    )(q, k, v, frame)


def _final_loss(h, g, target):
    s, d = h.shape
    tm = _row_tile(s, 512)

    def body(h_ref, g_ref, t_ref, dh_ref, dg_ref, l_ref):
        @pl.when(pl.program_id(0) == 0)
        def _():
            dg_ref[...] = jnp.zeros_like(dg_ref)
            l_ref[...] = jnp.zeros_like(l_ref)

        hf = h_ref[...]
        gg = g_ref[...]
        rs = lax.rsqrt(jnp.mean(hf * hf, axis=-1, keepdims=True) + EPS)
        xhat = hf * rs
        diff = xhat * gg - t_ref[...]
        l_ref[...] += 0.5 * jnp.sum(jnp.mean(diff * diff, axis=-1, keepdims=True), axis=0, keepdims=True)
        dout = diff * (1.0 / d)
        dg_ref[...] += jnp.sum(dout * xhat, axis=0, keepdims=True)
        dxh = dout * gg
        dh_ref[...] = rs * (dxh - xhat * jnp.mean(dxh * xhat, axis=-1, keepdims=True))

    return pl.pallas_call(
        body, name="final_loss", grid=(s // tm,),
        in_specs=[pl.BlockSpec((tm, d), lambda i: (i, 0)), _vec(d), pl.BlockSpec((tm, d), lambda i: (i, 0))],
        out_specs=[pl.BlockSpec((tm, d), lambda i: (i, 0)), _vec(d), _vec(LANES)],
        out_shape=[_sds((s, d), F32), _sds((1, d), F32), _sds((1, LANES), F32)],
        compiler_params=_cp("arbitrary"),
    )(h, g, target)


def _out_bwd(dh, y, gate, w, m, z, ascale, name):
    s, d = dh.shape
    e = m.shape[1]
    tm = _row_tile(s, 512)

    def body(dh_ref, y_ref, g_ref, w_ref, m_ref, z_ref, a_ref, dy_ref, dm_ref, dz_ref, dg_ref, da_ref):
        @pl.when(pl.program_id(0) == 0)
        def _():
            dg_ref[...] = jnp.zeros_like(dg_ref)
            da_ref[...] = jnp.zeros_like(da_ref)

        dhf = dh_ref[...]
        dg_ref[...] += jnp.sum(dhf * y_ref[...].astype(F32), axis=0, keepdims=True)
        dyb = (g_ref[...] * dhf).astype(BF16)
        dy_ref[...] = dyb
        dgated = lax.dot_general(dyb, w_ref[...].reshape(e, d), NT, preferred_element_type=F32)
        z = z_ref[...].astype(F32)
        sig = _sigmoid(z)
        mf = m_ref[...].astype(F32)
        a = a_ref[...]
        dms = dgated * (z * sig)
        da_ref[...] += jnp.sum(dms * mf, axis=0, keepdims=True)
        dm_ref[...] = (dms * a).astype(BF16)
        dz_ref[...] = (dgated * (mf * a) * (sig * (1.0 + z * (1.0 - sig)))).astype(BF16)

    return pl.pallas_call(
        body, name=name, grid=(s // tm,),
        in_specs=[pl.BlockSpec((tm, d), lambda i: (i, 0)), pl.BlockSpec((tm, d), lambda i: (i, 0)), _vec(d),
                  pl.BlockSpec((N_DEV, e // N_DEV, d), lambda i: (0, 0, 0)),
                  pl.BlockSpec((tm, e), lambda i: (i, 0)), pl.BlockSpec((tm, e), lambda i: (i, 0)), _vec(e)],
        out_specs=[pl.BlockSpec((tm, d), lambda i: (i, 0)), pl.BlockSpec((tm, e), lambda i: (i, 0)),
                   pl.BlockSpec((tm, e), lambda i: (i, 0)), _vec(d), _vec(e)],
        out_shape=[_sds((s, d), BF16), _sds((s, e), BF16), _sds((s, e), BF16), _sds((1, d), F32), _sds((1, e), F32)],
        compiler_params=_cp("arbitrary"),
    )(dh, y, gate, w, m, z, ascale)


def _pool_bwd(dm, wg):
    s, e = dm.shape
    ng = len(POOL_WINDOWS)
    gw = e // ng
    tm = _row_tile(s, 256)
    hb = tm // POOL_HALO
    nsteps = s // tm

    def body(d_ref, halo_ref, w_ref, o_ref):
        i = pl.program_id(0)
        t = i * tm + lax.broadcasted_iota(jnp.int32, (tm + POOL_HALO, 1), 0)
        for g, wdw in enumerate(POOL_WINDOWS):
            cols = slice(g * gw, (g + 1) * gw)
            dmx = jnp.concatenate([d_ref[:, cols], halo_ref[:, cols]], axis=0)
            dp = lax.dot_general(dmx, w_ref[:, g].reshape(gw, gw), NT, preferred_element_type=F32)
            dp = jnp.where(t < s, dp, 0.0)
            acc = dp / jnp.minimum(t + 1, wdw).astype(F32)
            sh = 1
            while sh < wdw:
                acc = acc + pltpu.roll(acc, tm + POOL_HALO - sh, 0)
                sh *= 2
            o_ref[:, cols] = (acc[:tm, :] - dp[:tm, :]).astype(BF16)

    return pl.pallas_call(
        body, name="pool_bwd", grid=(nsteps,),
        in_specs=[pl.BlockSpec((tm, e), lambda i: (i, 0)),
                  pl.BlockSpec((POOL_HALO, e), lambda i: (jnp.minimum((i + 1) * hb, s // POOL_HALO - 1), 0)),
                  pl.BlockSpec((N_DEV, ng, gw // N_DEV, gw), lambda i: (0, 0, 0, 0))],
        out_specs=pl.BlockSpec((tm, e), lambda i: (i, 0)),
        out_shape=_sds((s, e), BF16),
        compiler_params=_cp("arbitrary"),
    )(dm, dm, wg)


def _attn_bwd(q, k, v, do, frame_t):
    s, e = q.shape
    dh = e // N_HEADS
    ng, ni = _attn_tiles(s)
    sm = dh ** -0.5
    nleft = LEFT // TQ

    def body(q_ref, k_ref, v_ref, do_ref, b_ref, dq_ref, dk_ref, dv_ref, db_ref):
        i = pl.program_id(1)
        g0 = i * ni

        @pl.when(i == 0)
        def _():
            dk_ref[...] = jnp.zeros_like(dk_ref)
            dv_ref[...] = jnp.zeros_like(dv_ref)
            db_ref[...] = jnp.zeros_like(db_ref)

        keys = [pl.ds(pl.multiple_of((g0 + t) * TQ, TQ), BAND) for t in range(ni)]
        rows = [pl.ds(t * TQ, TQ) for t in range(ni)]
        sts = [lax.dot_general(k_ref[keys[t], :], q_ref[rows[t], :], NT, preferred_element_type=F32)
               for t in range(ni)]
        dpts = [lax.dot_general(v_ref[keys[t], :], do_ref[rows[t], :], NT, preferred_element_type=F32)
                for t in range(ni)]
        pbs, dsbs, dsts = [], [], []
        for t in range(ni):
            st = sts[t] * sm + b_ref[jnp.minimum(g0 + t, NFRAME - 1)]
            p = jnp.exp(st - jnp.max(st, axis=0, keepdims=True))
            p = p * (1.0 / jnp.sum(p, axis=0, keepdims=True))
            dst = p * (dpts[t] - jnp.sum(dpts[t] * p, axis=0, keepdims=True))
            pbs.append(p.astype(BF16))
            dsbs.append(dst.astype(BF16))
            dsts.append(dst)
        for t in range(min(ni, nleft)):
            db_ref[jnp.minimum(g0 + t, NFRAME - 1)] += dsts[t]
        if ni > nleft:
            rest = dsts[nleft]
            for t in range(nleft + 1, ni):
                rest = rest + dsts[t]
            db_ref[NFRAME - 1] += rest
        for t in range(ni):
            dq = lax.dot_general(dsbs[t], k_ref[keys[t], :], TN, preferred_element_type=F32)
            dq_ref[rows[t], :] = (dq * sm).astype(BF16)
        for r in range(ni + nleft):
            ts = [t for t in range(ni) if 0 <= r - t <= nleft]
            blk = lambda xs: jnp.concatenate([xs[t][(r - t) * TQ:(r - t + 1) * TQ, :] for t in ts], axis=1)
            qrows = slice(ts[0] * TQ, (ts[-1] + 1) * TQ)
            krows = pl.ds(pl.multiple_of((g0 + r) * TQ, TQ), TQ)
            dk_ref[krows, :] += jnp.dot(blk(dsbs), q_ref[qrows, :], preferred_element_type=F32) * sm
            dv_ref[krows, :] += jnp.dot(blk(pbs), do_ref[qrows, :], preferred_element_type=F32)

    kv_spec = pl.BlockSpec((s + LEFT, dh), lambda h, i: (0, h))
    fr_spec = pl.BlockSpec((None, NFRAME, BAND, TQ), lambda h, i: (h, 0, 0, 0))
    return pl.pallas_call(
        body, name="attn_bwd", grid=(N_HEADS, ng),
        in_specs=[pl.BlockSpec((ni * TQ, dh), lambda h, i: (i, h)), kv_spec, kv_spec,
                  pl.BlockSpec((ni * TQ, dh), lambda h, i: (i, h)), fr_spec],
        out_specs=[pl.BlockSpec((ni * TQ, dh), lambda h, i: (i, h)), kv_spec, kv_spec, fr_spec],
        out_shape=[_sds((s, e), BF16), _sds((s + LEFT, e), F32), _sds((s + LEFT, e), F32),
                   _sds((N_HEADS, NFRAME, BAND, TQ), F32)],
        compiler_params=_cp("arbitrary", "arbitrary"),
    )(q, k, v, do, frame_t)


def _add_cast(a, b, skip):
    s, e = a.shape[0] - skip, a.shape[1]
    tm = _row_tile(s, 512)
    assert skip % tm == 0
    off = skip // tm

    def body(a_ref, b_ref, o_ref):
        o_ref[...] = (a_ref[...] + b_ref[...]).astype(BF16)

    return pl.pallas_call(
        body, name="add_cast", grid=(s // tm,),
        in_specs=[pl.BlockSpec((tm, e), lambda i: (i + off, 0))] * 2,
        out_specs=pl.BlockSpec((tm, e), lambda i: (i, 0)), out_shape=_sds((s, e), BF16),
        compiler_params=_cp("arbitrary"),
    )(a, b)


def _in_bwd(dl, dr, w, h, g, scale, dres, name):
    s, d = h.shape
    nj, _, wn = w.shape
    half = nj // 2
    e = half * wn
    tm = _row_tile(s, 512)

    def body(dl_ref, dr_ref, w_ref, h_ref, g_ref, sc_ref, res_ref, dh_ref, st_ref):
        @pl.when(pl.program_id(0) == 0)
        def _():
            st_ref[...] = jnp.zeros_like(st_ref)

        du = jnp.zeros((tm, d), F32)
        for j in range(nj):
            src = dl_ref if j < half else dr_ref
            jj = j % half
            du = du + lax.dot_general(src[:, jj * wn:(jj + 1) * wn], w_ref[j], NT, preferred_element_type=F32)
        gg = g_ref[...]
        rs, xhat, n, _ = _norm_mod(h_ref[...], gg, 0.0, 0.0)
        dn = du * (1.0 + sc_ref[...])
        st_ref[0:1, :] += jnp.sum(du, axis=0, keepdims=True)
        st_ref[1:2, :] += jnp.sum(du * n, axis=0, keepdims=True)
        st_ref[2:3, :] += jnp.sum(dn * xhat, axis=0, keepdims=True)
        dxh = dn * gg
        dh_ref[...] = rs * (dxh - xhat * jnp.mean(dxh * xhat, axis=-1, keepdims=True)) + res_ref[...]

    return pl.pallas_call(
        body, name=name, grid=(s // tm,),
        in_specs=[pl.BlockSpec((tm, e), lambda i: (i, 0)), pl.BlockSpec((tm, e), lambda i: (i, 0)),
                  pl.BlockSpec((nj, d, wn), lambda i: (0, 0, 0)),
                  pl.BlockSpec((tm, d), lambda i: (i, 0)), _vec(d), _vec(d), pl.BlockSpec((tm, d), lambda i: (i, 0))],
        out_specs=[pl.BlockSpec((tm, d), lambda i: (i, 0)), pl.BlockSpec((8, d), lambda i: (0, 0))],
        out_shape=[_sds((s, d), F32), _sds((8, d), F32)],
        compiler_params=_cp("arbitrary"),
    )(dl, dr, w, h, g, scale, dres)


def _tn_matmul(x, ys, xw, yw, ymap, nb, out_shape, stage_shape, out_at, name):
    s, xfull = x.shape
    ts = _row_tile(s, 2048)
    ys = list(ys)
    half = nb // len(ys)
    nk = s // ts

    def body(*refs):
        x_ref, y_refs = refs[0], refs[1:1 + len(ys)]
        o_hbm, xt_ref, acc_ref, stage_ref, sem = refs[1 + len(ys):]
        k, b = pl.program_id(0), pl.program_id(1)

        @pl.when(b == 0)
        def _():
            xt_ref[...] = x_ref[...].T

        for n, y_ref in enumerate(y_refs):
            @pl.when((b >= n * half) & (b < (n + 1) * half))
            def _():
                xt = xt_ref[...] if xw == xfull else xt_ref[pl.ds(pl.multiple_of(b * xw, xw), xw), :]
                part = jnp.dot(xt, y_ref[...], preferred_element_type=F32)

                @pl.when(k == 0)
                def _():
                    acc_ref[b] = part

                @pl.when(k > 0)
                def _():
                    acc_ref[b] += part

        @pl.when(k == nk - 1)
        def _():
            stage_ref[...] = acc_ref[b].astype(BF16).reshape(stage_shape)
            cp = pltpu.make_async_copy(stage_ref, out_at(o_hbm, b), sem)
            cp.start()
            cp.wait()

    in_specs = [pl.BlockSpec((ts, xfull), lambda k, b: (k, 0))]
    for n in range(len(ys)):
        in_specs.append(pl.BlockSpec(
            (ts, yw), lambda k, b, n=n: (k, ymap(jnp.clip(b - n * half, 0, half - 1)))))
    return pl.pallas_call(
        body, name=name, grid=(nk, nb), in_specs=in_specs,
        out_specs=HBM_SPEC, out_shape=_sds(out_shape, BF16),
        scratch_shapes=[pltpu.VMEM((xfull, ts), BF16), pltpu.VMEM((nb, xw, yw), F32), pltpu.VMEM(stage_shape, BF16),
                        pltpu.SemaphoreType.DMA(())],
        compiler_params=_cp("arbitrary", "arbitrary"),
    )(x, *ys)


def _grad_w_in(u, dl, dr, name):
    d = u.shape[1]
    wn = 2 * dl.shape[1] // N_DEV
    return _tn_matmul(u, (dl, dr), d, wn, lambda b: b, N_DEV, (N_DEV, d, wn), (d, wn), lambda o, b: o.at[b], name)


def _grad_w_out(gated, dy, name):
    e, d = gated.shape[1], dy.shape[1]
    return _tn_matmul(gated, (dy,), e // N_DEV, d, lambda b: 0, N_DEV, (N_DEV, e // N_DEV, d), (e // N_DEV, d),
                      lambda o, b: o.at[b], name)


def _grad_w_group(pooled, dm, name):
    ng = len(POOL_WINDOWS)
    gw = pooled.shape[1] // ng
    return _tn_matmul(pooled, (dm,), gw, gw, lambda b: b, ng, (N_DEV, ng, gw // N_DEV, gw),
                      (N_DEV, gw // N_DEV, gw), lambda o, b: o.at[:, b], name)


def _adamw(staged, w, m, v, name):
    shape = w.shape
    nl = len(staged)
    n = staged[0].shape[0]
    cdim = shape[-1]
    total = 1
    for a in shape[:-1]:
        total *= a
    rows = total // nl
    sts = [st.reshape(n, rows, cdim) for st in staged]
    tr = rows if rows * cdim <= 128 * 1024 else max(8, (128 * 1024 // cdim) // 8 * 8)
    while rows % tr:
        tr -= 8
    nblk = rows // tr

    def body(*refs):
        s_refs = refs[:nl]
        w_ref, m_ref, v_ref, g_ref, d_ref, mo_ref, vo_ref = refs[nl:]
        for ll in range(nl):
            @pl.when(pl.program_id(0) == ll)
            def _():
                g = s_refs[ll][0].astype(F32)
                for j in range(1, n):
                    g = g + s_refs[ll][j].astype(F32)
                mn = ADAM_B1 * m_ref[...] + (1.0 - ADAM_B1) * g
                vn = ADAM_B2 * v_ref[...] + (1.0 - ADAM_B2) * (g * g)
                m_hat = mn / (1.0 - ADAM_B1 ** ADAM_STEP)
                v_hat = vn / (1.0 - ADAM_B2 ** ADAM_STEP)
                g_ref[...] = g
                d_ref[...] = -ADAM_LR * (m_hat / (jnp.sqrt(v_hat) + ADAM_EPS) + ADAM_WD * w_ref[...])
                mo_ref[...] = mn
                vo_ref[...] = vn

    blk = pl.BlockSpec((None, tr, cdim), lambda l, i: (l, i, 0))
    st_specs = [pl.BlockSpec((n, tr, cdim), lambda l, i, ll=ll: (0, jnp.clip(i + (l - ll) * nblk, 0, nblk - 1), 0))
                for ll in range(nl)]
    outs = pl.pallas_call(
        body, name=name, grid=(nl, nblk),
        in_specs=st_specs + [blk, blk, blk],
        out_specs=[blk] * 4, out_shape=[_sds((nl, rows, cdim), F32)] * 4,
        compiler_params=_cp("arbitrary", "arbitrary"),
    )(*sts, w.reshape(nl, rows, cdim), m.reshape(nl, rows, cdim), v.reshape(nl, rows, cdim))
    return [o.reshape(shape) for o in outs]


def _pack(parts, total):
    flat = jnp.concatenate([p.reshape(-1) for p in parts])
    return jnp.pad(flat, (0, total - flat.shape[0])).reshape(1, total)


def kernel(x, c, ada_w, ada_b, norm_g, a_w_in, a_w_group, a_scale, a_w_out, kv_norm_g, kv_ada_w, kv_ada_b, w_kv, b_w_in, b_rel_bias, b_w_out, final_g, loss_target, m_ada_w, m_ada_b, m_norm_g, m_a_w_in, m_a_w_group, m_a_scale, m_a_w_out, m_kv_norm_g, m_kv_ada_w, m_kv_ada_b, m_w_kv, m_b_w_in, m_b_rel_bias, m_b_w_out, m_final_g, v_ada_w, v_ada_b, v_norm_g, v_a_w_in, v_a_w_group, v_a_scale, v_a_w_out, v_kv_norm_g, v_kv_ada_w, v_kv_ada_b, v_w_kv, v_b_w_in, v_b_rel_bias, v_b_w_out, v_final_g):
    s, d = x.shape[1], x.shape[2]
    depth, _, wa = ada_w.shape
    wk = kv_ada_w.shape[1]
    n_a, n_b = a_w_in.shape[0], b_w_in.shape[0]
    e = a_w_out.shape[1] * N_DEV
    nrel = b_rel_bias.shape[-1]
    me = 4 * lax.axis_index("x") + 2 * lax.axis_index("y") + lax.axis_index("c")
    h0 = x[0]
    target = loss_target[0]

    bf = lambda a: a.astype(BF16)
    tie = lambda val, tok: lax.optimization_barrier((val, tok))[0]
    first_handle, token = _push_start([bf(a_w_in[0])], [True], c, "gather_start_first")
    c = tie(c, token)

    c_all = _exchange([c.reshape(1, 1, d)], [True], "gather_c")[0].reshape(N_DEV, d)
    ada_b_loc = lax.dynamic_slice(ada_b, (0, me * wa), (depth, wa)).reshape(depth, 1, wa)
    kv_ada_b_loc = lax.dynamic_slice(kv_ada_b, (me * wk,), (wk,)).reshape(1, wk)
    mods_cols = _mods_fwd(c_all, ada_w, ada_b_loc, kv_ada_w, kv_ada_b_loc)
    nm = depth * wa + wk
    mods_me, a_scale_all = _exchange([mods_cols.reshape(N_DEV, 1, nm), a_scale], [False, True], "exchange_mods")
    groups = [[bf(a_w_group[0]), bf(a_w_out[0])]]
    groups += [[bf(a_w_in[l]), bf(a_w_group[l]), bf(a_w_out[l])] for l in range(1, n_a)]
    groups.append([bf(w_kv), bf(b_w_in[0]), bf(b_w_out[0])])
    groups += [[bf(b_w_in[l]), bf(b_w_out[l])] for l in range(1, n_b)]
    handles = []
    token = mods_me
    for gi, grp in enumerate(groups):
        hd, token = _push_start(grp, [True] * len(grp), token, f"gather_start_{gi}")
        handles.append(hd)
    mods_me = mods_me.reshape(N_DEV, nm)
    mods = mods_me[:, :depth * wa].reshape(N_DEV, depth, wa).transpose(1, 0, 2).reshape(depth, 3, 1, d)
    kv_mods = mods_me[:, depth * wa:].reshape(2, 1, d)
    a_scale_full = a_scale_all.transpose(1, 0, 2).reshape(n_a, 1, e)
    ones_e = jnp.ones((1, e), F32)

    saved = []
    h = h0
    k = v = hk = wkv = None
    after = token
    for layer in range(depth):
        shift, scale, gate = mods[layer, 0], mods[layer, 1], mods[layer, 2]
        g = norm_g[layer].reshape(1, d)
        if layer == 0:
            w_in, = _push_wait(first_handle, after, "gather_wait_first")
            u, left, z = _inproj_fwd(h, g, shift, scale, w_in, "inproj_fwd")
            w_group, w_out = _push_wait(handles[0], u, "gather_wait_0")
        elif layer < n_a:
            w_in, w_group, w_out = _push_wait(handles[layer], after, f"gather_wait_{layer}")
            u, left, z = _inproj_fwd(h, g, shift, scale, w_in, "inproj_fwd")
        if layer < n_a:
            pooled, mixed = _pool_fwd(left, w_group)
            gated, y, hn = _gate_out_fwd(mixed, z, a_scale_full[layer], w_out, h, gate, "gate_out_fwd")
            saved.append(dict(h=h, u=u, left=left, z=z, pooled=pooled, m=mixed, gated=gated, y=y,
                              w_in=w_in, w_group=w_group, w_out=w_out))
        else:
            bi = layer - n_a
            got = _push_wait(handles[layer], after, f"gather_wait_{layer}")
            if bi == 0:
                wkv, w_in, w_out = got
                hk, k, v = _inproj_fwd(h, kv_norm_g.reshape(1, d), kv_mods[0], kv_mods[1], wkv, "inproj_kv", pad=LEFT)
            else:
                w_in, w_out = got
            u, left, z = _inproj_fwd(h, g, shift, scale, w_in, "inproj_fwd")
            rel = jnp.pad(b_rel_bias[bi], ((0, 0), (0, REL_PAD - nrel))).reshape(N_HEADS, 1, REL_PAD)
            frame, frame_t = _bias_frames(rel)
            att = _attn_fwd(left, k, v, frame)
            gated, y, hn = _gate_out_fwd(att, z, ones_e, w_out, h, gate, "gate_out_fwd")
            saved.append(dict(h=h, u=u, left=left, z=z, m=att, gated=gated, y=y, frame_t=frame_t,
                              w_in=w_in, w_out=w_out))
        h = hn
        after = h
    dh, d_final_g, loss_part = _final_loss(h, final_g.reshape(1, d), target)

    d_mods = [None] * depth
    d_norm_g = [None] * depth
    d_rel = [None] * n_b
    d_ascale = [None] * n_a
    dk_parts, dv_parts = [], []
    tie = lambda val, tok: lax.optimization_barrier((val, tok))[0]
    grad_handles = [None] * (depth + 1)
    for layer in range(depth - 1, n_a - 1, -1):
        bi = layer - n_a
        sv = saved[layer]
        scale, gate = mods[layer, 1], mods[layer, 2]
        dy, datt, dz, d_gate, _ = _out_bwd(dh, sv["y"], gate, sv["w_out"], sv["m"], sv["z"], ones_e, "out_bwd")
        dq, dk, dv, dframe_t = _attn_bwd(sv["left"], k, v, datt, sv["frame_t"])
        dk_parts.append(dk)
        dv_parts.append(dv)
        d_rel[bi] = _bias_bwd(dframe_t).reshape(N_HEADS, REL_PAD)[:, :nrel]
        g_out = _grad_w_out(sv["gated"], dy, "grad_w_out")
        g_in = _grad_w_in(sv["u"], dq, dz, "grad_w_in")
        grad_handles[layer], tok = _push_start([g_in, g_out], [False, False], d_gate, f"grads_start_{layer}")
        dh, st = _in_bwd(tie(dq, tok), dz, sv["w_in"], sv["h"], norm_g[layer].reshape(1, d), scale, dh, "in_bwd")
        d_mods[layer] = jnp.concatenate([st[0], st[1], d_gate[0]])
        d_norm_g[layer] = st[2]
    dkb = _add_cast(dk_parts[0], dk_parts[1], LEFT)
    dvb = _add_cast(dv_parts[0], dv_parts[1], LEFT)
    h_kv = saved[n_a]["h"]
    g_w_kv = _grad_w_in(hk, dkb, dvb, "grad_w_in")
    grad_handles[depth], tok = _push_start([g_w_kv], [False], dkb, "grads_start_kv")
    dh, st = _in_bwd(tie(dkb, tok), dvb, wkv, h_kv, kv_norm_g.reshape(1, d), kv_mods[1], dh, "in_bwd")
    d_kv_mods = jnp.concatenate([st[0], st[1]])
    d_kv_norm_g = st[2]
    for layer in range(n_a - 1, -1, -1):
        sv = saved[layer]
        scale, gate = mods[layer, 1], mods[layer, 2]
        dy, dm, dz, d_gate, da = _out_bwd(dh, sv["y"], gate, sv["w_out"], sv["m"], sv["z"], a_scale_full[layer],
                                          "out_bwd")
        d_ascale[layer] = da.reshape(N_DEV, e // N_DEV)
        g_group = _grad_w_group(sv["pooled"], dm, "grad_w_group")
        g_out = _grad_w_out(sv["gated"], dy, "grad_w_out")
        if layer == 0:
            last_handle, tok = _push_start([g_group, g_out], [False] * 2, d_gate, "grads_start_last")
            dm = tie(dm, tok)
        dval = _pool_bwd(dm, sv["w_group"])
        g_in = _grad_w_in(sv["u"], dval, dz, "grad_w_in")
        late = [g_in] if layer == 0 else [g_in, g_group, g_out]
        grad_handles[layer], tok = _push_start(late, [False] * len(late), d_gate, f"grads_start_{layer}")
        dh, st = _in_bwd(tie(dval, tok), dz, sv["w_in"], sv["h"], norm_g[layer].reshape(1, d), scale, dh, "in_bwd")
        d_mods[layer] = jnp.concatenate([st[0], st[1], d_gate[0]])
        d_norm_g[layer] = st[2]
    grad_x = dh.reshape(1, s, d)

    d_mods = jnp.stack(d_mods)
    dm_slots = jnp.concatenate(
        [d_mods.reshape(depth, N_DEV, wa).transpose(1, 0, 2).reshape(N_DEV, depth * wa), d_kv_mods.reshape(N_DEV, wk)],
        axis=1).reshape(N_DEV, 1, nm)
    small_parts = [d_mods, d_kv_mods, jnp.stack(d_norm_g), d_kv_norm_g, d_final_g, jnp.stack(d_rel), loss_part[0, :1]]
    n_small = sum(int(p.size) for p in small_parts)
    n_small_pad = -(-n_small // LANES) * LANES
    small = _pack(small_parts, n_small_pad).reshape(1, 1, n_small_pad)
    d_ascale_slots = jnp.stack(d_ascale, axis=1)
    s_a_scale, dm_cols, small_all = _exchange([d_ascale_slots, dm_slots, small], [False, False, True],
                                              "exchange_small")

    g_ada_w, g_kv_ada_w = _mods_bwd(c_all, dm_cols.reshape(N_DEV, nm), depth, wa, wk)
    res = {}
    res["ada_w"] = _adamw([g_ada_w[None]], ada_w, m_ada_w, v_ada_w, "adamw")
    res["kv_ada_w"] = _adamw([g_kv_ada_w[None]], kv_ada_w, m_kv_ada_w, v_kv_ada_w, "adamw")
    res["a_scale"] = _adamw([s_a_scale[:, l] for l in range(n_a)], a_scale, m_a_scale, v_a_scale, "adamw")
    after = res["ada_w"][1]
    s_b = [None] * n_b
    for layer in range(depth - 1, n_a - 1, -1):
        s_b[layer - n_a] = _push_wait(grad_handles[layer], after, f"grads_wait_{layer}")
        after = s_b[layer - n_a][0]
    res["b_w_in"] = _adamw([sb[0] for sb in s_b], b_w_in, m_b_w_in, v_b_w_in, "adamw")
    res["b_w_out"] = _adamw([sb[1] for sb in s_b], b_w_out, m_b_w_out, v_b_w_out, "adamw")
    s_w_kv, = _push_wait(grad_handles[depth], res["b_w_out"][1], "grads_wait_kv")
    res["w_kv"] = _adamw([s_w_kv], w_kv, m_w_kv, v_w_kv, "adamw")
    after = res["w_kv"][1]
    s_a = [None] * n_a
    for layer in range(n_a - 1, 0, -1):
        s_a[layer] = _push_wait(grad_handles[layer], after, f"grads_wait_{layer}")
        after = s_a[layer][0]
    s_group0, s_out0 = _push_wait(last_handle, after, "grads_wait_last")
    s_a[0] = [None, s_group0, s_out0]
    res["a_w_group"] = _adamw([sa[1] for sa in s_a], a_w_group, m_a_w_group, v_a_w_group, "adamw")
    res["a_w_out"] = _adamw([sa[2] for sa in s_a], a_w_out, m_a_w_out, v_a_w_out, "adamw")
    s_a[0][0], = _push_wait(grad_handles[0], res["a_w_out"][1], "grads_wait_0")
    res["a_w_in"] = _adamw([sa[0] for sa in s_a], a_w_in, m_a_w_in, v_a_w_in, "adamw")
    small_names = ["ada_b", "kv_ada_b", "norm_g", "kv_norm_g", "final_g", "b_rel_bias"]
    small_w = dict(ada_b=ada_b, kv_ada_b=kv_ada_b, norm_g=norm_g, kv_norm_g=kv_norm_g, final_g=final_g,
                   b_rel_bias=b_rel_bias)
    small_m = dict(ada_b=m_ada_b, kv_ada_b=m_kv_ada_b, norm_g=m_norm_g, kv_norm_g=m_kv_norm_g, final_g=m_final_g,
                   b_rel_bias=m_b_rel_bias)
    small_v = dict(ada_b=v_ada_b, kv_ada_b=v_kv_ada_b, norm_g=v_norm_g, kv_norm_g=v_kv_norm_g, final_g=v_final_g,
                   b_rel_bias=v_b_rel_bias)
    sw = _pack([small_w[n] for n in small_names], n_small_pad)
    smm = _pack([small_m[n] for n in small_names], n_small_pad)
    svv = _pack([small_v[n] for n in small_names] + [jnp.ones((n_small_pad - n_small + 1,), F32)], n_small_pad)
    small_out = _adamw([small_all.reshape(N_DEV, 1, n_small_pad)], sw, smm, svv, "adamw")
    off = 0
    for n in small_names:
        size = int(small_w[n].size)
        res[n] = [o[0, off:off + size].reshape(small_w[n].shape) for o in small_out]
        off += size
    loss = small_out[0][0, n_small - 1]

    order = ["ada_w", "ada_b", "norm_g", "a_w_in", "a_w_group", "a_scale", "a_w_out", "kv_norm_g", "kv_ada_w",
             "kv_ada_b", "w_kv", "b_w_in", "b_rel_bias", "b_w_out", "final_g"]
    outs = [loss, grad_x]
    for part in range(4):
        outs += [res[n][part] for n in order]
    return tuple(outs)
```

```python
import functools

import jax
import jax.numpy as jnp
from jax import lax
from jax.experimental import pallas as pl
from jax.experimental.pallas import tpu as pltpu

F32 = jnp.float32
BF16 = jnp.bfloat16

N_DEV = 8
CHUNK = 64
LEFT_CHUNKS = 8
N_HEADS = 16
POOL_WINDOWS = (2, 4, 8, 16)
REL_CLIP = 128
EPS = 1e-6
ADAM_LR = 0.001
ADAM_B1 = 0.9
ADAM_B2 = 0.999
ADAM_EPS = 1e-08
ADAM_WD = 0.01
ADAM_STEP = 10

LANES = 128
TQ = 2 * CHUNK
LEFT = LEFT_CHUNKS * CHUNK
BAND = LEFT + TQ
FRAME_PAD = BAND + TQ
NFRAME = LEFT // TQ + 1
POOL_HALO = 16
REL_PAD = 384
NEG = -1e30
ATTN_GROUP = 8
ATTN_BATCH = 4
VMEM_LIMIT = 56 * 1024 * 1024

NT = (((1,), (1,)), ((), ()))
TN = (((0,), (0,)), ((), ()))

HBM_SPEC = pl.BlockSpec(memory_space=pltpu.HBM)
ANY_SPEC = pl.BlockSpec(memory_space=pl.ANY)
SEM_SPEC = pl.BlockSpec(memory_space=pltpu.SEMAPHORE)


def _cp(*sem):
    return pltpu.CompilerParams(dimension_semantics=sem or None, vmem_limit_bytes=VMEM_LIMIT)


def _sds(shape, dtype):
    return jax.ShapeDtypeStruct(tuple(shape), dtype)


def _row_tile(s, want):
    return min(want, s)


def _vec(d):
    return pl.BlockSpec((1, d), lambda *_: (0, 0))


def _sigmoid(z):
    return 1.0 / (1.0 + jnp.exp(-z))


def _exchange(xs, gather, name):
    n = len(xs)
    out_shapes = []
    for x, ga in zip(xs, gather):
        out_shapes.append(_sds((N_DEV,) + (x.shape if ga else x.shape[1:]), x.dtype))

    def body(*refs):
        ins, outs = refs[:n], refs[n:2 * n]
        send_sems, recv_sems, local_sems = refs[2 * n:]
        mx, my, mc = lax.axis_index("x"), lax.axis_index("y"), lax.axis_index("c")
        me = 4 * mx + 2 * my + mc
        local = []
        for k in range(n):
            src = ins[k] if gather[k] else ins[k].at[me]
            cp = pltpu.make_async_copy(src, outs[k].at[me], local_sems.at[k])
            cp.start()
            local.append(cp)
        sends, recvs = [], []
        for r in range(1, N_DEV):
            px, py, pc = (mx + (r >> 2)) % 2, (my + ((r >> 1) & 1)) % 2, (mc + (r & 1)) % 2
            peer = 4 * px + 2 * py + pc
            for k in range(n):
                sem = k * (N_DEV - 1) + r - 1
                src = ins[k] if gather[k] else ins[k].at[peer]
                send = pltpu.make_async_remote_copy(
                    src_ref=src, dst_ref=outs[k].at[me], send_sem=send_sems.at[sem], recv_sem=recv_sems.at[sem],
                    device_id=(px, py, pc), device_id_type=pl.DeviceIdType.MESH)
                send.start()
                sends.append(send)
                recvs.append(pltpu.make_async_remote_copy(
                    src_ref=src, dst_ref=outs[k].at[peer], send_sem=send_sems.at[sem], recv_sem=recv_sems.at[sem],
                    device_id=(px, py, pc), device_id_type=pl.DeviceIdType.MESH))
        for cp in recvs:
            cp.wait_recv()
        for cp in sends:
            cp.wait_send()
        for cp in local:
            cp.wait()

    return pl.pallas_call(
        body, name=name, out_shape=out_shapes,
        in_specs=[HBM_SPEC] * n, out_specs=[HBM_SPEC] * n,
        scratch_shapes=[pltpu.SemaphoreType.DMA((n * (N_DEV - 1),)), pltpu.SemaphoreType.DMA((n * (N_DEV - 1),)),
                        pltpu.SemaphoreType.DMA((n,))],
    )(*xs)


def _peer(mx, my, mc, r):
    px, py, pc = (mx + (r >> 2)) % 2, (my + ((r >> 1) & 1)) % 2, (mc + (r & 1)) % 2
    return (px, py, pc), 4 * px + 2 * py + pc


def _push_start(xs, gather, dep, name):
    n = len(xs)
    nsem = n * (N_DEV - 1)
    me = 4 * lax.axis_index("x") + 2 * lax.axis_index("y") + lax.axis_index("c")
    lands = []
    for x, ga in zip(xs, gather):
        own = x[None] if ga else lax.dynamic_index_in_dim(x, me, 0, keepdims=True)
        empty = lax.empty((N_DEV,) + own.shape[1:], x.dtype)
        lands.append(lax.dynamic_update_slice(empty, own, (me,) + (0,) * (own.ndim - 1)))

    def body(*refs):
        ins, lands_in = refs[:n], refs[n:2 * n]
        send_sems, recv_sems = refs[2 * n + 1], refs[2 * n + 2]
        token = refs[-1]
        mx, my, mc = lax.axis_index("x"), lax.axis_index("y"), lax.axis_index("c")
        mine = 4 * mx + 2 * my + mc
        for k in range(n):
            for r in range(1, N_DEV):
                dev, peer = _peer(mx, my, mc, r)
                sem = k * (N_DEV - 1) + r - 1
                pltpu.make_async_remote_copy(
                    src_ref=ins[k] if gather[k] else ins[k].at[peer], dst_ref=lands_in[k].at[mine],
                    send_sem=send_sems.at[sem], recv_sem=recv_sems.at[sem],
                    device_id=dev, device_id_type=pl.DeviceIdType.MESH).start()
        token[...] = jnp.zeros_like(token)

    hbm = lambda a: pltpu.HBM(a.shape, a.dtype)
    outs = pl.pallas_call(
        body, name=name,
        out_shape=(pltpu.SemaphoreType.DMA((nsem,)), pltpu.SemaphoreType.DMA((nsem,)),
                   *[hbm(x) for x in xs], *[hbm(a) for a in lands], _sds((8, LANES), F32)),
        in_specs=[HBM_SPEC] * (2 * n) + [ANY_SPEC],
        out_specs=(SEM_SPEC, SEM_SPEC, *[HBM_SPEC] * (2 * n), pl.BlockSpec(memory_space=pltpu.VMEM)),
        input_output_aliases={k: 2 + k for k in range(2 * n)},
        compiler_params=pltpu.CompilerParams(has_side_effects=pltpu.SideEffectType.DATAFLOW_SIDE_EFFECTING),
    )(*[pltpu.with_memory_space_constraint(x, pltpu.HBM) for x in xs],
      *[pltpu.with_memory_space_constraint(a, pltpu.HBM) for a in lands], dep)
    return (outs[0], outs[1], outs[2:2 + n], outs[2 + n:2 + 2 * n], tuple(gather)), outs[-1]


def _push_wait(handle, after, name):
    send_sems, recv_sems, srcs, lands, gather = handle
    n = len(srcs)

    def body(*refs):
        ins, lands_in = refs[:n], refs[n:2 * n]
        send_sems, recv_sems = refs[2 * n], refs[2 * n + 1]
        mx, my, mc = lax.axis_index("x"), lax.axis_index("y"), lax.axis_index("c")
        for k in range(n):
            for r in range(1, N_DEV):
                dev, peer = _peer(mx, my, mc, r)
                sem = k * (N_DEV - 1) + r - 1
                cp = pltpu.make_async_remote_copy(
                    src_ref=ins[k] if gather[k] else ins[k].at[peer], dst_ref=lands_in[k].at[peer],
                    send_sem=send_sems.at[sem], recv_sem=recv_sems.at[sem],
                    device_id=dev, device_id_type=pl.DeviceIdType.MESH)
                cp.wait_send()
                cp.wait_recv()

    hbm = lambda a: pltpu.HBM(a.shape, a.dtype)
    outs = pl.pallas_call(
        body, name=name,
        out_shape=(*[hbm(x) for x in srcs], *[hbm(a) for a in lands]),
        in_specs=[HBM_SPEC] * (2 * n) + [SEM_SPEC, SEM_SPEC, ANY_SPEC],
        out_specs=tuple([HBM_SPEC] * (2 * n)),
        input_output_aliases={k: k for k in range(2 * n)},
        compiler_params=pltpu.CompilerParams(has_side_effects=pltpu.SideEffectType.DATAFLOW_SIDE_EFFECTING),
    )(*srcs, *lands, send_sems, recv_sems, after)
    return list(outs[n:])


def _mods_fwd(c_all, ada_w, ada_b_loc, kv_ada_w, kv_ada_b_loc):
    nl, d, wa = ada_w.shape
    wk = kv_ada_w.shape[1]

    def body(c_ref, w_ref, b_ref, kw_ref, kb_ref, o_ref):
        c = c_ref[...]
        ca = c * _sigmoid(c)
        for l in range(nl):
            o_ref[:, l * wa:(l + 1) * wa] = jnp.dot(
                ca, w_ref[l], preferred_element_type=F32, precision=lax.Precision.HIGHEST) + b_ref[l]
        o_ref[:, nl * wa:] = jnp.dot(
            ca, kw_ref[...], preferred_element_type=F32, precision=lax.Precision.HIGHEST) + kb_ref[...]

    return pl.pallas_call(body, name="mods_fwd", out_shape=_sds((N_DEV, nl * wa + wk), F32),
                          compiler_params=_cp())(c_all, ada_w, ada_b_loc, kv_ada_w, kv_ada_b_loc)


def _mods_bwd(c_all, dm, nl, wa, wk):
    d = c_all.shape[1]

    def body(c_ref, d_ref, gw_ref, gk_ref):
        c = c_ref[...]
        ca = c * _sigmoid(c)
        for l in range(nl):
            gw_ref[l] = lax.dot_general(ca, d_ref[:, l * wa:(l + 1) * wa], TN,
                                        preferred_element_type=F32, precision=lax.Precision.HIGHEST)
        gk_ref[...] = lax.dot_general(ca, d_ref[:, nl * wa:], TN,
                                      preferred_element_type=F32, precision=lax.Precision.HIGHEST)

    return pl.pallas_call(body, name="mods_bwd", out_shape=[_sds((nl, d, wa), F32), _sds((d, wk), F32)],
                          compiler_params=_cp())(c_all, dm)


def _norm_mod(hf, g, shift, scale):
    rs = lax.rsqrt(jnp.mean(hf * hf, axis=-1, keepdims=True) + EPS)
    xhat = hf * rs
    n = xhat * g
    return rs, xhat, n, n * (1.0 + scale) + shift


def _inproj_fwd(h, g, shift, scale, w, name, pad=0):
    s, d = h.shape
    nj, _, wn = w.shape
    half = nj // 2
    tm = _row_tile(s, 512)
    assert pad in (0, tm)
    pb = pad // tm

    def body(h_ref, g_ref, sh_ref, sc_ref, w_ref, u_ref, l_ref, r_ref):
        _, _, _, u = _norm_mod(h_ref[...], g_ref[...], sh_ref[...], sc_ref[...])
        ub = u.astype(BF16)
        u_ref[...] = ub
        for j in range(nj):
            o_ref = l_ref if j < half else r_ref
            jj = j % half
            o_ref[:, jj * wn:(jj + 1) * wn] = jnp.dot(ub, w_ref[j], preferred_element_type=F32).astype(BF16)
        if pb:
            @pl.when(pl.program_id(0) == 0)
            def _():
                l_ref[...] = jnp.zeros_like(l_ref)
                r_ref[...] = jnp.zeros_like(r_ref)

    e = half * wn
    src = lambda i: (jnp.maximum(i - pb, 0), 0)
    return pl.pallas_call(
        body, name=name, grid=(s // tm + pb,),
        in_specs=[pl.BlockSpec((tm, d), src), _vec(d), _vec(d), _vec(d),
                  pl.BlockSpec((nj, d, wn), lambda i: (0, 0, 0))],
        out_specs=[pl.BlockSpec((tm, d), src), pl.BlockSpec((tm, e), lambda i: (i, 0)),
                   pl.BlockSpec((tm, e), lambda i: (i, 0))],
        out_shape=[_sds((s, d), BF16), _sds((s + pad, e), BF16), _sds((s + pad, e), BF16)],
        compiler_params=_cp("arbitrary"),
    )(h, g, shift, scale, w)


def _pool_fwd(val, wg):
    s, e = val.shape
    ng = len(POOL_WINDOWS)
    gw = e // ng
    tm = _row_tile(s, 256)
    hb = tm // POOL_HALO

    def body(v_ref, halo_ref, w_ref, p_ref, m_ref):
        i = pl.program_id(0)
        t = i * tm + lax.broadcasted_iota(jnp.int32, (tm, 1), 0)
        for g, wdw in enumerate(POOL_WINDOWS):
            cols = slice(g * gw, (g + 1) * gw)
            v = v_ref[:, cols].astype(F32)
            halo = jnp.where(i > 0, halo_ref[:, cols].astype(F32), 0.0)
            acc = jnp.concatenate([halo, v], axis=0)
            sh = 1
            while sh < wdw:
                acc = acc + pltpu.roll(acc, sh, 0)
                sh *= 2
            cnt = jnp.minimum(t + 1, wdw).astype(F32)
            pb = (acc[POOL_HALO:, :] / cnt - v).astype(BF16)
            p_ref[:, cols] = pb
            m_ref[:, cols] = jnp.dot(pb, w_ref[:, g].reshape(gw, gw), preferred_element_type=F32).astype(BF16)

    return pl.pallas_call(
        body, name="pool_fwd", grid=(s // tm,),
        in_specs=[pl.BlockSpec((tm, e), lambda i: (i, 0)),
                  pl.BlockSpec((POOL_HALO, e), lambda i: (jnp.maximum(i * hb - 1, 0), 0)),
                  pl.BlockSpec((N_DEV, ng, gw // N_DEV, gw), lambda i: (0, 0, 0, 0))],
        out_specs=[pl.BlockSpec((tm, e), lambda i: (i, 0)), pl.BlockSpec((tm, e), lambda i: (i, 0))],
        out_shape=[_sds((s, e), BF16), _sds((s, e), BF16)],
        compiler_params=_cp("arbitrary"),
    )(val, val, wg)


def _gate_out_fwd(m, z, ascale, w, h, gate, name):
    s, e = m.shape
    d = h.shape[1]
    tm = _row_tile(s, 512)

    def body(m_ref, z_ref, a_ref, w_ref, h_ref, g_ref, gd_ref, y_ref, ho_ref):
        z = z_ref[...].astype(F32)
        gb = ((m_ref[...].astype(F32) * a_ref[...]) * (z * _sigmoid(z))).astype(BF16)
        gd_ref[...] = gb
        y = jnp.dot(gb, w_ref[...].reshape(e, d), preferred_element_type=F32)
        y_ref[...] = y.astype(BF16)
        ho_ref[...] = h_ref[...] + g_ref[...] * y

    return pl.pallas_call(
        body, name=name, grid=(s // tm,),
        in_specs=[pl.BlockSpec((tm, e), lambda i: (i, 0)), pl.BlockSpec((tm, e), lambda i: (i, 0)), _vec(e),
                  pl.BlockSpec((N_DEV, e // N_DEV, d), lambda i: (0, 0, 0)),
                  pl.BlockSpec((tm, d), lambda i: (i, 0)), _vec(d)],
        out_specs=[pl.BlockSpec((tm, e), lambda i: (i, 0)), pl.BlockSpec((tm, d), lambda i: (i, 0)),
                   pl.BlockSpec((tm, d), lambda i: (i, 0))],
        out_shape=[_sds((s, e), BF16), _sds((s, d), BF16), _sds((s, d), F32)],
        compiler_params=_cp("arbitrary"),
    )(m, z, ascale, w, h, gate)


def _rel_onehot(shape, r_axis):
    r = lax.broadcasted_iota(jnp.int32, shape, r_axis)
    j = lax.broadcasted_iota(jnp.int32, shape, 1 - r_axis)
    dist = LEFT - (j - TQ)
    return (jnp.clip(dist, -REL_CLIP, REL_CLIP) + REL_CLIP == r).astype(F32)


def _skew(x, sign):
    row = lax.broadcasted_iota(jnp.int32, x.shape, 0)
    for b in range(TQ.bit_length() - 1):
        amt = (1 << b) if sign > 0 else FRAME_PAD - (1 << b)
        x = jnp.where(((row >> b) & 1) == 1, pltpu.roll(x, amt, 1), x)
    return x


def _bias_frames(rel):
    nh = rel.shape[0]

    def body(r_ref, f_ref, ft_ref):
        fext = jnp.dot(r_ref[...], _rel_onehot((REL_PAD, FRAME_PAD), 0), preferred_element_type=F32,
                       precision=lax.Precision.HIGHEST)
        x = _skew(jnp.broadcast_to(fext, (TQ, FRAME_PAD)), 1)[:, TQ:]
        qc = lax.broadcasted_iota(jnp.int32, (TQ, BAND), 0) // CHUNK
        m = lax.broadcasted_iota(jnp.int32, (TQ, BAND), 1)
        mc = m // CHUNK
        x = jnp.where((mc >= qc) & (mc <= qc + LEFT_CHUNKS), x, NEG)
        for f in range(NFRAME):
            xf = jnp.where(m >= LEFT - f * TQ, x, NEG)
            f_ref[f] = xf
            ft_ref[f] = xf.T

    return pl.pallas_call(
        body, name="bias_frames", grid=(nh,),
        in_specs=[pl.BlockSpec((None, 1, REL_PAD), lambda h: (h, 0, 0))],
        out_specs=[pl.BlockSpec((None, NFRAME, TQ, BAND), lambda h: (h, 0, 0, 0)),
                   pl.BlockSpec((None, NFRAME, BAND, TQ), lambda h: (h, 0, 0, 0))],
        out_shape=[_sds((nh, NFRAME, TQ, BAND), F32), _sds((nh, NFRAME, BAND, TQ), F32)],
        compiler_params=_cp("arbitrary"),
    )(rel)


def _bias_bwd(dft):
    nh = dft.shape[0]

    def body(d_ref, o_ref):
        d = d_ref[0]
        for f in range(1, NFRAME):
            d = d + d_ref[f]
        x = jnp.concatenate([jnp.zeros((TQ, TQ), F32), d.T], axis=1)
        col = jnp.sum(_skew(x, -1), axis=0, keepdims=True)
        o_ref[...] = jnp.dot(col, _rel_onehot((FRAME_PAD, REL_PAD), 1), preferred_element_type=F32,
                             precision=lax.Precision.HIGHEST)

    return pl.pallas_call(
        body, name="bias_bwd", grid=(nh,),
        in_specs=[pl.BlockSpec((None, NFRAME, BAND, TQ), lambda h: (h, 0, 0, 0))],
        out_specs=pl.BlockSpec((None, 1, REL_PAD), lambda h: (h, 0, 0)),
        out_shape=_sds((nh, 1, REL_PAD), F32),
        compiler_params=_cp("arbitrary"),
    )(dft)


def _attn_tiles(s):
    nt = s // TQ
    ni = min(ATTN_GROUP, nt)
    return nt // ni, ni


def _attn_fwd(q, k, v, frame):
    s, e = q.shape
    dh = e // N_HEADS
    ng, ni = _attn_tiles(s)
    sm = dh ** -0.5
    nb = min(ATTN_BATCH, ni)

    def body(q_ref, k_ref, v_ref, b_ref, o_ref):
        g0 = pl.program_id(1) * ni
        for tb in range(0, ni, nb):
            ts = range(tb, tb + nb)
            keys = [pl.ds(pl.multiple_of((g0 + t) * TQ, TQ), BAND) for t in ts]
            rows = [pl.ds(t * TQ, TQ) for t in ts]
            scs = [lax.dot_general(q_ref[r, :], k_ref[kk, :], NT, preferred_element_type=F32)
                   for r, kk in zip(rows, keys)]
            ps, ls = [], []
            for t, sc in zip(ts, scs):
                sc = sc * sm + b_ref[jnp.minimum(g0 + t, NFRAME - 1)]
                p = jnp.exp(sc - jnp.max(sc, axis=-1, keepdims=True))
                ls.append(jnp.sum(p, axis=-1, keepdims=True))
                ps.append(p.astype(BF16))
            for r, p, l, kk in zip(rows, ps, ls, keys):
                o = jnp.dot(p, v_ref[kk, :], preferred_element_type=F32)
                o_ref[r, :] = (o / l).astype(BF16)

    return pl.pallas_call(
        body, name="attn_fwd", grid=(N_HEADS, ng),
        in_specs=[pl.BlockSpec((ni * TQ, dh), lambda h, i: (i, h)),
                  pl.BlockSpec((s + LEFT, dh), lambda h, i: (0, h)), pl.BlockSpec((s + LEFT, dh), lambda h, i: (0, h)),
                  pl.BlockSpec((None, NFRAME, TQ, BAND), lambda h, i: (h, 0, 0, 0))],
        out_specs=pl.BlockSpec((ni * TQ, dh), lambda h, i: (i, h)),
        out_shape=_sds((s, e), BF16),
        compiler_params=_cp("arbitrary", "arbitrary"),
    )(q, k, v, frame)


def _final_loss(h, g, target):
    s, d = h.shape
    tm = _row_tile(s, 512)

    def body(h_ref, g_ref, t_ref, dh_ref, dg_ref, l_ref):
        @pl.when(pl.program_id(0) == 0)
        def _():
            dg_ref[...] = jnp.zeros_like(dg_ref)
            l_ref[...] = jnp.zeros_like(l_ref)

        hf = h_ref[...]
        gg = g_ref[...]
        rs = lax.rsqrt(jnp.mean(hf * hf, axis=-1, keepdims=True) + EPS)
        xhat = hf * rs
        diff = xhat * gg - t_ref[...]
        l_ref[...] += 0.5 * jnp.sum(jnp.mean(diff * diff, axis=-1, keepdims=True), axis=0, keepdims=True)
        dout = diff * (1.0 / d)
        dg_ref[...] += jnp.sum(dout * xhat, axis=0, keepdims=True)
        dxh = dout * gg
        dh_ref[...] = rs * (dxh - xhat * jnp.mean(dxh * xhat, axis=-1, keepdims=True))

    return pl.pallas_call(
        body, name="final_loss", grid=(s // tm,),
        in_specs=[pl.BlockSpec((tm, d), lambda i: (i, 0)), _vec(d), pl.BlockSpec((tm, d), lambda i: (i, 0))],
        out_specs=[pl.BlockSpec((tm, d), lambda i: (i, 0)), _vec(d), _vec(LANES)],
        out_shape=[_sds((s, d), F32), _sds((1, d), F32), _sds((1, LANES), F32)],
        compiler_params=_cp("arbitrary"),
    )(h, g, target)


def _out_bwd(dh, y, gate, w, m, z, ascale, name):
    s, d = dh.shape
    e = m.shape[1]
    tm = _row_tile(s, 512)

    def body(dh_ref, y_ref, g_ref, w_ref, m_ref, z_ref, a_ref, dy_ref, dm_ref, dz_ref, dg_ref, da_ref):
        @pl.when(pl.program_id(0) == 0)
        def _():
            dg_ref[...] = jnp.zeros_like(dg_ref)
            da_ref[...] = jnp.zeros_like(da_ref)

        dhf = dh_ref[...]
        dg_ref[...] += jnp.sum(dhf * y_ref[...].astype(F32), axis=0, keepdims=True)
        dyb = (g_ref[...] * dhf).astype(BF16)
        dy_ref[...] = dyb
        dgated = lax.dot_general(dyb, w_ref[...].reshape(e, d), NT, preferred_element_type=F32)
        z = z_ref[...].astype(F32)
        sig = _sigmoid(z)
        mf = m_ref[...].astype(F32)
        a = a_ref[...]
        dms = dgated * (z * sig)
        da_ref[...] += jnp.sum(dms * mf, axis=0, keepdims=True)
        dm_ref[...] = (dms * a).astype(BF16)
        dz_ref[...] = (dgated * (mf * a) * (sig * (1.0 + z * (1.0 - sig)))).astype(BF16)

    return pl.pallas_call(
        body, name=name, grid=(s // tm,),
        in_specs=[pl.BlockSpec((tm, d), lambda i: (i, 0)), pl.BlockSpec((tm, d), lambda i: (i, 0)), _vec(d),
                  pl.BlockSpec((N_DEV, e // N_DEV, d), lambda i: (0, 0, 0)),
                  pl.BlockSpec((tm, e), lambda i: (i, 0)), pl.BlockSpec((tm, e), lambda i: (i, 0)), _vec(e)],
        out_specs=[pl.BlockSpec((tm, d), lambda i: (i, 0)), pl.BlockSpec((tm, e), lambda i: (i, 0)),
                   pl.BlockSpec((tm, e), lambda i: (i, 0)), _vec(d), _vec(e)],
        out_shape=[_sds((s, d), BF16), _sds((s, e), BF16), _sds((s, e), BF16), _sds((1, d), F32), _sds((1, e), F32)],
        compiler_params=_cp("arbitrary"),
    )(dh, y, gate, w, m, z, ascale)


def _pool_bwd(dm, wg):
    s, e = dm.shape
    ng = len(POOL_WINDOWS)
    gw = e // ng
    tm = _row_tile(s, 256)
    hb = tm // POOL_HALO
    nsteps = s // tm

    def body(d_ref, halo_ref, w_ref, o_ref):
        i = pl.program_id(0)
        t = i * tm + lax.broadcasted_iota(jnp.int32, (tm + POOL_HALO, 1), 0)
        for g, wdw in enumerate(POOL_WINDOWS):
            cols = slice(g * gw, (g + 1) * gw)
            dmx = jnp.concatenate([d_ref[:, cols], halo_ref[:, cols]], axis=0)
            dp = lax.dot_general(dmx, w_ref[:, g].reshape(gw, gw), NT, preferred_element_type=F32)
            dp = jnp.where(t < s, dp, 0.0)
            acc = dp / jnp.minimum(t + 1, wdw).astype(F32)
            sh = 1
            while sh < wdw:
                acc = acc + pltpu.roll(acc, tm + POOL_HALO - sh, 0)
                sh *= 2
            o_ref[:, cols] = (acc[:tm, :] - dp[:tm, :]).astype(BF16)

    return pl.pallas_call(
        body, name="pool_bwd", grid=(nsteps,),
        in_specs=[pl.BlockSpec((tm, e), lambda i: (i, 0)),
                  pl.BlockSpec((POOL_HALO, e), lambda i: (jnp.minimum((i + 1) * hb, s // POOL_HALO - 1), 0)),
                  pl.BlockSpec((N_DEV, ng, gw // N_DEV, gw), lambda i: (0, 0, 0, 0))],
        out_specs=pl.BlockSpec((tm, e), lambda i: (i, 0)),
        out_shape=_sds((s, e), BF16),
        compiler_params=_cp("arbitrary"),
    )(dm, dm, wg)


def _attn_bwd(q, k, v, do, frame_t, prev=None):
    s, e = q.shape
    dh = e // N_HEADS
    ng, ni = _attn_tiles(s)
    sm = dh ** -0.5
    nleft = LEFT // TQ

    def core(q_ref, k_ref, v_ref, do_ref, b_ref, dq_ref, dk_ref, dv_ref, db_ref, init):
        i = pl.program_id(1)
        g0 = i * ni

        @pl.when(i == 0)
        def _():
            init()
            db_ref[...] = jnp.zeros_like(db_ref)

        keys =[pl.ds(pl.multiple_of((g0 + t) * TQ, TQ), BAND) for t in range(ni)]
        rows = [pl.ds(t * TQ, TQ) for t in range(ni)]
        sts = [lax.dot_general(k_ref[keys[t], :], q_ref[rows[t], :], NT, preferred_element_type=F32)
               for t in range(ni)]
        dpts = [lax.dot_general(v_ref[keys[t], :], do_ref[rows[t], :], NT, preferred_element_type=F32)
                for t in range(ni)]
        pbs, dsbs, dsts = [], [], []
        for t in range(ni):
            st = sts[t] * sm + b_ref[jnp.minimum(g0 + t, NFRAME - 1)]
            p = jnp.exp(st - jnp.max(st, axis=0, keepdims=True))
            p = p * (1.0 / jnp.sum(p, axis=0, keepdims=True))
            dst = p * (dpts[t] - jnp.sum(dpts[t] * p, axis=0, keepdims=True))
            pbs.append(p.astype(BF16))
            dsbs.append(dst.astype(BF16))
            dsts.append(dst)
        for t in range(min(ni, nleft)):
            db_ref[jnp.minimum(g0 + t, NFRAME - 1)] += dsts[t]
        if ni > nleft:
            rest = dsts[nleft]
            for t in range(nleft + 1, ni):
                rest = rest + dsts[t]
            db_ref[NFRAME - 1] += rest
        for t in range(ni):
            dq = lax.dot_general(dsbs[t], k_ref[keys[t], :], TN, preferred_element_type=F32)
            dq_ref[rows[t], :] = (dq * sm).astype(BF16)
        for r in range(ni + nleft):
            ts = [t for t in range(ni) if 0 <= r - t <= nleft]
            blk = lambda xs: jnp.concatenate([xs[t][(r - t) * TQ:(r - t + 1) * TQ, :] for t in ts], axis=1)
            qrows = slice(ts[0] * TQ, (ts[-1] + 1) * TQ)
            krows = pl.ds(pl.multiple_of((g0 + r) * TQ, TQ), TQ)
            dk_ref[krows, :] += jnp.dot(blk(dsbs), q_ref[qrows, :], preferred_element_type=F32) * sm
            dv_ref[krows, :] += jnp.dot(blk(pbs), do_ref[qrows, :], preferred_element_type=F32)

    tile_spec = pl.BlockSpec((ni * TQ, dh), lambda h, i: (i, h))
    kv_spec = pl.BlockSpec((s + LEFT, dh), lambda h, i: (0, h))
    fr_spec = pl.BlockSpec((None, NFRAME, BAND, TQ), lambda h, i: (h, 0, 0, 0))
    fr_shape = _sds((N_HEADS, NFRAME, BAND, TQ), F32)
    if prev is None:
        def body(q_ref, k_ref, v_ref, do_ref, b_ref, dq_ref, dk_ref, dv_ref, db_ref):
            def init():
                dk_ref[...] = jnp.zeros_like(dk_ref)
                dv_ref[...] = jnp.zeros_like(dv_ref)
            core(q_ref, k_ref, v_ref, do_ref, b_ref, dq_ref, dk_ref, dv_ref, db_ref, init)

        return pl.pallas_call(
            body, name="attn_bwd", grid=(N_HEADS, ng),
            in_specs=[tile_spec, kv_spec, kv_spec, tile_spec, fr_spec],
            out_specs=[tile_spec, kv_spec, kv_spec, fr_spec],
            out_shape=[_sds((s, e), BF16), _sds((s + LEFT, e), F32), _sds((s + LEFT, e), F32), fr_shape],
            compiler_params=_cp("arbitrary", "arbitrary"),
        )(q, k, v, do, frame_t)

    def body_acc(q_ref, k_ref, v_ref, do_ref, b_ref, dkp_ref, dvp_ref, dq_ref, dkb_ref, dvb_ref, db_ref, dk_acc, dv_acc):
        def init():
            dk_acc[...] = dkp_ref[...]
            dv_acc[...] = dvp_ref[...]
        core(q_ref, k_ref, v_ref, do_ref, b_ref, dq_ref, dk_acc, dv_acc, db_ref, init)

        @pl.when(pl.program_id(1) == ng - 1)
        def _():
            dkb_ref[...] = dk_acc[LEFT:, :].astype(BF16)
            dvb_ref[...] = dv_acc[LEFT:, :].astype(BF16)

    once = pl.BlockSpec((s + LEFT, dh), lambda h, i: (0, h), pipeline_mode=pl.Buffered(1))
    out_kv = pl.BlockSpec((s, dh), lambda h, i: (0, h))
    return pl.pallas_call(
        body_acc, name="attn_bwd_acc", grid=(N_HEADS, ng),
        in_specs=[tile_spec, once, once, tile_spec, fr_spec, once, once],
        out_specs=[tile_spec, out_kv, out_kv, fr_spec],
        out_shape=[_sds((s, e), BF16), _sds((s, e), BF16), _sds((s, e), BF16), fr_shape],
        scratch_shapes=[pltpu.VMEM((s + LEFT, dh), F32), pltpu.VMEM((s + LEFT, dh), F32)],
        compiler_params=_cp("arbitrary", "arbitrary"),
    )(q, k, v, do, frame_t, *prev)


def _in_bwd(dl, dr, w, h, g, scale, dres, name):
    s, d = h.shape
    nj, _, wn = w.shape
    half = nj // 2
    e = half * wn
    tm = _row_tile(s, 512)

    def body(dl_ref, dr_ref, w_ref, h_ref, g_ref, sc_ref, res_ref, dh_ref, st_ref):
        @pl.when(pl.program_id(0) == 0)
        def _():
            st_ref[...] = jnp.zeros_like(st_ref)

        du = jnp.zeros((tm, d), F32)
        for j in range(nj):
            src = dl_ref if j < half else dr_ref
            jj = j % half
            du = du + lax.dot_general(src[:, jj * wn:(jj + 1) * wn], w_ref[j], NT, preferred_element_type=F32)
        gg = g_ref[...]
        rs, xhat, n, _ = _norm_mod(h_ref[...], gg, 0.0, 0.0)
        dn = du * (1.0 + sc_ref[...])
        st_ref[0:1, :] += jnp.sum(du, axis=0, keepdims=True)
        st_ref[1:2, :] += jnp.sum(du * n, axis=0, keepdims=True)
        st_ref[2:3, :] += jnp.sum(dn * xhat, axis=0, keepdims=True)
        dxh = dn * gg
        dh_ref[...] = rs * (dxh - xhat * jnp.mean(dxh * xhat, axis=-1, keepdims=True)) + res_ref[...]

    return pl.pallas_call(
        body, name=name, grid=(s // tm,),
        in_specs=[pl.BlockSpec((tm, e), lambda i: (i, 0)), pl.BlockSpec((tm, e), lambda i: (i, 0)),
                  pl.BlockSpec((nj, d, wn), lambda i: (0, 0, 0)),
                  pl.BlockSpec((tm, d), lambda i: (i, 0)), _vec(d), _vec(d), pl.BlockSpec((tm, d), lambda i: (i, 0))],
        out_specs=[pl.BlockSpec((tm, d), lambda i: (i, 0)), pl.BlockSpec((8, d), lambda i: (0, 0))],
        out_shape=[_sds((s, d), F32), _sds((8, d), F32)],
        compiler_params=_cp("arbitrary"),
    )(dl, dr, w, h, g, scale, dres)


def _tn_matmul(x, ys, xw, yw, ymap, nb, out_shape, stage_shape, out_at, name):
    s, xfull = x.shape
    ts = _row_tile(s, 2048)
    ys = list(ys)
    half = nb // len(ys)
    nk = s // ts

    def body(*refs):
        x_ref, y_refs = refs[0], refs[1:1 + len(ys)]
        o_hbm, xt_ref, acc_ref, stage_ref, sem = refs[1 + len(ys):]
        k, b = pl.program_id(0), pl.program_id(1)

        @pl.when(b == 0)
        def _():
            xt_ref[...] = x_ref[...].T

        for n, y_ref in enumerate(y_refs):
            @pl.when((b >= n * half) & (b < (n + 1) * half))
            def _():
                xt = xt_ref[...] if xw == xfull else xt_ref[pl.ds(pl.multiple_of(b * xw, xw), xw), :]
                part = jnp.dot(xt, y_ref[...], preferred_element_type=F32)

                @pl.when(k == 0)
                def _():
                    acc_ref[b] = part

                @pl.when(k > 0)
                def _():
                    acc_ref[b] += part

        @pl.when(k == nk - 1)
        def _():
            stage_ref[...] = acc_ref[b].astype(BF16).reshape(stage_shape)
            cp = pltpu.make_async_copy(stage_ref, out_at(o_hbm, b), sem)
            cp.start()
            cp.wait()

    in_specs = [pl.BlockSpec((ts, xfull), lambda k, b: (k, 0))]
    for n in range(len(ys)):
        in_specs.append(pl.BlockSpec(
            (ts, yw), lambda k, b, n=n: (k, ymap(jnp.clip(b - n * half, 0, half - 1)))))
    return pl.pallas_call(
        body, name=name, grid=(nk, nb), in_specs=in_specs,
        out_specs=HBM_SPEC, out_shape=_sds(out_shape, BF16),
        scratch_shapes=[pltpu.VMEM((xfull, ts), BF16), pltpu.VMEM((nb, xw, yw), F32), pltpu.VMEM(stage_shape, BF16),
                        pltpu.SemaphoreType.DMA(())],
        compiler_params=_cp("arbitrary", "arbitrary"),
    )(x, *ys)


def _grad_w_in(u, dl, dr, name):
    d = u.shape[1]
    wn = 2 * dl.shape[1] // N_DEV
    return _tn_matmul(u, (dl, dr), d, wn, lambda b: b, N_DEV, (N_DEV, d, wn), (d, wn), lambda o, b: o.at[b], name)


def _grad_w_out(gated, dy, name):
    e, d = gated.shape[1], dy.shape[1]
    return _tn_matmul(gated, (dy,), e // N_DEV, d, lambda b: 0, N_DEV, (N_DEV, e // N_DEV, d), (e // N_DEV, d),
                      lambda o, b: o.at[b], name)


def _grad_w_group(pooled, dm, name):
    ng = len(POOL_WINDOWS)
    gw = pooled.shape[1] // ng
    return _tn_matmul(pooled, (dm,), gw, gw, lambda b: b, ng, (N_DEV, ng, gw // N_DEV, gw),
                      (N_DEV, gw // N_DEV, gw), lambda o, b: o.at[:, b], name)


def _adamw(staged, w, m, v, name):
    shape = w.shape
    nl = len(staged)
    n = staged[0].shape[0]
    cdim = shape[-1]
    total = 1
    for a in shape[:-1]:
        total *= a
    rows = total // nl
    sts = [st.reshape(n, rows, cdim) for st in staged]
    tr = rows if rows * cdim <= 128 * 1024 else max(8, (128 * 1024 // cdim) // 8 * 8)
    while rows % tr:
        tr -= 8
    nblk = rows // tr

    def body(*refs):
        s_refs = refs[:nl]
        w_ref, m_ref, v_ref, g_ref, d_ref, mo_ref, vo_ref = refs[nl:]
        for ll in range(nl):
            @pl.when(pl.program_id(0) == ll)
            def _():
                g = s_refs[ll][0].astype(F32)
                for j in range(1, n):
                    g = g + s_refs[ll][j].astype(F32)
                mn = ADAM_B1 * m_ref[...] + (1.0 - ADAM_B1) * g
                vn = ADAM_B2 * v_ref[...] + (1.0 - ADAM_B2) * (g * g)
                m_hat = mn / (1.0 - ADAM_B1 ** ADAM_STEP)
                v_hat = vn / (1.0 - ADAM_B2 ** ADAM_STEP)
                g_ref[...] = g
                d_ref[...] = -ADAM_LR * (m_hat / (jnp.sqrt(v_hat) + ADAM_EPS) + ADAM_WD * w_ref[...])
                mo_ref[...] = mn
                vo_ref[...] = vn

    blk = pl.BlockSpec((None, tr, cdim), lambda l, i: (l, i, 0))
    st_specs = [pl.BlockSpec((n, tr, cdim), lambda l, i, ll=ll: (0, jnp.clip(i + (l - ll) * nblk, 0, nblk - 1), 0))
                for ll in range(nl)]
    outs = pl.pallas_call(
        body, name=name, grid=(nl, nblk),
        in_specs=st_specs + [blk, blk, blk],
        out_specs=[blk] * 4, out_shape=[_sds((nl, rows, cdim), F32)] * 4,
        compiler_params=_cp("arbitrary", "arbitrary"),
    )(*sts, w.reshape(nl, rows, cdim), m.reshape(nl, rows, cdim), v.reshape(nl, rows, cdim))
    return [o.reshape(shape) for o in outs]


def _pack(parts, total):
    flat = jnp.concatenate([p.reshape(-1) for p in parts])
    return jnp.pad(flat, (0, total - flat.shape[0])).reshape(1, total)


def kernel(x, c, ada_w, ada_b, norm_g, a_w_in, a_w_group, a_scale, a_w_out, kv_norm_g, kv_ada_w, kv_ada_b, w_kv, b_w_in, b_rel_bias, b_w_out, final_g, loss_target, m_ada_w, m_ada_b, m_norm_g, m_a_w_in, m_a_w_group, m_a_scale, m_a_w_out, m_kv_norm_g, m_kv_ada_w, m_kv_ada_b, m_w_kv, m_b_w_in, m_b_rel_bias, m_b_w_out, m_final_g, v_ada_w, v_ada_b, v_norm_g, v_a_w_in, v_a_w_group, v_a_scale, v_a_w_out, v_kv_norm_g, v_kv_ada_w, v_kv_ada_b, v_w_kv, v_b_w_in, v_b_rel_bias, v_b_w_out, v_final_g):
    s, d = x.shape[1], x.shape[2]
    depth, _, wa = ada_w.shape
    wk = kv_ada_w.shape[1]
    n_a, n_b = a_w_in.shape[0], b_w_in.shape[0]
    e = a_w_out.shape[1] * N_DEV
    nrel = b_rel_bias.shape[-1]
    me = 4 * lax.axis_index("x") + 2 * lax.axis_index("y") + lax.axis_index("c")
    h0 = x[0]
    target = loss_target[0]

    bf = lambda a: a.astype(BF16)
    tie = lambda val, tok: lax.optimization_barrier((val, tok))[0]

    c_all = _exchange([c.reshape(1, 1, d)], [True], "gather_c")[0].reshape(N_DEV, d)
    ada_b_loc = lax.dynamic_slice(ada_b, (0, me * wa), (depth, wa)).reshape(depth, 1, wa)
    kv_ada_b_loc = lax.dynamic_slice(kv_ada_b, (me * wk,), (wk,)).reshape(1, wk)
    mods_cols = _mods_fwd(c_all, ada_w, ada_b_loc, kv_ada_w, kv_ada_b_loc)
    nm = depth * wa + wk
    mods_me, a_scale_all = _exchange([mods_cols.reshape(N_DEV, 1, nm), a_scale], [False, True], "exchange_mods")
    groups = [[bf(a_w_in[l]), bf(a_w_group[l]), bf(a_w_out[l])] for l in range(n_a)]
    groups.append([bf(w_kv), bf(b_w_in[0]), bf(b_w_out[0])])
    groups += [[bf(b_w_in[l]), bf(b_w_out[l])] for l in range(1, n_b)]
    handles = []
    token = mods_me
    for gi, grp in enumerate(groups):
        hd, token = _push_start(grp, [True] * len(grp), token, f"gather_start_{gi}")
        handles.append(hd)
    mods_me = mods_me.reshape(N_DEV, nm)
    mods = mods_me[:, :depth * wa].reshape(N_DEV, depth, wa).transpose(1, 0, 2).reshape(depth, 3, 1, d)
    kv_mods = mods_me[:, depth * wa:].reshape(2, 1, d)
    a_scale_full = a_scale_all.transpose(1, 0, 2).reshape(n_a, 1, e)
    ones_e = jnp.ones((1, e), F32)

    saved = []
    h = h0
    k = v = hk = wkv = None
    after = token
    for layer in range(depth):
        shift, scale, gate = mods[layer, 0], mods[layer, 1], mods[layer, 2]
        g = norm_g[layer].reshape(1, d)
        if layer < n_a:
            w_in, w_group, w_out = _push_wait(handles[layer], after, f"gather_wait_{layer}")
            u, left, z = _inproj_fwd(h, g, shift, scale, w_in, "inproj_fwd")
            pooled, mixed = _pool_fwd(left, w_group)
            gated, y, hn = _gate_out_fwd(mixed, z, a_scale_full[layer], w_out, h, gate, "gate_out_fwd")
            saved.append(dict(h=h, u=u, left=left, z=z, pooled=pooled, m=mixed, gated=gated, y=y,
                              w_in=w_in, w_group=w_group, w_out=w_out))
        else:
            bi = layer - n_a
            got = _push_wait(handles[layer], after, f"gather_wait_{layer}")
            if bi == 0:
                wkv, w_in, w_out = got
                hk, k, v = _inproj_fwd(h, kv_norm_g.reshape(1, d), kv_mods[0], kv_mods[1], wkv, "inproj_kv", pad=LEFT)
            else:
                w_in, w_out = got
            u, left, z = _inproj_fwd(h, g, shift, scale, w_in, "inproj_fwd")
            rel = jnp.pad(b_rel_bias[bi], ((0, 0), (0, REL_PAD - nrel))).reshape(N_HEADS, 1, REL_PAD)
            frame, frame_t = _bias_frames(rel)
            att = _attn_fwd(left, k, v, frame)
            gated, y, hn = _gate_out_fwd(att, z, ones_e, w_out, h, gate, "gate_out_fwd")
            saved.append(dict(h=h, u=u, left=left, z=z, m=att, gated=gated, y=y, frame_t=frame_t,
                              w_in=w_in, w_out=w_out))
        h = hn
        after = h
    dh, d_final_g, loss_part = _final_loss(h, final_g.reshape(1, d), target)

    d_mods = [None] * depth
    d_norm_g = [None] * depth
    d_rel = [None] * n_b
    d_ascale = [None] * n_a
    assert n_b == 2
    dk = dv = None
    tie = lambda val, tok: lax.optimization_barrier((val, tok))[0]
    grad_handles = [None] * (depth + 1)
    for layer in range(depth - 1, n_a - 1, -1):
        bi = layer - n_a
        sv = saved[layer]
        scale, gate = mods[layer, 1], mods[layer, 2]
        dy, datt, dz, d_gate, _ = _out_bwd(dh, sv["y"], gate, sv["w_out"], sv["m"], sv["z"], ones_e, "out_bwd")
        dq, dk, dv, dframe_t = _attn_bwd(sv["left"], k, v, datt, sv["frame_t"], None if bi == n_b - 1 else (dk, dv))
        d_rel[bi] = _bias_bwd(dframe_t).reshape(N_HEADS, REL_PAD)[:, :nrel]
        g_out = _grad_w_out(sv["gated"], dy, "grad_w_out")
        g_in = _grad_w_in(sv["u"], dq, dz, "grad_w_in")
        grad_handles[layer], tok = _push_start([g_in, g_out], [False, False], d_gate, f"grads_start_{layer}")
        dh, st = _in_bwd(tie(dq, tok), dz, sv["w_in"], sv["h"], norm_g[layer].reshape(1, d), scale, dh, "in_bwd")
        d_mods[layer] = jnp.concatenate([st[0], st[1], d_gate[0]])
        d_norm_g[layer] = st[2]
    dkb, dvb = dk, dv
    h_kv = saved[n_a]["h"]
    g_w_kv = _grad_w_in(hk, dkb, dvb, "grad_w_in")
    grad_handles[depth], tok = _push_start([g_w_kv], [False], dkb, "grads_start_kv")
    dh, st = _in_bwd(tie(dkb, tok), dvb, wkv, h_kv, kv_norm_g.reshape(1, d), kv_mods[1], dh, "in_bwd")
    d_kv_mods = jnp.concatenate([st[0], st[1]])
    d_kv_norm_g = st[2]
    for layer in range(n_a - 1, -1, -1):
        sv = saved[layer]
        scale, gate = mods[layer, 1], mods[layer, 2]
        dy, dm, dz, d_gate, da = _out_bwd(dh, sv["y"], gate, sv["w_out"], sv["m"], sv["z"], a_scale_full[layer],
                                          "out_bwd")
        d_ascale[layer] = da.reshape(N_DEV, e // N_DEV)
        g_group = _grad_w_group(sv["pooled"], dm, "grad_w_group")
        g_out = _grad_w_out(sv["gated"], dy, "grad_w_out")
        if layer == 0:
            last_handle, tok = _push_start([g_group, g_out], [False] * 2, d_gate, "grads_start_last")
            dm = tie(dm, tok)
        dval = _pool_bwd(dm, sv["w_group"])
        g_in = _grad_w_in(sv["u"], dval, dz, "grad_w_in")
        late = [g_in] if layer == 0 else [g_in, g_group, g_out]
        grad_handles[layer], tok = _push_start(late, [False] * len(late), d_gate, f"grads_start_{layer}")
        dh, st = _in_bwd(tie(dval, tok), dz, sv["w_in"], sv["h"], norm_g[layer].reshape(1, d), scale, dh, "in_bwd")
        d_mods[layer] = jnp.concatenate([st[0], st[1], d_gate[0]])
        d_norm_g[layer] = st[2]
    grad_x = dh.reshape(1, s, d)

    d_mods = jnp.stack(d_mods)
    dm_slots = jnp.concatenate(
        [d_mods.reshape(depth, N_DEV, wa).transpose(1, 0, 2).reshape(N_DEV, depth * wa), d_kv_mods.reshape(N_DEV, wk)],
        axis=1).reshape(N_DEV, 1, nm)
    small_parts = [d_mods, d_kv_mods, jnp.stack(d_norm_g), d_kv_norm_g, d_final_g, jnp.stack(d_rel), loss_part[0, :1]]
    n_small = sum(int(p.size) for p in small_parts)
    n_small_pad = -(-n_small // LANES) * LANES
    small = _pack(small_parts, n_small_pad).reshape(1, 1, n_small_pad)
    d_ascale_slots = jnp.stack(d_ascale, axis=1)
    s_a_scale, dm_cols, small_all = _exchange([d_ascale_slots, dm_slots, small], [False, False, True],
                                              "exchange_small")

    g_ada_w, g_kv_ada_w = _mods_bwd(c_all, dm_cols.reshape(N_DEV, nm), depth, wa, wk)
    res = {}
    res["ada_w"] = _adamw([g_ada_w[None]], ada_w, m_ada_w, v_ada_w, "adamw")
    res["kv_ada_w"] = _adamw([g_kv_ada_w[None]], kv_ada_w, m_kv_ada_w, v_kv_ada_w, "adamw")
    res["a_scale"] = _adamw([s_a_scale[:, l] for l in range(n_a)], a_scale, m_a_scale, v_a_scale, "adamw")
    after = res["ada_w"][1]
    s_b = [None] * n_b
    for layer in range(depth - 1, n_a - 1, -1):
        s_b[layer - n_a] = _push_wait(grad_handles[layer], after, f"grads_wait_{layer}")
        after = s_b[layer - n_a][0]
    res["b_w_in"] = _adamw([sb[0] for sb in s_b], b_w_in, m_b_w_in, v_b_w_in, "adamw")
    res["b_w_out"] = _adamw([sb[1] for sb in s_b], b_w_out, m_b_w_out, v_b_w_out, "adamw")
    s_w_kv, = _push_wait(grad_handles[depth], res["b_w_out"][1], "grads_wait_kv")
    res["w_kv"] = _adamw([s_w_kv], w_kv, m_w_kv, v_w_kv, "adamw")
    after = res["w_kv"][1]
    s_a = [None] * n_a
    for layer in range(n_a - 1, 0, -1):
        s_a[layer] = _push_wait(grad_handles[layer], after, f"grads_wait_{layer}")
        after = s_a[layer][0]
    s_group0, s_out0 = _push_wait(last_handle, after, "grads_wait_last")
    s_a[0] = [None, s_group0, s_out0]
    res["a_w_group"] = _adamw([sa[1] for sa in s_a], a_w_group, m_a_w_group, v_a_w_group, "adamw")
    res["a_w_out"] = _adamw([sa[2] for sa in s_a], a_w_out, m_a_w_out, v_a_w_out, "adamw")
    s_a[0][0], = _push_wait(grad_handles[0], res["a_w_out"][1], "grads_wait_0")
    res["a_w_in"] = _adamw([sa[0] for sa in s_a], a_w_in, m_a_w_in, v_a_w_in, "adamw")
    small_names = ["ada_b", "kv_ada_b", "norm_g", "kv_norm_g", "final_g", "b_rel_bias"]
    small_w = dict(ada_b=ada_b, kv_ada_b=kv_ada_b, norm_g=norm_g, kv_norm_g=kv_norm_g, final_g=final_g,
                   b_rel_bias=b_rel_bias)
    small_m = dict(ada_b=m_ada_b, kv_ada_b=m_kv_ada_b, norm_g=m_norm_g, kv_norm_g=m_kv_norm_g, final_g=m_final_g,
                   b_rel_bias=m_b_rel_bias)
    small_v = dict(ada_b=v_ada_b, kv_ada_b=v_kv_ada_b, norm_g=v_norm_g, kv_norm_g=v_kv_norm_g, final_g=v_final_g,
                   b_rel_bias=v_b_rel_bias)
    sw = _pack([small_w[n] for n in small_names], n_small_pad)
    smm = _pack([small_m[n] for n in small_names], n_small_pad)
    svv = _pack([small_v[n] for n in small_names] + [jnp.ones((n_small_pad - n_small + 1,), F32)], n_small_pad)
    small_out = _adamw([small_all.reshape(N_DEV, 1, n_small_pad)], sw, smm, svv, "adamw")
    off = 0
    for n in small_names:
        size = int(small_w[n].size)
        res[n] = [o[0, off:off + size].reshape(small_w[n].shape) for o in small_out]
        off += size
    loss = small_out[0][0, n_small - 1]

    order = ["ada_w", "ada_b", "norm_g", "a_w_in", "a_w_group", "a_scale", "a_w_out", "kv_norm_g", "kv_ada_w",
             "kv_ada_b", "w_kv", "b_w_in", "b_rel_bias", "b_w_out", "final_g"]
    outs = [loss, grad_x]
    for part in range(4):
        outs += [res[n][part] for n in order]
    return tuple(outs)
```

```python
import functools

import jax
import jax.numpy as jnp
from jax import lax
from jax.experimental import pallas as pl
from jax.experimental.pallas import tpu as pltpu

F32 = jnp.float32
BF16 = jnp.bfloat16

N_DEV = 8
CHUNK = 64
LEFT_CHUNKS = 8
N_HEADS = 16
POOL_WINDOWS = (2, 4, 8, 16)
REL_CLIP = 128
EPS = 1e-6
ADAM_LR = 0.001
ADAM_B1 = 0.9
ADAM_B2 = 0.999
ADAM_EPS = 1e-08
ADAM_WD = 0.01
ADAM_STEP = 10

LANES = 128
TQ = 2 * CHUNK
LEFT = LEFT_CHUNKS * CHUNK
BAND = LEFT + TQ
FRAME_PAD = BAND + TQ
NFRAME = LEFT // TQ + 1
POOL_HALO = 16
REL_PAD = 384
NEG = -1e30
ATTN_GROUP = 8
ATTN_BATCH = 4
VMEM_LIMIT = 56 * 1024 * 1024

NT = (((1,), (1,)), ((), ()))
TN = (((0,), (0,)), ((), ()))

HBM_SPEC = pl.BlockSpec(memory_space=pltpu.HBM)
ANY_SPEC = pl.BlockSpec(memory_space=pl.ANY)
SEM_SPEC = pl.BlockSpec(memory_space=pltpu.SEMAPHORE)


def _cp(*sem):
    return pltpu.CompilerParams(dimension_semantics=sem or None, vmem_limit_bytes=VMEM_LIMIT)


def _sds(shape, dtype):
    return jax.ShapeDtypeStruct(tuple(shape), dtype)


def _row_tile(s, want):
    return min(want, s)


def _vec(d):
    return pl.BlockSpec((1, d), lambda *_: (0, 0))


def _sigmoid(z):
    return 1.0 / (1.0 + jnp.exp(-z))


def _exchange(xs, gather, name):
    n = len(xs)
    out_shapes = []
    for x, ga in zip(xs, gather):
        out_shapes.append(_sds((N_DEV,) + (x.shape if ga else x.shape[1:]), x.dtype))

    def body(*refs):
        ins, outs = refs[:n], refs[n:2 * n]
        send_sems, recv_sems, local_sems = refs[2 * n:]
        mx, my, mc = lax.axis_index("x"), lax.axis_index("y"), lax.axis_index("c")
        me = 4 * mx + 2 * my + mc
        local = []
        for k in range(n):
            src = ins[k] if gather[k] else ins[k].at[me]
            cp = pltpu.make_async_copy(src, outs[k].at[me], local_sems.at[k])
            cp.start()
            local.append(cp)
        sends, recvs = [], []
        for r in range(1, N_DEV):
            px, py, pc = (mx + (r >> 2)) % 2, (my + ((r >> 1) & 1)) % 2, (mc + (r & 1)) % 2
            peer = 4 * px + 2 * py + pc
            for k in range(n):
                sem = k * (N_DEV - 1) + r - 1
                src = ins[k] if gather[k] else ins[k].at[peer]
                send = pltpu.make_async_remote_copy(
                    src_ref=src, dst_ref=outs[k].at[me], send_sem=send_sems.at[sem], recv_sem=recv_sems.at[sem],
                    device_id=(px, py, pc), device_id_type=pl.DeviceIdType.MESH)
                send.start()
                sends.append(send)
                recvs.append(pltpu.make_async_remote_copy(
                    src_ref=src, dst_ref=outs[k].at[peer], send_sem=send_sems.at[sem], recv_sem=recv_sems.at[sem],
                    device_id=(px, py, pc), device_id_type=pl.DeviceIdType.MESH))
        for cp in recvs:
            cp.wait_recv()
        for cp in sends:
            cp.wait_send()
        for cp in local:
            cp.wait()

    return pl.pallas_call(
        body, name=name, out_shape=out_shapes,
        in_specs=[HBM_SPEC] * n, out_specs=[HBM_SPEC] * n,
        scratch_shapes=[pltpu.SemaphoreType.DMA((n * (N_DEV - 1),)), pltpu.SemaphoreType.DMA((n * (N_DEV - 1),)),
                        pltpu.SemaphoreType.DMA((n,))],
    )(*xs)


def _peer(mx, my, mc, r):
    px, py, pc = (mx + (r >> 2)) % 2, (my + ((r >> 1) & 1)) % 2, (mc + (r & 1)) % 2
    return (px, py, pc), 4 * px + 2 * py + pc


def _push_start(xs, gather, dep, name):
    n = len(xs)
    nsem = n * (N_DEV - 1)
    me = 4 * lax.axis_index("x") + 2 * lax.axis_index("y") + lax.axis_index("c")
    lands = []
    for x, ga in zip(xs, gather):
        own = x[None] if ga else lax.dynamic_index_in_dim(x, me, 0, keepdims=True)
        empty = lax.empty((N_DEV,) + own.shape[1:], x.dtype)
        lands.append(lax.dynamic_update_slice(empty, own, (me,) + (0,) * (own.ndim - 1)))

    def body(*refs):
        ins, lands_in = refs[:n], refs[n:2 * n]
        send_sems, recv_sems = refs[2 * n + 1], refs[2 * n + 2]
        token = refs[-1]
        mx, my, mc = lax.axis_index("x"), lax.axis_index("y"), lax.axis_index("c")
        mine = 4 * mx + 2 * my + mc
        for k in range(n):
            for r in range(1, N_DEV):
                dev, peer = _peer(mx, my, mc, r)
                sem = k * (N_DEV - 1) + r - 1
                pltpu.make_async_remote_copy(
                    src_ref=ins[k] if gather[k] else ins[k].at[peer], dst_ref=lands_in[k].at[mine],
                    send_sem=send_sems.at[sem], recv_sem=recv_sems.at[sem],
                    device_id=dev, device_id_type=pl.DeviceIdType.MESH).start()
        token[...] = jnp.zeros_like(token)

    hbm = lambda a: pltpu.HBM(a.shape, a.dtype)
    outs = pl.pallas_call(
        body, name=name,
        out_shape=(pltpu.SemaphoreType.DMA((nsem,)), pltpu.SemaphoreType.DMA((nsem,)),
                   *[hbm(x) for x in xs], *[hbm(a) for a in lands], _sds((8, LANES), F32)),
        in_specs=[HBM_SPEC] * (2 * n) + [ANY_SPEC],
        out_specs=(SEM_SPEC, SEM_SPEC, *[HBM_SPEC] * (2 * n), pl.BlockSpec(memory_space=pltpu.VMEM)),
        input_output_aliases={k: 2 + k for k in range(2 * n)},
        compiler_params=pltpu.CompilerParams(has_side_effects=pltpu.SideEffectType.DATAFLOW_SIDE_EFFECTING),
    )(*[pltpu.with_memory_space_constraint(x, pltpu.HBM) for x in xs],
      *[pltpu.with_memory_space_constraint(a, pltpu.HBM) for a in lands], dep)
    return (outs[0], outs[1], outs[2:2 + n], outs[2 + n:2 + 2 * n], tuple(gather)), outs[-1]


def _push_wait(handle, after, name):
    send_sems, recv_sems, srcs, lands, gather = handle
    n = len(srcs)

    def body(*refs):
        ins, lands_in = refs[:n], refs[n:2 * n]
        send_sems, recv_sems = refs[2 * n], refs[2 * n + 1]
        mx, my, mc = lax.axis_index("x"), lax.axis_index("y"), lax.axis_index("c")
        for k in range(n):
            for r in range(1, N_DEV):
                dev, peer = _peer(mx, my, mc, r)
                sem = k * (N_DEV - 1) + r - 1
                cp = pltpu.make_async_remote_copy(
                    src_ref=ins[k] if gather[k] else ins[k].at[peer], dst_ref=lands_in[k].at[peer],
                    send_sem=send_sems.at[sem], recv_sem=recv_sems.at[sem],
                    device_id=dev, device_id_type=pl.DeviceIdType.MESH)
                cp.wait_send()
                cp.wait_recv()

    hbm = lambda a: pltpu.HBM(a.shape, a.dtype)
    outs = pl.pallas_call(
        body, name=name,
        out_shape=(*[hbm(x) for x in srcs], *[hbm(a) for a in lands]),
        in_specs=[HBM_SPEC] * (2 * n) + [SEM_SPEC, SEM_SPEC, ANY_SPEC],
        out_specs=tuple([HBM_SPEC] * (2 * n)),
        input_output_aliases={k: k for k in range(2 * n)},
        compiler_params=pltpu.CompilerParams(has_side_effects=pltpu.SideEffectType.DATAFLOW_SIDE_EFFECTING),
    )(*srcs, *lands, send_sems, recv_sems, after)
    return list(outs[n:])


def _mods_fwd(c_all, ada_w, ada_b_loc, kv_ada_w, kv_ada_b_loc):
    nl, d, wa = ada_w.shape
    wk = kv_ada_w.shape[1]

    def body(c_ref, w_ref, b_ref, kw_ref, kb_ref, o_ref):
        c = c_ref[...]
        ca = c * _sigmoid(c)
        for l in range(nl):
            o_ref[:, l * wa:(l + 1) * wa] = jnp.dot(
                ca, w_ref[l], preferred_element_type=F32, precision=lax.Precision.HIGHEST) + b_ref[l]
        o_ref[:, nl * wa:] = jnp.dot(
            ca, kw_ref[...], preferred_element_type=F32, precision=lax.Precision.HIGHEST) + kb_ref[...]

    return pl.pallas_call(body, name="mods_fwd", out_shape=_sds((N_DEV, nl * wa + wk), F32),
                          compiler_params=_cp())(c_all, ada_w, ada_b_loc, kv_ada_w, kv_ada_b_loc)


def _mods_bwd(c_all, dm, nl, wa, wk):
    d = c_all.shape[1]

    def body(c_ref, d_ref, gw_ref, gk_ref):
        c = c_ref[...]
        ca = c * _sigmoid(c)
        for l in range(nl):
            gw_ref[l] = lax.dot_general(ca, d_ref[:, l * wa:(l + 1) * wa], TN,
                                        preferred_element_type=F32, precision=lax.Precision.HIGHEST)
        gk_ref[...] = lax.dot_general(ca, d_ref[:, nl * wa:], TN,
                                      preferred_element_type=F32, precision=lax.Precision.HIGHEST)

    return pl.pallas_call(body, name="mods_bwd", out_shape=[_sds((nl, d, wa), F32), _sds((d, wk), F32)],
                          compiler_params=_cp())(c_all, dm)


def _norm_mod(hf, g, shift, scale):
    rs = lax.rsqrt(jnp.mean(hf * hf, axis=-1, keepdims=True) + EPS)
    xhat = hf * rs
    n = xhat * g
    return rs, xhat, n, n * (1.0 + scale) + shift


def _inproj_fwd(h, g, shift, scale, w, name, pad=0):
    s, d = h.shape
    nj, _, wn = w.shape
    half = nj // 2
    tm = _row_tile(s, 512)
    assert pad in (0, tm)
    pb = pad // tm

    def body(h_ref, g_ref, sh_ref, sc_ref, w_ref, u_ref, l_ref, r_ref):
        _, _, _, u = _norm_mod(h_ref[...], g_ref[...], sh_ref[...], sc_ref[...])
        ub = u.astype(BF16)
        u_ref[...] = ub
        for j in range(nj):
            o_ref = l_ref if j < half else r_ref
            jj = j % half
            o_ref[:, jj * wn:(jj + 1) * wn] = jnp.dot(ub, w_ref[j], preferred_element_type=F32).astype(BF16)
        if pb:
            @pl.when(pl.program_id(0) == 0)
            def _():
                l_ref[...] = jnp.zeros_like(l_ref)
                r_ref[...] = jnp.zeros_like(r_ref)

    e = half * wn
    src = lambda i: (jnp.maximum(i - pb, 0), 0)
    return pl.pallas_call(
        body, name=name, grid=(s // tm + pb,),
        in_specs=[pl.BlockSpec((tm, d), src), _vec(d), _vec(d), _vec(d),
                  pl.BlockSpec((nj, d, wn), lambda i: (0, 0, 0))],
        out_specs=[pl.BlockSpec((tm, d), src), pl.BlockSpec((tm, e), lambda i: (i, 0)),
                   pl.BlockSpec((tm, e), lambda i: (i, 0))],
        out_shape=[_sds((s, d), BF16), _sds((s + pad, e), BF16), _sds((s + pad, e), BF16)],
        compiler_params=_cp("arbitrary"),
    )(h, g, shift, scale, w)


def _pool_fwd(val, wg):
    s, e = val.shape
    ng = len(POOL_WINDOWS)
    gw = e // ng
    tm = _row_tile(s, 256)
    hb = tm // POOL_HALO

    def body(v_ref, halo_ref, w_ref, p_ref, m_ref):
        i = pl.program_id(0)
        t = i * tm + lax.broadcasted_iota(jnp.int32, (tm, 1), 0)
        for g, wdw in enumerate(POOL_WINDOWS):
            cols = slice(g * gw, (g + 1) * gw)
            v = v_ref[:, cols].astype(F32)
            halo = jnp.where(i > 0, halo_ref[:, cols].astype(F32), 0.0)
            acc = jnp.concatenate([halo, v], axis=0)
            sh = 1
            while sh < wdw:
                acc = acc + pltpu.roll(acc, sh, 0)
                sh *= 2
            cnt = jnp.minimum(t + 1, wdw).astype(F32)
            pb = (acc[POOL_HALO:, :] / cnt - v).astype(BF16)
            p_ref[:, cols] = pb
            m_ref[:, cols] = jnp.dot(pb, w_ref[:, g].reshape(gw, gw), preferred_element_type=F32).astype(BF16)

    return pl.pallas_call(
        body, name="pool_fwd", grid=(s // tm,),
        in_specs=[pl.BlockSpec((tm, e), lambda i: (i, 0)),
                  pl.BlockSpec((POOL_HALO, e), lambda i: (jnp.maximum(i * hb - 1, 0), 0)),
                  pl.BlockSpec((N_DEV, ng, gw // N_DEV, gw), lambda i: (0, 0, 0, 0))],
        out_specs=[pl.BlockSpec((tm, e), lambda i: (i, 0)), pl.BlockSpec((tm, e), lambda i: (i, 0))],
        out_shape=[_sds((s, e), BF16), _sds((s, e), BF16)],
        compiler_params=_cp("arbitrary"),
    )(val, val, wg)


def _gate_out_fwd(m, z, ascale, w, h, gate, name):
    s, e = m.shape
    d = h.shape[1]
    tm = _row_tile(s, 512)

    def body(m_ref, z_ref, a_ref, w_ref, h_ref, g_ref, gd_ref, y_ref, ho_ref):
        z = z_ref[...].astype(F32)
        gb = ((m_ref[...].astype(F32) * a_ref[...]) * (z * _sigmoid(z))).astype(BF16)
        gd_ref[...] = gb
        y = jnp.dot(gb, w_ref[...].reshape(e, d), preferred_element_type=F32)
        y_ref[...] = y.astype(BF16)
        ho_ref[...] = h_ref[...] + g_ref[...] * y

    return pl.pallas_call(
        body, name=name, grid=(s // tm,),
        in_specs=[pl.BlockSpec((tm, e), lambda i: (i, 0)), pl.BlockSpec((tm, e), lambda i: (i, 0)), _vec(e),
                  pl.BlockSpec((N_DEV, e // N_DEV, d), lambda i: (0, 0, 0)),
                  pl.BlockSpec((tm, d), lambda i: (i, 0)), _vec(d)],
        out_specs=[pl.BlockSpec((tm, e), lambda i: (i, 0)), pl.BlockSpec((tm, d), lambda i: (i, 0)),
                   pl.BlockSpec((tm, d), lambda i: (i, 0))],
        out_shape=[_sds((s, e), BF16), _sds((s, d), BF16), _sds((s, d), F32)],
        compiler_params=_cp("arbitrary"),
    )(m, z, ascale, w, h, gate)


def _rel_onehot(shape, r_axis):
    r = lax.broadcasted_iota(jnp.int32, shape, r_axis)
    j = lax.broadcasted_iota(jnp.int32, shape, 1 - r_axis)
    dist = LEFT - (j - TQ)
    return (jnp.clip(dist, -REL_CLIP, REL_CLIP) + REL_CLIP == r).astype(F32)


def _skew(x, sign):
    row = lax.broadcasted_iota(jnp.int32, x.shape, 0)
    for b in range(TQ.bit_length() - 1):
        amt = (1 << b) if sign > 0 else FRAME_PAD - (1 << b)
        x = jnp.where(((row >> b) & 1) == 1, pltpu.roll(x, amt, 1), x)
    return x


def _bias_frames(rel):
    nh = rel.shape[0]

    def body(r_ref, f_ref, ft_ref):
        fext = jnp.dot(r_ref[...], _rel_onehot((REL_PAD, FRAME_PAD), 0), preferred_element_type=F32,
                       precision=lax.Precision.HIGHEST)
        x = _skew(jnp.broadcast_to(fext, (TQ, FRAME_PAD)), 1)[:, TQ:]
        qc = lax.broadcasted_iota(jnp.int32, (TQ, BAND), 0) // CHUNK
        m = lax.broadcasted_iota(jnp.int32, (TQ, BAND), 1)
        mc = m // CHUNK
        x = jnp.where((mc >= qc) & (mc <= qc + LEFT_CHUNKS), x, NEG)
        for f in range(NFRAME):
            xf = jnp.where(m >= LEFT - f * TQ, x, NEG)
            f_ref[f] = xf
            ft_ref[f] = xf.T

    return pl.pallas_call(
        body, name="bias_frames", grid=(nh,),
        in_specs=[pl.BlockSpec((None, 1, REL_PAD), lambda h: (h, 0, 0))],
        out_specs=[pl.BlockSpec((None, NFRAME, TQ, BAND), lambda h: (h, 0, 0, 0)),
                   pl.BlockSpec((None, NFRAME, BAND, TQ), lambda h: (h, 0, 0, 0))],
        out_shape=[_sds((nh, NFRAME, TQ, BAND), F32), _sds((nh, NFRAME, BAND, TQ), F32)],
        compiler_params=_cp("arbitrary"),
    )(rel)


def _bias_bwd(dft):
    nh = dft.shape[0]

    def body(d_ref, o_ref):
        d = d_ref[0]
        for f in range(1, NFRAME):
            d = d + d_ref[f]
        x = jnp.concatenate([jnp.zeros((TQ, TQ), F32), d.T], axis=1)
        col = jnp.sum(_skew(x, -1), axis=0, keepdims=True)
        o_ref[...] = jnp.dot(col, _rel_onehot((FRAME_PAD, REL_PAD), 1), preferred_element_type=F32,
                             precision=lax.Precision.HIGHEST)

    return pl.pallas_call(
        body, name="bias_bwd", grid=(nh,),
        in_specs=[pl.BlockSpec((None, NFRAME, BAND, TQ), lambda h: (h, 0, 0, 0))],
        out_specs=pl.BlockSpec((None, 1, REL_PAD), lambda h: (h, 0, 0)),
        out_shape=_sds((nh, 1, REL_PAD), F32),
        compiler_params=_cp("arbitrary"),
    )(dft)


def _attn_tiles(s):
    nt = s // TQ
    ni = min(ATTN_GROUP, nt)
    return nt // ni, ni


def _attn_fwd(q, k, v, frame):
    s, e = q.shape
    dh = e // N_HEADS
    ng, ni = _attn_tiles(s)
    sm = dh ** -0.5
    nb = min(ATTN_BATCH, ni)

    def body(q_ref, k_ref, v_ref, b_ref, o_ref):
        g0 = pl.program_id(1) * ni
        for tb in range(0, ni, nb):
            ts = range(tb, tb + nb)
            keys = [pl.ds(pl.multiple_of((g0 + t) * TQ, TQ), BAND) for t in ts]
            rows = [pl.ds(t * TQ, TQ) for t in ts]
            scs = [lax.dot_general(q_ref[r, :], k_ref[kk, :], NT, preferred_element_type=F32)
                   for r, kk in zip(rows, keys)]
            ps, ls = [], []
            for t, sc in zip(ts, scs):
                sc = sc * sm + b_ref[jnp.minimum(g0 + t, NFRAME - 1)]
                p = jnp.exp(sc - jnp.max(sc, axis=-1, keepdims=True))
                ls.append(jnp.sum(p, axis=-1, keepdims=True))
                ps.append(p.astype(BF16))
            for r, p, l, kk in zip(rows, ps, ls, keys):
                o = jnp.dot(p, v_ref[kk, :], preferred_element_type=F32)
                o_ref[r, :] = (o / l).astype(BF16)

    return pl.pallas_call(
        body, name="attn_fwd", grid=(N_HEADS, ng),
        in_specs=[pl.BlockSpec((ni * TQ, dh), lambda h, i: (i, h)),
                  pl.BlockSpec((s + LEFT, dh), lambda h, i: (0, h)), pl.BlockSpec((s + LEFT, dh), lambda h, i: (0, h)),
                  pl.BlockSpec((None, NFRAME, TQ, BAND), lambda h, i: (h, 0, 0, 0))],
        out_specs=pl.BlockSpec((ni * TQ, dh), lambda h, i: (i, h)),
        out_shape=_sds((s, e), BF16),
        compiler_params=_cp("arbitrary", "arbitrary"),
    )(q, k, v, frame)


def _final_loss(h, g, target):
    s, d = h.shape
    tm = _row_tile(s, 512)

    def body(h_ref, g_ref, t_ref, dh_ref, dg_ref, l_ref):
        @pl.when(pl.program_id(0) == 0)
        def _():
            dg_ref[...] = jnp.zeros_like(dg_ref)
            l_ref[...] = jnp.zeros_like(l_ref)

        hf = h_ref[...]
        gg = g_ref[...]
        rs = lax.rsqrt(jnp.mean(hf * hf, axis=-1, keepdims=True) + EPS)
        xhat = hf * rs
        diff = xhat * gg - t_ref[...]
        l_ref[...] += 0.5 * jnp.sum(jnp.mean(diff * diff, axis=-1, keepdims=True), axis=0, keepdims=True)
        dout = diff * (1.0 / d)
        dg_ref[...] += jnp.sum(dout * xhat, axis=0, keepdims=True)
        dxh = dout * gg
        dh_ref[...] = rs * (dxh - xhat * jnp.mean(dxh * xhat, axis=-1, keepdims=True))

    return pl.pallas_call(
        body, name="final_loss", grid=(s // tm,),
        in_specs=[pl.BlockSpec((tm, d), lambda i: (i, 0)), _vec(d), pl.BlockSpec((tm, d), lambda i: (i, 0))],
        out_specs=[pl.BlockSpec((tm, d), lambda i: (i, 0)), _vec(d), _vec(LANES)],
        out_shape=[_sds((s, d), F32), _sds((1, d), F32), _sds((1, LANES), F32)],
        compiler_params=_cp("arbitrary"),
    )(h, g, target)


def _out_bwd(dh, y, gate, w, m, z, ascale, name):
    s, d = dh.shape
    e = m.shape[1]
    tm = _row_tile(s, 512)

    def body(dh_ref, y_ref, g_ref, w_ref, m_ref, z_ref, a_ref, dy_ref, dm_ref, dz_ref, dg_ref, da_ref):
        @pl.when(pl.program_id(0) == 0)
        def _():
            dg_ref[...] = jnp.zeros_like(dg_ref)
            da_ref[...] = jnp.zeros_like(da_ref)

        dhf = dh_ref[...]
        dg_ref[...] += jnp.sum(dhf * y_ref[...].astype(F32), axis=0, keepdims=True)
        dyb = (g_ref[...] * dhf).astype(BF16)
        dy_ref[...] = dyb
        dgated = lax.dot_general(dyb, w_ref[...].reshape(e, d), NT, preferred_element_type=F32)
        z = z_ref[...].astype(F32)
        sig = _sigmoid(z)
        mf = m_ref[...].astype(F32)
        a = a_ref[...]
        dms = dgated * (z * sig)
        da_ref[...] += jnp.sum(dms * mf, axis=0, keepdims=True)
        dm_ref[...] = (dms * a).astype(BF16)
        dz_ref[...] = (dgated * (mf * a) * (sig * (1.0 + z * (1.0 - sig)))).astype(BF16)

    return pl.pallas_call(
        body, name=name, grid=(s // tm,),
        in_specs=[pl.BlockSpec((tm, d), lambda i: (i, 0)), pl.BlockSpec((tm, d), lambda i: (i, 0)), _vec(d),
                  pl.BlockSpec((N_DEV, e // N_DEV, d), lambda i: (0, 0, 0)),
                  pl.BlockSpec((tm, e), lambda i: (i, 0)), pl.BlockSpec((tm, e), lambda i: (i, 0)), _vec(e)],
        out_specs=[pl.BlockSpec((tm, d), lambda i: (i, 0)), pl.BlockSpec((tm, e), lambda i: (i, 0)),
                   pl.BlockSpec((tm, e), lambda i: (i, 0)), _vec(d), _vec(e)],
        out_shape=[_sds((s, d), BF16), _sds((s, e), BF16), _sds((s, e), BF16), _sds((1, d), F32), _sds((1, e), F32)],
        compiler_params=_cp("arbitrary"),
    )(dh, y, gate, w, m, z, ascale)


def _pool_bwd(dm, wg):
    s, e = dm.shape
    ng = len(POOL_WINDOWS)
    gw = e // ng
    tm = _row_tile(s, 256)
    hb = tm // POOL_HALO
    nsteps = s // tm

    def body(d_ref, halo_ref, w_ref, o_ref):
        i = pl.program_id(0)
        t = i * tm + lax.broadcasted_iota(jnp.int32, (tm + POOL_HALO, 1), 0)
        for g, wdw in enumerate(POOL_WINDOWS):
            cols = slice(g * gw, (g + 1) * gw)
            dmx = jnp.concatenate([d_ref[:, cols], halo_ref[:, cols]], axis=0)
            dp = lax.dot_general(dmx, w_ref[:, g].reshape(gw, gw), NT, preferred_element_type=F32)
            dp = jnp.where(t < s, dp, 0.0)
            acc = dp / jnp.minimum(t + 1, wdw).astype(F32)
            sh = 1
            while sh < wdw:
                acc = acc + pltpu.roll(acc, tm + POOL_HALO - sh, 0)
                sh *= 2
            o_ref[:, cols] = (acc[:tm, :] - dp[:tm, :]).astype(BF16)

    return pl.pallas_call(
        body, name="pool_bwd", grid=(nsteps,),
        in_specs=[pl.BlockSpec((tm, e), lambda i: (i, 0)),
                  pl.BlockSpec((POOL_HALO, e), lambda i: (jnp.minimum((i + 1) * hb, s // POOL_HALO - 1), 0)),
                  pl.BlockSpec((N_DEV, ng, gw // N_DEV, gw), lambda i: (0, 0, 0, 0))],
        out_specs=pl.BlockSpec((tm, e), lambda i: (i, 0)),
        out_shape=_sds((s, e), BF16),
        compiler_params=_cp("arbitrary"),
    )(dm, dm, wg)


def _attn_bwd(q, k, v, do, frame_t, prev=None):
    s, e = q.shape
    dh = e // N_HEADS
    ng, ni = _attn_tiles(s)
    sm = dh ** -0.5
    nleft = LEFT // TQ

    def core(q_ref, k_ref, v_ref, do_ref, b_ref, dq_ref, dk_ref, dv_ref, db_ref, init):
        i = pl.program_id(1)
        g0 = i * ni

        @pl.when(i == 0)
        def _():
            init()
            db_ref[...] = jnp.zeros_like(db_ref)

        keys = [pl.ds(pl.multiple_of((g0 + t) * TQ, TQ), BAND) for t in range(ni)]
        rows = [pl.ds(t * TQ, TQ) for t in range(ni)]
        pbs, dsbs, dsts = [], [], []
        nb = min(ATTN_BATCH, ni)
        for tb in range(0, ni, nb):
            sts = [lax.dot_general(k_ref[keys[t], :], q_ref[rows[t], :], NT, preferred_element_type=F32)
                   for t in range(tb, tb + nb)]
            dpts = [lax.dot_general(v_ref[keys[t], :], do_ref[rows[t], :], NT, preferred_element_type=F32)
                    for t in range(tb, tb + nb)]
            for t, st, dpt in zip(range(tb, tb + nb), sts, dpts):
                st = st * sm + b_ref[jnp.minimum(g0 + t, NFRAME - 1)]
                p = jnp.exp(st - jnp.max(st, axis=0, keepdims=True))
                p = p * (1.0 / jnp.sum(p, axis=0, keepdims=True))
                dst = p * (dpt - jnp.sum(dpt * p, axis=0, keepdims=True))
                pbs.append(p.astype(BF16))
                dsbs.append(dst.astype(BF16))
                dsts.append(dst)
        for t in range(min(ni, nleft)):
            db_ref[jnp.minimum(g0 + t, NFRAME - 1)] += dsts[t]
        if ni > nleft:
            rest = dsts[nleft]
            for t in range(nleft + 1, ni):
                rest = rest + dsts[t]
            db_ref[NFRAME - 1] += rest
        for t in range(ni):
            dq = lax.dot_general(dsbs[t], k_ref[keys[t], :], TN, preferred_element_type=F32)
            dq_ref[rows[t], :] = (dq * sm).astype(BF16)
        for r in range(ni + nleft):
            ts = [t for t in range(ni) if 0 <= r - t <= nleft]
            blk = lambda xs: jnp.concatenate([xs[t][(r - t) * TQ:(r - t + 1) * TQ, :] for t in ts], axis=1)
            qrows = slice(ts[0] * TQ, (ts[-1] + 1) * TQ)
            krows = pl.ds(pl.multiple_of((g0 + r) * TQ, TQ), TQ)
            dk_ref[krows, :] += jnp.dot(blk(dsbs), q_ref[qrows, :], preferred_element_type=F32) * sm
            dv_ref[krows, :] += jnp.dot(blk(pbs), do_ref[qrows, :], preferred_element_type=F32)

    tile_spec = pl.BlockSpec((ni * TQ, dh), lambda h, i: (i, h))
    kv_spec = pl.BlockSpec((s + LEFT, dh), lambda h, i: (0, h))
    fr_spec = pl.BlockSpec((None, NFRAME, BAND, TQ), lambda h, i: (h, 0, 0, 0))
    fr_shape = _sds((N_HEADS, NFRAME, BAND, TQ), F32)
    if prev is None:
        def body(q_ref, k_ref, v_ref, do_ref, b_ref, dq_ref, dk_ref, dv_ref, db_ref):
            def init():
                dk_ref[...] = jnp.zeros_like(dk_ref)
                dv_ref[...] = jnp.zeros_like(dv_ref)
            core(q_ref, k_ref, v_ref, do_ref, b_ref, dq_ref, dk_ref, dv_ref, db_ref, init)

        return pl.pallas_call(
            body, name="attn_bwd", grid=(N_HEADS, ng),
            in_specs=[tile_spec, kv_spec, kv_spec, tile_spec, fr_spec],
            out_specs=[tile_spec, kv_spec, kv_spec, fr_spec],
            out_shape=[_sds((s, e), BF16), _sds((s + LEFT, e), F32), _sds((s + LEFT, e), F32), fr_shape],
            compiler_params=_cp("arbitrary", "arbitrary"),
        )(q, k, v, do, frame_t)

    def body_acc(q_ref, k_ref, v_ref, do_ref, b_ref, dkp_hbm, dvp_hbm, dq_ref, dkb_ref, dvb_ref, db_ref,
                 dk_acc, dv_acc, dkp_buf, dvp_buf, sems):
        cols = pl.ds(pl.multiple_of(pl.program_id(0) * dh, dh), dh)
        fetch = [pltpu.make_async_copy(dkp_hbm.at[:, cols], dkp_buf, sems.at[0]),
                 pltpu.make_async_copy(dvp_hbm.at[:, cols], dvp_buf, sems.at[1])]

        def init():
            for cp in fetch:
                cp.start()
            dk_acc[...] = jnp.zeros_like(dk_acc)
            dv_acc[...] = jnp.zeros_like(dv_acc)
        core(q_ref, k_ref, v_ref, do_ref, b_ref, dq_ref, dk_acc, dv_acc, db_ref, init)

        @pl.when(pl.program_id(1) == ng - 1)
        def _():
            for cp in fetch:
                cp.wait()
            dkb_ref[...] = (dk_acc[LEFT:, :] + dkp_buf[LEFT:, :]).astype(BF16)
            dvb_ref[...] = (dv_acc[LEFT:, :] + dvp_buf[LEFT:, :]).astype(BF16)

    out_kv = pl.BlockSpec((s, dh), lambda h, i: (0, h))
    acc = pltpu.VMEM((s + LEFT, dh), F32)
    return pl.pallas_call(
        body_acc, name="attn_bwd_acc", grid=(N_HEADS, ng),
        in_specs=[tile_spec, kv_spec, kv_spec, tile_spec, fr_spec, ANY_SPEC, ANY_SPEC],
        out_specs=[tile_spec, out_kv, out_kv, fr_spec],
        out_shape=[_sds((s, e), BF16), _sds((s, e), BF16), _sds((s, e), BF16), fr_shape],
        scratch_shapes=[acc, acc, acc, acc, pltpu.SemaphoreType.DMA((2,))],
        compiler_params=_cp("arbitrary", "arbitrary"),
    )(q, k, v, do, frame_t, *prev)


def _in_bwd(dl, dr, w, h, g, scale, dres, name):
    s, d = h.shape
    nj, _, wn = w.shape
    half = nj // 2
    e = half * wn
    tm = _row_tile(s, 512)

    def body(dl_ref, dr_ref, w_ref, h_ref, g_ref, sc_ref, res_ref, dh_ref, st_ref):
        @pl.when(pl.program_id(0) == 0)
        def _():
            st_ref[...] = jnp.zeros_like(st_ref)

        du = jnp.zeros((tm, d), F32)
        for j in range(nj):
            src = dl_ref if j < half else dr_ref
            jj = j % half
            du = du + lax.dot_general(src[:, jj * wn:(jj + 1) * wn], w_ref[j], NT, preferred_element_type=F32)
        gg = g_ref[...]
        rs, xhat, n, _ = _norm_mod(h_ref[...], gg, 0.0, 0.0)
        dn = du * (1.0 + sc_ref[...])
        st_ref[0:1, :] += jnp.sum(du, axis=0, keepdims=True)
        st_ref[1:2, :] += jnp.sum(du * n, axis=0, keepdims=True)
        st_ref[2:3, :] += jnp.sum(dn * xhat, axis=0, keepdims=True)
        dxh = dn * gg
        dh_ref[...] = rs * (dxh - xhat * jnp.mean(dxh * xhat, axis=-1, keepdims=True)) + res_ref[...]

    return pl.pallas_call(
        body, name=name, grid=(s // tm,),
        in_specs=[pl.BlockSpec((tm, e), lambda i: (i, 0)), pl.BlockSpec((tm, e), lambda i: (i, 0)),
                  pl.BlockSpec((nj, d, wn), lambda i: (0, 0, 0)),
                  pl.BlockSpec((tm, d), lambda i: (i, 0)), _vec(d), _vec(d), pl.BlockSpec((tm, d), lambda i: (i, 0))],
        out_specs=[pl.BlockSpec((tm, d), lambda i: (i, 0)), pl.BlockSpec((8, d), lambda i: (0, 0))],
        out_shape=[_sds((s, d), F32), _sds((8, d), F32)],
        compiler_params=_cp("arbitrary"),
    )(dl, dr, w, h, g, scale, dres)


def _tn_matmul(x, ys, xw, yw, ymap, nb, out_shape, stage_shape, out_at, name):
    s, xfull = x.shape
    ts = _row_tile(s, 2048)
    ys = list(ys)
    half = nb // len(ys)
    nk = s // ts

    def body(*refs):
        x_ref, y_refs = refs[0], refs[1:1 + len(ys)]
        o_hbm, xt_ref, acc_ref, stage_ref, sem = refs[1 + len(ys):]
        k, b = pl.program_id(0), pl.program_id(1)

        @pl.when(b == 0)
        def _():
            xt_ref[...] = x_ref[...].T

        for n, y_ref in enumerate(y_refs):
            @pl.when((b >= n * half) & (b < (n + 1) * half))
            def _():
                xt = xt_ref[...] if xw == xfull else xt_ref[pl.ds(pl.multiple_of(b * xw, xw), xw), :]
                part = jnp.dot(xt, y_ref[...], preferred_element_type=F32)

                @pl.when(k == 0)
                def _():
                    acc_ref[b] = part

                @pl.when(k > 0)
                def _():
                    acc_ref[b] += part

        @pl.when(k == nk - 1)
        def _():
            stage_ref[...] = acc_ref[b].astype(BF16).reshape(stage_shape)
            cp = pltpu.make_async_copy(stage_ref, out_at(o_hbm, b), sem)
            cp.start()
            cp.wait()

    in_specs = [pl.BlockSpec((ts, xfull), lambda k, b: (k, 0))]
    for n in range(len(ys)):
        in_specs.append(pl.BlockSpec(
            (ts, yw), lambda k, b, n=n: (k, ymap(jnp.clip(b - n * half, 0, half - 1)))))
    return pl.pallas_call(
        body, name=name, grid=(nk, nb), in_specs=in_specs,
        out_specs=HBM_SPEC, out_shape=_sds(out_shape, BF16),
        scratch_shapes=[pltpu.VMEM((xfull, ts), BF16), pltpu.VMEM((nb, xw, yw), F32), pltpu.VMEM(stage_shape, BF16),
                        pltpu.SemaphoreType.DMA(())],
        compiler_params=_cp("arbitrary", "arbitrary"),
    )(x, *ys)


def _grad_w_in(u, dl, dr, name):
    d = u.shape[1]
    wn = 2 * dl.shape[1] // N_DEV
    return _tn_matmul(u, (dl, dr), d, wn, lambda b: b, N_DEV, (N_DEV, d, wn), (d, wn), lambda o, b: o.at[b], name)


def _grad_w_out(gated, dy, name):
    e, d = gated.shape[1], dy.shape[1]
    return _tn_matmul(gated, (dy,), e // N_DEV, d, lambda b: 0, N_DEV, (N_DEV, e // N_DEV, d), (e // N_DEV, d),
                      lambda o, b: o.at[b], name)


def _grad_w_group(pooled, dm, name):
    ng = len(POOL_WINDOWS)
    gw = pooled.shape[1] // ng
    return _tn_matmul(pooled, (dm,), gw, gw, lambda b: b, ng, (N_DEV, ng, gw // N_DEV, gw),
                      (N_DEV, gw // N_DEV, gw), lambda o, b: o.at[:, b], name)


def _adamw(staged, w, m, v, name):
    shape = w.shape
    nl = len(staged)
    n = staged[0].shape[0]
    cdim = shape[-1]
    total = 1
    for a in shape[:-1]:
        total *= a
    rows = total // nl
    sts = [st.reshape(n, rows, cdim) for st in staged]
    tr = rows if rows * cdim <= 128 * 1024 else max(8, (128 * 1024 // cdim) // 8 * 8)
    while rows % tr:
        tr -= 8
    nblk = rows // tr

    def body(*refs):
        s_refs = refs[:nl]
        w_ref, m_ref, v_ref, g_ref, d_ref, mo_ref, vo_ref = refs[nl:]
        for ll in range(nl):
            @pl.when(pl.program_id(0) == ll)
            def _():
                g = s_refs[ll][0].astype(F32)
                for j in range(1, n):
                    g = g + s_refs[ll][j].astype(F32)
                mn = ADAM_B1 * m_ref[...] + (1.0 - ADAM_B1) * g
                vn = ADAM_B2 * v_ref[...] + (1.0 - ADAM_B2) * (g * g)
                m_hat = mn / (1.0 - ADAM_B1 ** ADAM_STEP)
                v_hat = vn / (1.0 - ADAM_B2 ** ADAM_STEP)
                g_ref[...] = g
                d_ref[...] = -ADAM_LR * (m_hat / (jnp.sqrt(v_hat) + ADAM_EPS) + ADAM_WD * w_ref[...])
                mo_ref[...] = mn
                vo_ref[...] = vn

    blk = pl.BlockSpec((None, tr, cdim), lambda l, i: (l, i, 0))
    st_specs = [pl.BlockSpec((n, tr, cdim), lambda l, i, ll=ll: (0, jnp.clip(i + (l - ll) * nblk, 0, nblk - 1), 0))
                for ll in range(nl)]
    outs = pl.pallas_call(
        body, name=name, grid=(nl, nblk),
        in_specs=st_specs + [blk, blk, blk],
        out_specs=[blk] * 4, out_shape=[_sds((nl, rows, cdim), F32)] * 4,
        compiler_params=_cp("arbitrary", "arbitrary"),
    )(*sts, w.reshape(nl, rows, cdim), m.reshape(nl, rows, cdim), v.reshape(nl, rows, cdim))
    return [o.reshape(shape) for o in outs]


def _pack(parts, total):
    flat = jnp.concatenate([p.reshape(-1) for p in parts])
    return jnp.pad(flat, (0, total - flat.shape[0])).reshape(1, total)


def kernel(x, c, ada_w, ada_b, norm_g, a_w_in, a_w_group, a_scale, a_w_out, kv_norm_g, kv_ada_w, kv_ada_b, w_kv, b_w_in, b_rel_bias, b_w_out, final_g, loss_target, m_ada_w, m_ada_b, m_norm_g, m_a_w_in, m_a_w_group, m_a_scale, m_a_w_out, m_kv_norm_g, m_kv_ada_w, m_kv_ada_b, m_w_kv, m_b_w_in, m_b_rel_bias, m_b_w_out, m_final_g, v_ada_w, v_ada_b, v_norm_g, v_a_w_in, v_a_w_group, v_a_scale, v_a_w_out, v_kv_norm_g, v_kv_ada_w, v_kv_ada_b, v_w_kv, v_b_w_in, v_b_rel_bias, v_b_w_out, v_final_g):
    s, d = x.shape[1], x.shape[2]
    depth, _, wa = ada_w.shape
    wk = kv_ada_w.shape[1]
    n_a, n_b = a_w_in.shape[0], b_w_in.shape[0]
    e = a_w_out.shape[1] * N_DEV
    nrel = b_rel_bias.shape[-1]
    me = 4 * lax.axis_index("x") + 2 * lax.axis_index("y") + lax.axis_index("c")
    h0 = x[0]
    target = loss_target[0]

    bf = lambda a: a.astype(BF16)
    tie = lambda val, tok: lax.optimization_barrier((val, tok))[0]

    c_all = _exchange([c.reshape(1, 1, d)], [True], "gather_c")[0].reshape(N_DEV, d)
    ada_b_loc = lax.dynamic_slice(ada_b, (0, me * wa), (depth, wa)).reshape(depth, 1, wa)
    kv_ada_b_loc = lax.dynamic_slice(kv_ada_b, (me * wk,), (wk,)).reshape(1, wk)
    mods_cols = _mods_fwd(c_all, ada_w, ada_b_loc, kv_ada_w, kv_ada_b_loc)
    nm = depth * wa + wk
    mods_me, a_scale_all = _exchange([mods_cols.reshape(N_DEV, 1, nm), a_scale], [False, True], "exchange_mods")
    groups = [[bf(a_w_in[l]), bf(a_w_group[l]), bf(a_w_out[l])] for l in range(n_a)]
    groups.append([bf(w_kv), bf(b_w_in[0]), bf(b_w_out[0])])
    groups += [[bf(b_w_in[l]), bf(b_w_out[l])] for l in range(1, n_b)]
    handles = []
    token = mods_me
    for gi, grp in enumerate(groups):
        hd, token = _push_start(grp, [True] * len(grp), token, f"gather_start_{gi}")
        handles.append(hd)
    mods_me = mods_me.reshape(N_DEV, nm)
    mods = mods_me[:, :depth * wa].reshape(N_DEV, depth, wa).transpose(1, 0, 2).reshape(depth, 3, 1, d)
    kv_mods = mods_me[:, depth * wa:].reshape(2, 1, d)
    a_scale_full = a_scale_all.transpose(1, 0, 2).reshape(n_a, 1, e)
    ones_e = jnp.ones((1, e), F32)

    saved = []
    h = h0
    k = v = hk = wkv = None
    frames = [_bias_frames(tie(jnp.pad(b_rel_bias[bi], ((0, 0), (0, REL_PAD - nrel))), token).reshape(N_HEADS, 1, REL_PAD))
              for bi in range(n_b)]
    after = frames[-1][1]
    for layer in range(depth):
        shift, scale, gate = mods[layer, 0], mods[layer, 1], mods[layer, 2]
        g = norm_g[layer].reshape(1, d)
        if layer < n_a:
            w_in, w_group, w_out = _push_wait(handles[layer], after, f"gather_wait_{layer}")
            u, left, z = _inproj_fwd(h, g, shift, scale, w_in, "inproj_fwd")
            pooled, mixed = _pool_fwd(left, w_group)
            gated, y, hn = _gate_out_fwd(mixed, z, a_scale_full[layer], w_out, h, gate, "gate_out_fwd")
            saved.append(dict(h=h, u=u, left=left, z=z, pooled=pooled, m=mixed, gated=gated, y=y,
                              w_in=w_in, w_group=w_group, w_out=w_out))
        else:
            bi = layer - n_a
            got = _push_wait(handles[layer], after, f"gather_wait_{layer}")
            if bi == 0:
                wkv, w_in, w_out = got
                hk, k, v = _inproj_fwd(h, kv_norm_g.reshape(1, d), kv_mods[0], kv_mods[1], wkv, "inproj_kv", pad=LEFT)
            else:
                w_in, w_out = got
            u, left, z = _inproj_fwd(h, g, shift, scale, w_in, "inproj_fwd")
            frame, frame_t = frames[bi]
            att = _attn_fwd(left, k, v, frame)
            gated, y, hn = _gate_out_fwd(att, z, ones_e, w_out, h, gate, "gate_out_fwd")
            saved.append(dict(h=h, u=u, left=left, z=z, m=att, gated=gated, y=y, frame_t=frame_t,
                              w_in=w_in, w_out=w_out))
        h = hn
        after = h
    dh, d_final_g, loss_part = _final_loss(h, final_g.reshape(1, d), target)

    d_mods = [None] * depth
    d_norm_g = [None] * depth
    d_rel = [None] * n_b
    d_ascale = [None] * n_a
    assert n_b == 2
    dk = dv = None
    tie = lambda val, tok: lax.optimization_barrier((val, tok))[0]
    grad_handles = [None] * (depth + 1)
    for layer in range(depth - 1, n_a - 1, -1):
        bi = layer - n_a
        sv = saved[layer]
        scale, gate = mods[layer, 1], mods[layer, 2]
        dy, datt, dz, d_gate, _ = _out_bwd(dh, sv["y"], gate, sv["w_out"], sv["m"], sv["z"], ones_e, "out_bwd")
        dq, dk, dv, dframe_t = _attn_bwd(sv["left"], k, v, datt, sv["frame_t"], None if bi == n_b - 1 else (dk, dv))
        d_rel[bi] = _bias_bwd(dframe_t).reshape(N_HEADS, REL_PAD)[:, :nrel]
        g_out = _grad_w_out(sv["gated"], dy, "grad_w_out")
        g_in = _grad_w_in(sv["u"], dq, dz, "grad_w_in")
        grad_handles[layer], tok = _push_start([g_in, g_out], [False, False], d_gate, f"grads_start_{layer}")
        dh, st = _in_bwd(tie(dq, tok), dz, sv["w_in"], sv["h"], norm_g[layer].reshape(1, d), scale, dh, "in_bwd")
        d_mods[layer] = jnp.concatenate([st[0], st[1], d_gate[0]])
        d_norm_g[layer] = st[2]
    dkb, dvb = dk, dv
    h_kv = saved[n_a]["h"]
    g_w_kv = _grad_w_in(hk, dkb, dvb, "grad_w_in")
    grad_handles[depth], tok = _push_start([g_w_kv], [False], dkb, "grads_start_kv")
    dh, st = _in_bwd(tie(dkb, tok), dvb, wkv, h_kv, kv_norm_g.reshape(1, d), kv_mods[1], dh, "in_bwd")
    d_kv_mods = jnp.concatenate([st[0], st[1]])
    d_kv_norm_g = st[2]
    for layer in range(n_a - 1, -1, -1):
        sv = saved[layer]
        scale, gate = mods[layer, 1], mods[layer, 2]
        dy, dm, dz, d_gate, da = _out_bwd(dh, sv["y"], gate, sv["w_out"], sv["m"], sv["z"], a_scale_full[layer],
                                          "out_bwd")
        d_ascale[layer] = da.reshape(N_DEV, e // N_DEV)
        g_group = _grad_w_group(sv["pooled"], dm, "grad_w_group")
        g_out = _grad_w_out(sv["gated"], dy, "grad_w_out")
        if layer == 0:
            last_handle, tok = _push_start([g_group, g_out], [False] * 2, d_gate, "grads_start_last")
            dm = tie(dm, tok)
        dval = _pool_bwd(dm, sv["w_group"])
        g_in = _grad_w_in(sv["u"], dval, dz, "grad_w_in")
        late = [g_in] if layer == 0 else [g_in, g_group, g_out]
        grad_handles[layer], tok = _push_start(late, [False] * len(late), d_gate, f"grads_start_{layer}")
        dh, st = _in_bwd(tie(dval, tok), dz, sv["w_in"], sv["h"], norm_g[layer].reshape(1, d), scale, dh, "in_bwd")
        d_mods[layer] = jnp.concatenate([st[0], st[1], d_gate[0]])
        d_norm_g[layer] = st[2]
    grad_x = dh.reshape(1, s, d)

    d_mods = jnp.stack(d_mods)
    dm_slots = jnp.concatenate(
        [d_mods.reshape(depth, N_DEV, wa).transpose(1, 0, 2).reshape(N_DEV, depth * wa), d_kv_mods.reshape(N_DEV, wk)],
        axis=1).reshape(N_DEV, 1, nm)
    small_parts = [d_mods, d_kv_mods, jnp.stack(d_norm_g), d_kv_norm_g, d_final_g, jnp.stack(d_rel), loss_part[0, :1]]
    n_small = sum(int(p.size) for p in small_parts)
    n_small_pad = -(-n_small // LANES) * LANES
    small = _pack(small_parts, n_small_pad).reshape(1, 1, n_small_pad)
    d_ascale_slots = jnp.stack(d_ascale, axis=1)
    s_a_scale, dm_cols, small_all = _exchange([d_ascale_slots, dm_slots, small], [False, False, True],
                                              "exchange_small")

    g_ada_w, g_kv_ada_w = _mods_bwd(c_all, dm_cols.reshape(N_DEV, nm), depth, wa, wk)
    res = {}
    res["ada_w"] = _adamw([g_ada_w[None]], ada_w, m_ada_w, v_ada_w, "adamw")
    res["kv_ada_w"] = _adamw([g_kv_ada_w[None]], kv_ada_w, m_kv_ada_w, v_kv_ada_w, "adamw")
    res["a_scale"] = _adamw([s_a_scale[:, l] for l in range(n_a)], a_scale, m_a_scale, v_a_scale, "adamw")
    after = res["ada_w"][1]
    s_b = [None] * n_b
    for layer in range(depth - 1, n_a - 1, -1):
        s_b[layer - n_a] = _push_wait(grad_handles[layer], after, f"grads_wait_{layer}")
        after = s_b[layer - n_a][0]
    res["b_w_in"] = _adamw([sb[0] for sb in s_b], b_w_in, m_b_w_in, v_b_w_in, "adamw")
    res["b_w_out"] = _adamw([sb[1] for sb in s_b], b_w_out, m_b_w_out, v_b_w_out, "adamw")
    s_w_kv, = _push_wait(grad_handles[depth], res["b_w_out"][1], "grads_wait_kv")
    res["w_kv"] = _adamw([s_w_kv], w_kv, m_w_kv, v_w_kv, "adamw")
    after = res["w_kv"][1]
    s_a = [None] * n_a
    for layer in range(n_a - 1, 0, -1):
        s_a[layer] = _push_wait(grad_handles[layer], after, f"grads_wait_{layer}")
        after = s_a[layer][0]
    s_group0, s_out0 = _push_wait(last_handle, after, "grads_wait_last")
    s_a[0] = [None, s_group0, s_out0]
    res["a_w_group"] = _adamw([sa[1] for sa in s_a], a_w_group, m_a_w_group, v_a_w_group, "adamw")
    res["a_w_out"] = _adamw([sa[2] for sa in s_a], a_w_out, m_a_w_out, v_a_w_out, "adamw")
    s_a[0][0], = _push_wait(grad_handles[0], res["a_w_out"][1], "grads_wait_0")
    res["a_w_in"] = _adamw([sa[0] for sa in s_a], a_w_in, m_a_w_in, v_a_w_in, "adamw")
    small_names = ["ada_b", "kv_ada_b", "norm_g", "kv_norm_g", "final_g", "b_rel_bias"]
    small_w = dict(ada_b=ada_b, kv_ada_b=kv_ada_b, norm_g=norm_g, kv_norm_g=kv_norm_g, final_g=final_g,
                   b_rel_bias=b_rel_bias)
    small_m = dict(ada_b=m_ada_b, kv_ada_b=m_kv_ada_b, norm_g=m_norm_g, kv_norm_g=m_kv_norm_g, final_g=m_final_g,
                   b_rel_bias=m_b_rel_bias)
    small_v = dict(ada_b=v_ada_b, kv_ada_b=v_kv_ada_b, norm_g=v_norm_g, kv_norm_g=v_kv_norm_g, final_g=v_final_g,
                   b_rel_bias=v_b_rel_bias)
    sw = _pack([small_w[n] for n in small_names], n_small_pad)
    smm = _pack([small_m[n] for n in small_names], n_small_pad)
    svv = _pack([small_v[n] for n in small_names] + [jnp.ones((n_small_pad - n_small + 1,), F32)], n_small_pad)
    small_out = _adamw([small_all.reshape(N_DEV, 1, n_small_pad)], sw, smm, svv, "adamw")
    off = 0
    for n in small_names:
        size = int(small_w[n].size)
        res[n] = [o[0, off:off + size].reshape(small_w[n].shape) for o in small_out]
        off += size
    loss = small_out[0][0, n_small - 1]

    order = ["ada_w", "ada_b", "norm_g", "a_w_in", "a_w_group", "a_scale", "a_w_out", "kv_norm_g", "kv_ada_w",
             "kv_ada_b", "w_kv", "b_w_in", "b_rel_bias", "b_w_out", "final_g"]
    outs = [loss, grad_x]
    for part in range(4):
        outs += [res[n][part] for n in order]
    return tuple(outs)
```

```python
import functools

import jax
import jax.numpy as jnp
from jax import lax
from jax.experimental import pallas as pl
from jax.experimental.pallas import tpu as pltpu

F32 = jnp.float32
BF16 = jnp.bfloat16

N_DEV = 8
CHUNK = 64
LEFT_CHUNKS = 8
N_HEADS = 16
POOL_WINDOWS = (2, 4, 8, 16)
REL_CLIP = 128
EPS = 1e-6
ADAM_LR = 0.001
ADAM_B1 = 0.9
ADAM_B2 = 0.999
ADAM_EPS = 1e-08
ADAM_WD = 0.01
ADAM_STEP = 10

LANES = 128
TQ = 2 * CHUNK
LEFT = LEFT_CHUNKS * CHUNK
BAND = LEFT + TQ
FRAME_PAD = BAND + TQ
NFRAME = LEFT // TQ + 1
POOL_HALO = 16
REL_PAD = 384
NEG = -1e30
ATTN_GROUP_FWD = 64
ATTN_GROUP = 16
ATTN_BATCH = 4
VMEM_LIMIT = 56 * 1024 * 1024

NT = (((1,), (1,)), ((), ()))
TN = (((0,), (0,)), ((), ()))

HBM_SPEC = pl.BlockSpec(memory_space=pltpu.HBM)
ANY_SPEC = pl.BlockSpec(memory_space=pl.ANY)
SEM_SPEC = pl.BlockSpec(memory_space=pltpu.SEMAPHORE)


def _cp(*sem):
    return pltpu.CompilerParams(dimension_semantics=sem or None, vmem_limit_bytes=VMEM_LIMIT)


def _sds(shape, dtype):
    return jax.ShapeDtypeStruct(tuple(shape), dtype)


def _row_tile(s, want):
    return min(want, s)


def _vec(d):
    return pl.BlockSpec((1, d), lambda *_: (0, 0))


def _sigmoid(z):
    return 1.0 / (1.0 + jnp.exp(-z))


def _exchange(xs, gather, name):
    n = len(xs)
    out_shapes = []
    for x, ga in zip(xs, gather):
        out_shapes.append(_sds((N_DEV,) + (x.shape if ga else x.shape[1:]), x.dtype))

    def body(*refs):
        ins, outs = refs[:n], refs[n:2 * n]
        send_sems, recv_sems, local_sems = refs[2 * n:]
        mx, my, mc = lax.axis_index("x"), lax.axis_index("y"), lax.axis_index("c")
        me = 4 * mx + 2 * my + mc
        local = []
        for k in range(n):
            src = ins[k] if gather[k] else ins[k].at[me]
            cp = pltpu.make_async_copy(src, outs[k].at[me], local_sems.at[k])
            cp.start()
            local.append(cp)
        sends, recvs = [], []
        for r in range(1, N_DEV):
            px, py, pc = (mx + (r >> 2)) % 2, (my + ((r >> 1) & 1)) % 2, (mc + (r & 1)) % 2
            peer = 4 * px + 2 * py + pc
            for k in range(n):
                sem = k * (N_DEV - 1) + r - 1
                src = ins[k] if gather[k] else ins[k].at[peer]
                send = pltpu.make_async_remote_copy(
                    src_ref=src, dst_ref=outs[k].at[me], send_sem=send_sems.at[sem], recv_sem=recv_sems.at[sem],
                    device_id=(px, py, pc), device_id_type=pl.DeviceIdType.MESH)
                send.start()
                sends.append(send)
                recvs.append(pltpu.make_async_remote_copy(
                    src_ref=src, dst_ref=outs[k].at[peer], send_sem=send_sems.at[sem], recv_sem=recv_sems.at[sem],
                    device_id=(px, py, pc), device_id_type=pl.DeviceIdType.MESH))
        for cp in recvs:
            cp.wait_recv()
        for cp in sends:
            cp.wait_send()
        for cp in local:
            cp.wait()

    return pl.pallas_call(
        body, name=name, out_shape=out_shapes,
        in_specs=[HBM_SPEC] * n, out_specs=[HBM_SPEC] * n,
        scratch_shapes=[pltpu.SemaphoreType.DMA((n * (N_DEV - 1),)), pltpu.SemaphoreType.DMA((n * (N_DEV - 1),)),
                        pltpu.SemaphoreType.DMA((n,))],
    )(*xs)


def _peer(mx, my, mc, r):
    px, py, pc = (mx + (r >> 2)) % 2, (my + ((r >> 1) & 1)) % 2, (mc + (r & 1)) % 2
    return (px, py, pc), 4 * px + 2 * py + pc


def _push_start(xs, gather, dep, name):
    n = len(xs)
    nsem = n * (N_DEV - 1)
    me = 4 * lax.axis_index("x") + 2 * lax.axis_index("y") + lax.axis_index("c")
    lands = []
    for x, ga in zip(xs, gather):
        own = x[None] if ga else lax.dynamic_index_in_dim(x, me, 0, keepdims=True)
        empty = lax.empty((N_DEV,) + own.shape[1:], x.dtype)
        lands.append(lax.dynamic_update_slice(empty, own, (me,) + (0,) * (own.ndim - 1)))

    def body(*refs):
        ins, lands_in = refs[:n], refs[n:2 * n]
        send_sems, recv_sems = refs[2 * n + 1], refs[2 * n + 2]
        token = refs[-1]
        mx, my, mc = lax.axis_index("x"), lax.axis_index("y"), lax.axis_index("c")
        mine = 4 * mx + 2 * my + mc
        for k in range(n):
            for r in range(1, N_DEV):
                dev, peer = _peer(mx, my, mc, r)
                sem = k * (N_DEV - 1) + r - 1
                pltpu.make_async_remote_copy(
                    src_ref=ins[k] if gather[k] else ins[k].at[peer], dst_ref=lands_in[k].at[mine],
                    send_sem=send_sems.at[sem], recv_sem=recv_sems.at[sem],
                    device_id=dev, device_id_type=pl.DeviceIdType.MESH).start()
        token[...] = jnp.zeros_like(token)

    hbm = lambda a: pltpu.HBM(a.shape, a.dtype)
    outs = pl.pallas_call(
        body, name=name,
        out_shape=(pltpu.SemaphoreType.DMA((nsem,)), pltpu.SemaphoreType.DMA((nsem,)),
                   *[hbm(x) for x in xs], *[hbm(a) for a in lands], _sds((8, LANES), F32)),
        in_specs=[HBM_SPEC] * (2 * n) + [ANY_SPEC],
        out_specs=(SEM_SPEC, SEM_SPEC, *[HBM_SPEC] * (2 * n), pl.BlockSpec(memory_space=pltpu.VMEM)),
        input_output_aliases={k: 2 + k for k in range(2 * n)},
        compiler_params=pltpu.CompilerParams(has_side_effects=pltpu.SideEffectType.DATAFLOW_SIDE_EFFECTING),
    )(*[pltpu.with_memory_space_constraint(x, pltpu.HBM) for x in xs],
      *[pltpu.with_memory_space_constraint(a, pltpu.HBM) for a in lands], dep)
    return (outs[0], outs[1], outs[2:2 + n], outs[2 + n:2 + 2 * n], tuple(gather)), outs[-1]


def _push_wait(handle, after, name):
    send_sems, recv_sems, srcs, lands, gather = handle
    n = len(srcs)

    def body(*refs):
        ins, lands_in = refs[:n], refs[n:2 * n]
        send_sems, recv_sems = refs[2 * n], refs[2 * n + 1]
        mx, my, mc = lax.axis_index("x"), lax.axis_index("y"), lax.axis_index("c")
        for k in range(n):
            for r in range(1, N_DEV):
                dev, peer = _peer(mx, my, mc, r)
                sem = k * (N_DEV - 1) + r - 1
                cp = pltpu.make_async_remote_copy(
                    src_ref=ins[k] if gather[k] else ins[k].at[peer], dst_ref=lands_in[k].at[peer],
                    send_sem=send_sems.at[sem], recv_sem=recv_sems.at[sem],
                    device_id=dev, device_id_type=pl.DeviceIdType.MESH)
                cp.wait_send()
                cp.wait_recv()

    hbm = lambda a: pltpu.HBM(a.shape, a.dtype)
    outs = pl.pallas_call(
        body, name=name,
        out_shape=(*[hbm(x) for x in srcs], *[hbm(a) for a in lands]),
        in_specs=[HBM_SPEC] * (2 * n) + [SEM_SPEC, SEM_SPEC, ANY_SPEC],
        out_specs=tuple([HBM_SPEC] * (2 * n)),
        input_output_aliases={k: k for k in range(2 * n)},
        compiler_params=pltpu.CompilerParams(has_side_effects=pltpu.SideEffectType.DATAFLOW_SIDE_EFFECTING),
    )(*srcs, *lands, send_sems, recv_sems, after)
    return list(outs[n:])


def _mods_fwd(c_all, ada_w, ada_b_loc, kv_ada_w, kv_ada_b_loc):
    nl, d, wa = ada_w.shape
    wk = kv_ada_w.shape[1]

    def body(c_ref, w_ref, b_ref, kw_ref, kb_ref, o_ref):
        c = c_ref[...]
        ca = c * _sigmoid(c)
        for l in range(nl):
            o_ref[:, l * wa:(l + 1) * wa] = jnp.dot(
                ca, w_ref[l], preferred_element_type=F32, precision=lax.Precision.HIGHEST) + b_ref[l]
        o_ref[:, nl * wa:] = jnp.dot(
            ca, kw_ref[...], preferred_element_type=F32, precision=lax.Precision.HIGHEST) + kb_ref[...]

    return pl.pallas_call(body, name="mods_fwd", out_shape=_sds((N_DEV, nl * wa + wk), F32),
                          compiler_params=_cp())(c_all, ada_w, ada_b_loc, kv_ada_w, kv_ada_b_loc)


def _mods_bwd(c_all, dm, nl, wa, wk):
    d = c_all.shape[1]

    def body(c_ref, d_ref, gw_ref, gk_ref):
        c = c_ref[...]
        ca = c * _sigmoid(c)
        for l in range(nl):
            gw_ref[l] = lax.dot_general(ca, d_ref[:, l * wa:(l + 1) * wa], TN,
                                        preferred_element_type=F32, precision=lax.Precision.HIGHEST)
        gk_ref[...] = lax.dot_general(ca, d_ref[:, nl * wa:], TN,
                                      preferred_element_type=F32, precision=lax.Precision.HIGHEST)

    return pl.pallas_call(body, name="mods_bwd", out_shape=[_sds((nl, d, wa), F32), _sds((d, wk), F32)],
                          compiler_params=_cp())(c_all, dm)


def _norm_mod(hf, g, shift, scale):
    rs = lax.rsqrt(jnp.mean(hf * hf, axis=-1, keepdims=True) + EPS)
    xhat = hf * rs
    n = xhat * g
    return rs, xhat, n, n * (1.0 + scale) + shift


def _inproj_fwd(h, g, shift, scale, w, name, pad=0):
    s, d = h.shape
    nj, _, wn = w.shape
    half = nj // 2
    tm = _row_tile(s, 512)
    assert pad in (0, tm)
    pb = pad // tm

    def body(h_ref, g_ref, sh_ref, sc_ref, w_ref, u_ref, l_ref, r_ref):
        _, _, _, u = _norm_mod(h_ref[...], g_ref[...], sh_ref[...], sc_ref[...])
        ub = u.astype(BF16)
        u_ref[...] = ub
        for j in range(nj):
            o_ref = l_ref if j < half else r_ref
            jj = j % half
            o_ref[:, jj * wn:(jj + 1) * wn] = jnp.dot(ub, w_ref[j], preferred_element_type=F32).astype(BF16)
        if pb:
            @pl.when(pl.program_id(0) == 0)
            def _():
                l_ref[...] = jnp.zeros_like(l_ref)
                r_ref[...] = jnp.zeros_like(r_ref)

    e = half * wn
    src = lambda i: (jnp.maximum(i - pb, 0), 0)
    return pl.pallas_call(
        body, name=name, grid=(s // tm + pb,),
        in_specs=[pl.BlockSpec((tm, d), src), _vec(d), _vec(d), _vec(d),
                  pl.BlockSpec((nj, d, wn), lambda i: (0, 0, 0))],
        out_specs=[pl.BlockSpec((tm, d), src), pl.BlockSpec((tm, e), lambda i: (i, 0)),
                   pl.BlockSpec((tm, e), lambda i: (i, 0))],
        out_shape=[_sds((s, d), BF16), _sds((s + pad, e), BF16), _sds((s + pad, e), BF16)],
        compiler_params=_cp("arbitrary"),
    )(h, g, shift, scale, w)


def _pool_fwd(val, wg):
    s, e = val.shape
    ng = len(POOL_WINDOWS)
    gw = e // ng
    tm = _row_tile(s, 512)
    hb = tm // POOL_HALO

    def body(v_ref, halo_ref, w_ref, p_ref, m_ref):
        i = pl.program_id(0)
        t = i * tm + lax.broadcasted_iota(jnp.int32, (tm, 1), 0)
        for g, wdw in enumerate(POOL_WINDOWS):
            cols = slice(g * gw, (g + 1) * gw)
            v = v_ref[:, cols].astype(F32)
            halo = jnp.where(i > 0, halo_ref[:, cols].astype(F32), 0.0)
            acc = jnp.concatenate([halo, v], axis=0)
            sh = 1
            while sh < wdw:
                acc = acc + pltpu.roll(acc, sh, 0)
                sh *= 2
            cnt = jnp.minimum(t + 1, wdw).astype(F32)
            pb = (acc[POOL_HALO:, :] / cnt - v).astype(BF16)
            p_ref[:, cols] = pb
            m_ref[:, cols] = jnp.dot(pb, w_ref[:, g].reshape(gw, gw), preferred_element_type=F32).astype(BF16)

    return pl.pallas_call(
        body, name="pool_fwd", grid=(s // tm,),
        in_specs=[pl.BlockSpec((tm, e), lambda i: (i, 0)),
                  pl.BlockSpec((POOL_HALO, e), lambda i: (jnp.maximum(i * hb - 1, 0), 0)),
                  pl.BlockSpec((N_DEV, ng, gw // N_DEV, gw), lambda i: (0, 0, 0, 0))],
        out_specs=[pl.BlockSpec((tm, e), lambda i: (i, 0)), pl.BlockSpec((tm, e), lambda i: (i, 0))],
        out_shape=[_sds((s, e), BF16), _sds((s, e), BF16)],
        compiler_params=_cp("arbitrary"),
    )(val, val, wg)


def _gate_out_fwd(m, z, ascale, w, h, gate, name):
    s, e = m.shape
    d = h.shape[1]
    tm = _row_tile(s, 512)

    def body(m_ref, z_ref, a_ref, w_ref, h_ref, g_ref, gd_ref, y_ref, ho_ref):
        z = z_ref[...].astype(F32)
        gb = ((m_ref[...].astype(F32) * a_ref[...]) * (z * _sigmoid(z))).astype(BF16)
        gd_ref[...] = gb
        y = jnp.dot(gb, w_ref[...].reshape(e, d), preferred_element_type=F32)
        y_ref[...] = y.astype(BF16)
        ho_ref[...] = h_ref[...] + g_ref[...] * y

    return pl.pallas_call(
        body, name=name, grid=(s // tm,),
        in_specs=[pl.BlockSpec((tm, e), lambda i: (i, 0)), pl.BlockSpec((tm, e), lambda i: (i, 0)), _vec(e),
                  pl.BlockSpec((N_DEV, e // N_DEV, d), lambda i: (0, 0, 0)),
                  pl.BlockSpec((tm, d), lambda i: (i, 0)), _vec(d)],
        out_specs=[pl.BlockSpec((tm, e), lambda i: (i, 0)), pl.BlockSpec((tm, d), lambda i: (i, 0)),
                   pl.BlockSpec((tm, d), lambda i: (i, 0))],
        out_shape=[_sds((s, e), BF16), _sds((s, d), BF16), _sds((s, d), F32)],
        compiler_params=_cp("arbitrary"),
    )(m, z, ascale, w, h, gate)


def _rel_onehot(shape, r_axis):
    r = lax.broadcasted_iota(jnp.int32, shape, r_axis)
    j = lax.broadcasted_iota(jnp.int32, shape, 1 - r_axis)
    dist = LEFT - (j - TQ)
    return (jnp.clip(dist, -REL_CLIP, REL_CLIP) + REL_CLIP == r).astype(F32)


def _skew(x, sign):
    row = lax.broadcasted_iota(jnp.int32, x.shape, 0)
    for b in range(TQ.bit_length() - 1):
        amt = (1 << b) if sign > 0 else FRAME_PAD - (1 << b)
        x = jnp.where(((row >> b) & 1) == 1, pltpu.roll(x, amt, 1), x)
    return x


def _bias_frames(rel):
    nh = rel.shape[0]

    def body(r_ref, f_ref, ft_ref):
        fext = jnp.dot(r_ref[...], _rel_onehot((REL_PAD, FRAME_PAD), 0), preferred_element_type=F32,
                       precision=lax.Precision.HIGHEST)
        x = _skew(jnp.broadcast_to(fext, (TQ, FRAME_PAD)), 1)[:, TQ:]
        qc = lax.broadcasted_iota(jnp.int32, (TQ, BAND), 0) // CHUNK
        m = lax.broadcasted_iota(jnp.int32, (TQ, BAND), 1)
        mc = m // CHUNK
        x = jnp.where((mc >= qc) & (mc <= qc + LEFT_CHUNKS), x, NEG)
        for f in range(NFRAME):
            xf = jnp.where(m >= LEFT - f * TQ, x, NEG)
            f_ref[f] = xf
            ft_ref[f] = xf.T

    return pl.pallas_call(
        body, name="bias_frames", grid=(nh,),
        in_specs=[pl.BlockSpec((None, 1, REL_PAD), lambda h: (h, 0, 0))],
        out_specs=[pl.BlockSpec((None, NFRAME, TQ, BAND), lambda h: (h, 0, 0, 0)),
                   pl.BlockSpec((None, NFRAME, BAND, TQ), lambda h: (h, 0, 0, 0))],
        out_shape=[_sds((nh, NFRAME, TQ, BAND), F32), _sds((nh, NFRAME, BAND, TQ), F32)],
        compiler_params=_cp("arbitrary"),
    )(rel)


def _bias_bwd(dft):
    nh = dft.shape[0]

    def body(d_ref, o_ref):
        d = d_ref[0]
        for f in range(1, NFRAME):
            d = d + d_ref[f]
        x = jnp.concatenate([jnp.zeros((TQ, TQ), F32), d.T], axis=1)
        col = jnp.sum(_skew(x, -1), axis=0, keepdims=True)
        o_ref[...] = jnp.dot(col, _rel_onehot((FRAME_PAD, REL_PAD), 1), preferred_element_type=F32,
                             precision=lax.Precision.HIGHEST)

    return pl.pallas_call(
        body, name="bias_bwd", grid=(nh,),
        in_specs=[pl.BlockSpec((None, NFRAME, BAND, TQ), lambda h: (h, 0, 0, 0))],
        out_specs=pl.BlockSpec((None, 1, REL_PAD), lambda h: (h, 0, 0)),
        out_shape=_sds((nh, 1, REL_PAD), F32),
        compiler_params=_cp("arbitrary"),
    )(dft)


def _attn_tiles(s, group):
    nt = s // TQ
    ni = min(group, nt)
    return nt // ni, ni


def _attn_fwd(q, k, v, frame):
    s, e = q.shape
    dh = e // N_HEADS
    ng, ni = _attn_tiles(s, ATTN_GROUP_FWD)
    sm = dh ** -0.5
    nb = min(ATTN_BATCH, ni)

    def body(q_ref, k_ref, v_ref, b_ref, o_ref):
        g0 = pl.program_id(1) * ni
        for tb in range(0, ni, nb):
            ts = range(tb, tb + nb)
            keys = [pl.ds(pl.multiple_of((g0 + t) * TQ, TQ), BAND) for t in ts]
            rows = [pl.ds(t * TQ, TQ) for t in ts]
            scs = [lax.dot_general(q_ref[r, :], k_ref[kk, :], NT, preferred_element_type=F32)
                   for r, kk in zip(rows, keys)]
            ps, ls = [], []
            for t, sc in zip(ts, scs):
                sc = sc * sm + b_ref[jnp.minimum(g0 + t, NFRAME - 1)]
                p = jnp.exp(sc - jnp.max(sc, axis=-1, keepdims=True))
                ls.append(jnp.sum(p, axis=-1, keepdims=True))
                ps.append(p.astype(BF16))
            for r, p, l, kk in zip(rows, ps, ls, keys):
                o = jnp.dot(p, v_ref[kk, :], preferred_element_type=F32)
                o_ref[r, :] = (o / l).astype(BF16)

    return pl.pallas_call(
        body, name="attn_fwd", grid=(N_HEADS, ng),
        in_specs=[pl.BlockSpec((ni * TQ, dh), lambda h, i: (i, h)),
                  pl.BlockSpec((s + LEFT, dh), lambda h, i: (0, h)), pl.BlockSpec((s + LEFT, dh), lambda h, i: (0, h)),
                  pl.BlockSpec((None, NFRAME, TQ, BAND), lambda h, i: (h, 0, 0, 0))],
        out_specs=pl.BlockSpec((ni * TQ, dh), lambda h, i: (i, h)),
        out_shape=_sds((s, e), BF16),
        compiler_params=_cp("arbitrary", "arbitrary"),
    )(q, k, v, frame)


def _final_loss(h, g, target):
    s, d = h.shape
    tm = _row_tile(s, 512)

    def body(h_ref, g_ref, t_ref, dh_ref, dg_ref, l_ref):
        @pl.when(pl.program_id(0) == 0)
        def _():
            dg_ref[...] = jnp.zeros_like(dg_ref)
            l_ref[...] = jnp.zeros_like(l_ref)

        hf = h_ref[...]
        gg = g_ref[...]
        rs = lax.rsqrt(jnp.mean(hf * hf, axis=-1, keepdims=True) + EPS)
        xhat = hf * rs
        diff = xhat * gg - t_ref[...]
        l_ref[...] += 0.5 * jnp.sum(jnp.mean(diff * diff, axis=-1, keepdims=True), axis=0, keepdims=True)
        dout = diff * (1.0 / d)
        dg_ref[...] += jnp.sum(dout * xhat, axis=0, keepdims=True)
        dxh = dout * gg
        dh_ref[...] = rs * (dxh - xhat * jnp.mean(dxh * xhat, axis=-1, keepdims=True))

    return pl.pallas_call(
        body, name="final_loss", grid=(s // tm,),
        in_specs=[pl.BlockSpec((tm, d), lambda i: (i, 0)), _vec(d), pl.BlockSpec((tm, d), lambda i: (i, 0))],
        out_specs=[pl.BlockSpec((tm, d), lambda i: (i, 0)), _vec(d), _vec(LANES)],
        out_shape=[_sds((s, d), F32), _sds((1, d), F32), _sds((1, LANES), F32)],
        compiler_params=_cp("arbitrary"),
    )(h, g, target)


def _out_bwd(dh, y, gate, w, m, z, ascale, name):
    s, d = dh.shape
    e = m.shape[1]
    tm = _row_tile(s, 512)

    def body(dh_ref, y_ref, g_ref, w_ref, m_ref, z_ref, a_ref, dy_ref, dm_ref, dz_ref, dg_ref, da_ref):
        @pl.when(pl.program_id(0) == 0)
        def _():
            dg_ref[...] = jnp.zeros_like(dg_ref)
            da_ref[...] = jnp.zeros_like(da_ref)

        dhf = dh_ref[...]
        dg_ref[...] += jnp.sum(dhf * y_ref[...].astype(F32), axis=0, keepdims=True)
        dyb = (g_ref[...] * dhf).astype(BF16)
        dy_ref[...] = dyb
        dgated = lax.dot_general(dyb, w_ref[...].reshape(e, d), NT, preferred_element_type=F32)
        z = z_ref[...].astype(F32)
        sig = _sigmoid(z)
        mf = m_ref[...].astype(F32)
        a = a_ref[...]
        dms = dgated * (z * sig)
        da_ref[...] += jnp.sum(dms * mf, axis=0, keepdims=True)
        dm_ref[...] = (dms * a).astype(BF16)
        dz_ref[...] = (dgated * (mf * a) * (sig * (1.0 + z * (1.0 - sig)))).astype(BF16)

    return pl.pallas_call(
        body, name=name, grid=(s // tm,),
        in_specs=[pl.BlockSpec((tm, d), lambda i: (i, 0)), pl.BlockSpec((tm, d), lambda i: (i, 0)), _vec(d),
                  pl.BlockSpec((N_DEV, e // N_DEV, d), lambda i: (0, 0, 0)),
                  pl.BlockSpec((tm, e), lambda i: (i, 0)), pl.BlockSpec((tm, e), lambda i: (i, 0)), _vec(e)],
        out_specs=[pl.BlockSpec((tm, d), lambda i: (i, 0)), pl.BlockSpec((tm, e), lambda i: (i, 0)),
                   pl.BlockSpec((tm, e), lambda i: (i, 0)), _vec(d), _vec(e)],
        out_shape=[_sds((s, d), BF16), _sds((s, e), BF16), _sds((s, e), BF16), _sds((1, d), F32), _sds((1, e), F32)],
        compiler_params=_cp("arbitrary"),
    )(dh, y, gate, w, m, z, ascale)


def _pool_bwd(dm, wg):
    s, e = dm.shape
    ng = len(POOL_WINDOWS)
    gw = e // ng
    tm = _row_tile(s, 512)
    hb = tm // POOL_HALO
    nsteps = s // tm

    def body(d_ref, halo_ref, w_ref, o_ref):
        i = pl.program_id(0)
        t = i * tm + lax.broadcasted_iota(jnp.int32, (tm + POOL_HALO, 1), 0)
        for g, wdw in enumerate(POOL_WINDOWS):
            cols = slice(g * gw, (g + 1) * gw)
            dmx = jnp.concatenate([d_ref[:, cols], halo_ref[:, cols]], axis=0)
            dp = lax.dot_general(dmx, w_ref[:, g].reshape(gw, gw), NT, preferred_element_type=F32)
            dp = jnp.where(t < s, dp, 0.0)
            acc = dp / jnp.minimum(t + 1, wdw).astype(F32)
            sh = 1
            while sh < wdw:
                acc = acc + pltpu.roll(acc, tm + POOL_HALO - sh, 0)
                sh *= 2
            o_ref[:, cols] = (acc[:tm, :] - dp[:tm, :]).astype(BF16)

    return pl.pallas_call(
        body, name="pool_bwd", grid=(nsteps,),
        in_specs=[pl.BlockSpec((tm, e), lambda i: (i, 0)),
                  pl.BlockSpec((POOL_HALO, e), lambda i: (jnp.minimum((i + 1) * hb, s // POOL_HALO - 1), 0)),
                  pl.BlockSpec((N_DEV, ng, gw // N_DEV, gw), lambda i: (0, 0, 0, 0))],
        out_specs=pl.BlockSpec((tm, e), lambda i: (i, 0)),
        out_shape=_sds((s, e), BF16),
        compiler_params=_cp("arbitrary"),
    )(dm, dm, wg)


def _attn_bwd(q, k, v, do, frame_t, prev=None):
    s, e = q.shape
    dh = e // N_HEADS
    ng, ni = _attn_tiles(s, ATTN_GROUP)
    sm = dh ** -0.5
    nleft = LEFT // TQ

    def core(q_ref, k_ref, v_ref, do_ref, b_ref, dq_ref, dk_ref, dv_ref, db_ref, init):
        i = pl.program_id(1)
        g0 = i * ni

        @pl.when(i == 0)
        def _():
            init()
            db_ref[...] = jnp.zeros_like(db_ref)

        keys = [pl.ds(pl.multiple_of((g0 + t) * TQ, TQ), BAND) for t in range(ni)]
        rows = [pl.ds(t * TQ, TQ) for t in range(ni)]
        pbs, dsbs, dsts = [], [], []
        nb = min(ATTN_BATCH, ni)
        for tb in range(0, ni, nb):
            sts = [lax.dot_general(k_ref[keys[t], :], q_ref[rows[t], :], NT, preferred_element_type=F32)
                   for t in range(tb, tb + nb)]
            dpts = [lax.dot_general(v_ref[keys[t], :], do_ref[rows[t], :], NT, preferred_element_type=F32)
                    for t in range(tb, tb + nb)]
            for t, st, dpt in zip(range(tb, tb + nb), sts, dpts):
                st = st * sm + b_ref[jnp.minimum(g0 + t, NFRAME - 1)]
                p = jnp.exp(st - jnp.max(st, axis=0, keepdims=True))
                p = p * (1.0 / jnp.sum(p, axis=0, keepdims=True))
                dst = p * (dpt - jnp.sum(dpt * p, axis=0, keepdims=True))
                pbs.append(p.astype(BF16))
                dsbs.append(dst.astype(BF16))
                dsts.append(dst)
        for t in range(min(ni, nleft)):
            db_ref[jnp.minimum(g0 + t, NFRAME - 1)] += dsts[t]
        if ni > nleft:
            rest = dsts[nleft]
            for t in range(nleft + 1, ni):
                rest = rest + dsts[t]
            db_ref[NFRAME - 1] += rest
        for t in range(ni):
            dq = lax.dot_general(dsbs[t], k_ref[keys[t], :], TN, preferred_element_type=F32)
            dq_ref[rows[t], :] = (dq * sm).astype(BF16)
        for r in range(ni + nleft):
            ts = [t for t in range(ni) if 0 <= r - t <= nleft]
            blk = lambda xs: jnp.concatenate([xs[t][(r - t) * TQ:(r - t + 1) * TQ, :] for t in ts], axis=1)
            qrows = slice(ts[0] * TQ, (ts[-1] + 1) * TQ)
            krows = pl.ds(pl.multiple_of((g0 + r) * TQ, TQ), TQ)
            dk_ref[krows, :] += jnp.dot(blk(dsbs), q_ref[qrows, :], preferred_element_type=F32) * sm
            dv_ref[krows, :] += jnp.dot(blk(pbs), do_ref[qrows, :], preferred_element_type=F32)

    tile_spec = pl.BlockSpec((ni * TQ, dh), lambda h, i: (i, h))
    kv_spec = pl.BlockSpec((s + LEFT, dh), lambda h, i: (0, h))
    fr_spec = pl.BlockSpec((None, NFRAME, BAND, TQ), lambda h, i: (h, 0, 0, 0))
    fr_shape = _sds((N_HEADS, NFRAME, BAND, TQ), F32)
    if prev is None:
        def body(q_ref, k_ref, v_ref, do_ref, b_ref, dq_ref, dk_ref, dv_ref, db_ref):
            def init():
                dk_ref[...] = jnp.zeros_like(dk_ref)
                dv_ref[...] = jnp.zeros_like(dv_ref)
            core(q_ref, k_ref, v_ref, do_ref, b_ref, dq_ref, dk_ref, dv_ref, db_ref, init)

        return pl.pallas_call(
            body, name="attn_bwd", grid=(N_HEADS, ng),
            in_specs=[tile_spec, kv_spec, kv_spec, tile_spec, fr_spec],
            out_specs=[tile_spec, kv_spec, kv_spec, fr_spec],
            out_shape=[_sds((s, e), BF16), _sds((s + LEFT, e), F32), _sds((s + LEFT, e), F32), fr_shape],
            compiler_params=_cp("arbitrary", "arbitrary"),
        )(q, k, v, do, frame_t)

    def body_acc(q_ref, k_ref, v_ref, do_ref, b_ref, dkp_hbm, dvp_hbm, dq_ref, dkb_ref, dvb_ref, db_ref,
                 dk_acc, dv_acc, dkp_buf, dvp_buf, sems):
        cols = pl.ds(pl.multiple_of(pl.program_id(0) * dh, dh), dh)
        fetch = [pltpu.make_async_copy(dkp_hbm.at[:, cols], dkp_buf, sems.at[0]),
                 pltpu.make_async_copy(dvp_hbm.at[:, cols], dvp_buf, sems.at[1])]

        def init():
            for cp in fetch:
                cp.start()
            dk_acc[...] = jnp.zeros_like(dk_acc)
            dv_acc[...] = jnp.zeros_like(dv_acc)
        core(q_ref, k_ref, v_ref, do_ref, b_ref, dq_ref, dk_acc, dv_acc, db_ref, init)

        @pl.when(pl.program_id(1) == ng - 1)
        def _():
            for cp in fetch:
                cp.wait()
            dkb_ref[...] = (dk_acc[LEFT:, :] + dkp_buf[LEFT:, :]).astype(BF16)
            dvb_ref[...] = (dv_acc[LEFT:, :] + dvp_buf[LEFT:, :]).astype(BF16)

    out_kv = pl.BlockSpec((s, dh), lambda h, i: (0, h))
    acc = pltpu.VMEM((s + LEFT, dh), F32)
    return pl.pallas_call(
        body_acc, name="attn_bwd_acc", grid=(N_HEADS, ng),
        in_specs=[tile_spec, kv_spec, kv_spec, tile_spec, fr_spec, ANY_SPEC, ANY_SPEC],
        out_specs=[tile_spec, out_kv, out_kv, fr_spec],
        out_shape=[_sds((s, e), BF16), _sds((s, e), BF16), _sds((s, e), BF16), fr_shape],
        scratch_shapes=[acc, acc, acc, acc, pltpu.SemaphoreType.DMA((2,))],
        compiler_params=_cp("arbitrary", "arbitrary"),
    )(q, k, v, do, frame_t, *prev)


def _in_bwd(dl, dr, w, h, g, scale, dres, name):
    s, d = h.shape
    nj, _, wn = w.shape
    half = nj // 2
    e = half * wn
    tm = _row_tile(s, 512)

    def body(dl_ref, dr_ref, w_ref, h_ref, g_ref, sc_ref, res_ref, dh_ref, st_ref):
        @pl.when(pl.program_id(0) == 0)
        def _():
            st_ref[...] = jnp.zeros_like(st_ref)

        du = jnp.zeros((tm, d), F32)
        for j in range(nj):
            src = dl_ref if j < half else dr_ref
            jj = j % half
            du = du + lax.dot_general(src[:, jj * wn:(jj + 1) * wn], w_ref[j], NT, preferred_element_type=F32)
        gg = g_ref[...]
        rs, xhat, n, _ = _norm_mod(h_ref[...], gg, 0.0, 0.0)
        dn = du * (1.0 + sc_ref[...])
        st_ref[0:1, :] += jnp.sum(du, axis=0, keepdims=True)
        st_ref[1:2, :] += jnp.sum(du * n, axis=0, keepdims=True)
        st_ref[2:3, :] += jnp.sum(dn * xhat, axis=0, keepdims=True)
        dxh = dn * gg
        dh_ref[...] = rs * (dxh - xhat * jnp.mean(dxh * xhat, axis=-1, keepdims=True)) + res_ref[...]

    return pl.pallas_call(
        body, name=name, grid=(s // tm,),
        in_specs=[pl.BlockSpec((tm, e), lambda i: (i, 0)), pl.BlockSpec((tm, e), lambda i: (i, 0)),
                  pl.BlockSpec((nj, d, wn), lambda i: (0, 0, 0)),
                  pl.BlockSpec((tm, d), lambda i: (i, 0)), _vec(d), _vec(d), pl.BlockSpec((tm, d), lambda i: (i, 0))],
        out_specs=[pl.BlockSpec((tm, d), lambda i: (i, 0)), pl.BlockSpec((8, d), lambda i: (0, 0))],
        out_shape=[_sds((s, d), F32), _sds((8, d), F32)],
        compiler_params=_cp("arbitrary"),
    )(dl, dr, w, h, g, scale, dres)


def _tn_matmul(x, ys, xw, yw, ymap, nb, out_shape, stage_shape, out_at, name):
    s, xfull = x.shape
    ts = _row_tile(s, 2048)
    ys = list(ys)
    half = nb // len(ys)
    nk = s // ts

    def body(*refs):
        x_ref, y_refs = refs[0], refs[1:1 + len(ys)]
        o_hbm, xt_ref, acc_ref, stage_ref, sem = refs[1 + len(ys):]
        k, b = pl.program_id(0), pl.program_id(1)

        @pl.when(b == 0)
        def _():
            xt_ref[...] = x_ref[...].T

        for n, y_ref in enumerate(y_refs):
            @pl.when((b >= n * half) & (b < (n + 1) * half))
            def _():
                xt = xt_ref[...] if xw == xfull else xt_ref[pl.ds(pl.multiple_of(b * xw, xw), xw), :]
                part = jnp.dot(xt, y_ref[...], preferred_element_type=F32)

                @pl.when(k == 0)
                def _():
                    acc_ref[b] = part

                @pl.when(k > 0)
                def _():
                    acc_ref[b] += part

        @pl.when(k == nk - 1)
        def _():
            stage_ref[...] = acc_ref[b].astype(BF16).reshape(stage_shape)
            cp = pltpu.make_async_copy(stage_ref, out_at(o_hbm, b), sem)
            cp.start()
            cp.wait()

    in_specs = [pl.BlockSpec((ts, xfull), lambda k, b: (k, 0))]
    for n in range(len(ys)):
        in_specs.append(pl.BlockSpec(
            (ts, yw), lambda k, b, n=n: (k, ymap(jnp.clip(b - n * half, 0, half - 1)))))
    return pl.pallas_call(
        body, name=name, grid=(nk, nb), in_specs=in_specs,
        out_specs=HBM_SPEC, out_shape=_sds(out_shape, BF16),
        scratch_shapes=[pltpu.VMEM((xfull, ts), BF16), pltpu.VMEM((nb, xw, yw), F32), pltpu.VMEM(stage_shape, BF16),
                        pltpu.SemaphoreType.DMA(())],
        compiler_params=_cp("arbitrary", "arbitrary"),
    )(x, *ys)


def _grad_w_in(u, dl, dr, name):
    d = u.shape[1]
    wn = 2 * dl.shape[1] // N_DEV
    return _tn_matmul(u, (dl, dr), d, wn, lambda b: b, N_DEV, (N_DEV, d, wn), (d, wn), lambda o, b: o.at[b], name)


def _grad_w_out(gated, dy, name):
    s, e = gated.shape
    d = dy.shape[1]
    ts = _row_tile(s, 1024)
    nk = s // ts

    def body(x_ref, y_ref, o_ref, acc_ref):
        k = pl.program_id(0)
        part = lax.dot_general(x_ref[...], y_ref[...], TN, preferred_element_type=F32)

        @pl.when(k == 0)
        def _():
            acc_ref[...] = part

        @pl.when(k > 0)
        def _():
            acc_ref[...] += part

        @pl.when(k == nk - 1)
        def _():
            o_ref[...] = acc_ref[...].astype(BF16).reshape(o_ref.shape)

    return pl.pallas_call(
        body, name=name, grid=(nk,),
        in_specs=[pl.BlockSpec((ts, e), lambda k: (k, 0)), pl.BlockSpec((ts, d), lambda k: (k, 0))],
        out_specs=pl.BlockSpec((N_DEV, e // N_DEV, d), lambda k: (0, 0, 0)),
        out_shape=_sds((N_DEV, e // N_DEV, d), BF16),
        scratch_shapes=[pltpu.VMEM((e, d), F32)],
        compiler_params=_cp("arbitrary"),
    )(gated, dy)


def _grad_w_group(pooled, dm, name):
    ng = len(POOL_WINDOWS)
    gw = pooled.shape[1] // ng
    return _tn_matmul(pooled, (dm,), gw, gw, lambda b: b, ng, (N_DEV, ng, gw // N_DEV, gw),
                      (N_DEV, gw // N_DEV, gw), lambda o, b: o.at[:, b], name)


def _adamw(staged, w, m, v, name):
    shape = w.shape
    nl = len(staged)
    n = staged[0].shape[0]
    cdim = shape[-1]
    total = 1
    for a in shape[:-1]:
        total *= a
    rows = total // nl
    sts = [st.reshape(n, rows, cdim) for st in staged]
    tr = rows if rows * cdim <= 128 * 1024 else max(8, (128 * 1024 // cdim) // 8 * 8)
    while rows % tr:
        tr -= 8
    nblk = rows // tr

    def body(*refs):
        s_refs = refs[:nl]
        w_ref, m_ref, v_ref, g_ref, d_ref, mo_ref, vo_ref = refs[nl:]
        for ll in range(nl):
            @pl.when(pl.program_id(0) == ll)
            def _():
                g = s_refs[ll][0].astype(F32)
                for j in range(1, n):
                    g = g + s_refs[ll][j].astype(F32)
                mn = ADAM_B1 * m_ref[...] + (1.0 - ADAM_B1) * g
                vn = ADAM_B2 * v_ref[...] + (1.0 - ADAM_B2) * (g * g)
                m_hat = mn / (1.0 - ADAM_B1 ** ADAM_STEP)
                v_hat = vn / (1.0 - ADAM_B2 ** ADAM_STEP)
                g_ref[...] = g
                d_ref[...] = -ADAM_LR * (m_hat / (jnp.sqrt(v_hat) + ADAM_EPS) + ADAM_WD * w_ref[...])
                mo_ref[...] = mn
                vo_ref[...] = vn

    blk = pl.BlockSpec((None, tr, cdim), lambda l, i: (l, i, 0))
    st_specs = [pl.BlockSpec((n, tr, cdim), lambda l, i, ll=ll: (0, jnp.clip(i + (l - ll) * nblk, 0, nblk - 1), 0))
                for ll in range(nl)]
    outs = pl.pallas_call(
        body, name=name, grid=(nl, nblk),
        in_specs=st_specs + [blk, blk, blk],
        out_specs=[blk] * 4, out_shape=[_sds((nl, rows, cdim), F32)] * 4,
        compiler_params=_cp("arbitrary", "arbitrary"),
    )(*sts, w.reshape(nl, rows, cdim), m.reshape(nl, rows, cdim), v.reshape(nl, rows, cdim))
    return [o.reshape(shape) for o in outs]


def _pack(parts, total):
    flat = jnp.concatenate([p.reshape(-1) for p in parts])
    return jnp.pad(flat, (0, total - flat.shape[0])).reshape(1, total)


def kernel(x, c, ada_w, ada_b, norm_g, a_w_in, a_w_group, a_scale, a_w_out, kv_norm_g, kv_ada_w, kv_ada_b, w_kv, b_w_in, b_rel_bias, b_w_out, final_g, loss_target, m_ada_w, m_ada_b, m_norm_g, m_a_w_in, m_a_w_group, m_a_scale, m_a_w_out, m_kv_norm_g, m_kv_ada_w, m_kv_ada_b, m_w_kv, m_b_w_in, m_b_rel_bias, m_b_w_out, m_final_g, v_ada_w, v_ada_b, v_norm_g, v_a_w_in, v_a_w_group, v_a_scale, v_a_w_out, v_kv_norm_g, v_kv_ada_w, v_kv_ada_b, v_w_kv, v_b_w_in, v_b_rel_bias, v_b_w_out, v_final_g):
    s, d = x.shape[1], x.shape[2]
    depth, _, wa = ada_w.shape
    wk = kv_ada_w.shape[1]
    n_a, n_b = a_w_in.shape[0], b_w_in.shape[0]
    e = a_w_out.shape[1] * N_DEV
    nrel = b_rel_bias.shape[-1]
    me = 4 * lax.axis_index("x") + 2 * lax.axis_index("y") + lax.axis_index("c")
    h0 = x[0]
    target = loss_target[0]

    bf = lambda a: a.astype(BF16)
    tie = lambda val, tok: lax.optimization_barrier((val, tok))[0]

    c_all = _exchange([c.reshape(1, 1, d)], [True], "gather_c")[0].reshape(N_DEV, d)
    ada_b_loc = lax.dynamic_slice(ada_b, (0, me * wa), (depth, wa)).reshape(depth, 1, wa)
    kv_ada_b_loc = lax.dynamic_slice(kv_ada_b, (me * wk,), (wk,)).reshape(1, wk)
    mods_cols = _mods_fwd(c_all, ada_w, ada_b_loc, kv_ada_w, kv_ada_b_loc)
    nm = depth * wa + wk
    mods_me, a_scale_all = _exchange([mods_cols.reshape(N_DEV, 1, nm), a_scale], [False, True], "exchange_mods")
    groups = [[bf(a_w_in[l]), bf(a_w_group[l]), bf(a_w_out[l])] for l in range(n_a)]
    groups.append([bf(w_kv), bf(b_w_in[0]), bf(b_w_out[0])])
    groups += [[bf(b_w_in[l]), bf(b_w_out[l])] for l in range(1, n_b)]
    handles = []
    token = mods_me
    for gi, grp in enumerate(groups):
        hd, token = _push_start(grp, [True] * len(grp), token, f"gather_start_{gi}")
        handles.append(hd)
    mods_me = mods_me.reshape(N_DEV, nm)
    mods = mods_me[:, :depth * wa].reshape(N_DEV, depth, wa).transpose(1, 0, 2).reshape(depth, 3, 1, d)
    kv_mods = mods_me[:, depth * wa:].reshape(2, 1, d)
    a_scale_full = a_scale_all.transpose(1, 0, 2).reshape(n_a, 1, e)
    ones_e = jnp.ones((1, e), F32)

    saved = []
    h = h0
    k = v = hk = wkv = None
    frames = [_bias_frames(tie(jnp.pad(b_rel_bias[bi], ((0, 0), (0, REL_PAD - nrel))), token).reshape(N_HEADS, 1, REL_PAD))
              for bi in range(n_b)]
    after = frames[-1][1]
    for layer in range(depth):
        shift, scale, gate = mods[layer, 0], mods[layer, 1], mods[layer, 2]
        g = norm_g[layer].reshape(1, d)
        if layer < n_a:
            w_in, w_group, w_out = _push_wait(handles[layer], after, f"gather_wait_{layer}")
            u, left, z = _inproj_fwd(h, g, shift, scale, w_in, "inproj_fwd")
            pooled, mixed = _pool_fwd(left, w_group)
            gated, y, hn = _gate_out_fwd(mixed, z, a_scale_full[layer], w_out, h, gate, "gate_out_fwd")
            saved.append(dict(h=h, u=u, left=left, z=z, pooled=pooled, m=mixed, gated=gated, y=y,
                              w_in=w_in, w_group=w_group, w_out=w_out))
        else:
            bi = layer - n_a
            got = _push_wait(handles[layer], after, f"gather_wait_{layer}")
            if bi == 0:
                wkv, w_in, w_out = got
                hk, k, v = _inproj_fwd(h, kv_norm_g.reshape(1, d), kv_mods[0], kv_mods[1], wkv, "inproj_kv", pad=LEFT)
            else:
                w_in, w_out = got
            u, left, z = _inproj_fwd(h, g, shift, scale, w_in, "inproj_fwd")
            frame, frame_t = frames[bi]
            att = _attn_fwd(left, k, v, frame)
            gated, y, hn = _gate_out_fwd(att, z, ones_e, w_out, h, gate, "gate_out_fwd")
            saved.append(dict(h=h, u=u, left=left, z=z, m=att, gated=gated, y=y, frame_t=frame_t,
                              w_in=w_in, w_out=w_out))
        h = hn
        after = h
    dh, d_final_g, loss_part = _final_loss(h, final_g.reshape(1, d), target)

    d_mods = [None] * depth
    d_norm_g = [None] * depth
    d_rel = [None] * n_b
    d_ascale = [None] * n_a
    assert n_b == 2
    dk = dv = None
    tie = lambda val, tok: lax.optimization_barrier((val, tok))[0]
    grad_handles = [None] * (depth + 1)
    for layer in range(depth - 1, n_a - 1, -1):
        bi = layer - n_a
        sv = saved[layer]
        scale, gate = mods[layer, 1], mods[layer, 2]
        dy, datt, dz, d_gate, _ = _out_bwd(dh, sv["y"], gate, sv["w_out"], sv["m"], sv["z"], ones_e, "out_bwd")
        dq, dk, dv, dframe_t = _attn_bwd(sv["left"], k, v, datt, sv["frame_t"], None if bi == n_b - 1 else (dk, dv))
        d_rel[bi] = _bias_bwd(dframe_t).reshape(N_HEADS, REL_PAD)[:, :nrel]
        g_out = _grad_w_out(sv["gated"], dy, "grad_w_out")
        g_in = _grad_w_in(sv["u"], dq, dz, "grad_w_in")
        grad_handles[layer], tok = _push_start([g_in, g_out], [False, False], d_gate, f"grads_start_{layer}")
        dh, st = _in_bwd(tie(dq, tok), dz, sv["w_in"], sv["h"], norm_g[layer].reshape(1, d), scale, dh, "in_bwd")
        d_mods[layer] = jnp.concatenate([st[0], st[1], d_gate[0]])
        d_norm_g[layer] = st[2]
    dkb, dvb = dk, dv
    h_kv = saved[n_a]["h"]
    g_w_kv = _grad_w_in(hk, dkb, dvb, "grad_w_in")
    grad_handles[depth], tok = _push_start([g_w_kv], [False], dkb, "grads_start_kv")
    dh, st = _in_bwd(tie(dkb, tok), dvb, wkv, h_kv, kv_norm_g.reshape(1, d), kv_mods[1], dh, "in_bwd")
    d_kv_mods = jnp.concatenate([st[0], st[1]])
    d_kv_norm_g = st[2]
    for layer in range(n_a - 1, -1, -1):
        sv = saved[layer]
        scale, gate = mods[layer, 1], mods[layer, 2]
        dy, dm, dz, d_gate, da = _out_bwd(dh, sv["y"], gate, sv["w_out"], sv["m"], sv["z"], a_scale_full[layer],
                                          "out_bwd")
        d_ascale[layer] = da.reshape(N_DEV, e // N_DEV)
        g_group = _grad_w_group(sv["pooled"], dm, "grad_w_group")
        g_out = _grad_w_out(sv["gated"], dy, "grad_w_out")
        if layer == 0:
            last_handle, tok = _push_start([g_group, g_out], [False] * 2, d_gate, "grads_start_last")
            dm = tie(dm, tok)
        dval = _pool_bwd(dm, sv["w_group"])
        g_in = _grad_w_in(sv["u"], dval, dz, "grad_w_in")
        late = [g_in] if layer == 0 else [g_in, g_group, g_out]
        grad_handles[layer], tok = _push_start(late, [False] * len(late), d_gate, f"grads_start_{layer}")
        dh, st = _in_bwd(tie(dval, tok), dz, sv["w_in"], sv["h"], norm_g[layer].reshape(1, d), scale, dh, "in_bwd")
        d_mods[layer] = jnp.concatenate([st[0], st[1], d_gate[0]])
        d_norm_g[layer] = st[2]
    grad_x = dh.reshape(1, s, d)

    d_mods = jnp.stack(d_mods)
    dm_slots = jnp.concatenate(
        [d_mods.reshape(depth, N_DEV, wa).transpose(1, 0, 2).reshape(N_DEV, depth * wa), d_kv_mods.reshape(N_DEV, wk)],
        axis=1).reshape(N_DEV, 1, nm)
    small_parts = [d_mods, d_kv_mods, jnp.stack(d_norm_g), d_kv_norm_g, d_final_g, jnp.stack(d_rel), loss_part[0, :1]]
    n_small = sum(int(p.size) for p in small_parts)
    n_small_pad = -(-n_small // LANES) * LANES
    small = _pack(small_parts, n_small_pad).reshape(1, 1, n_small_pad)
    d_ascale_slots = jnp.stack(d_ascale, axis=1)
    s_a_scale, dm_cols, small_all = _exchange([d_ascale_slots, dm_slots, small], [False, False, True],
                                              "exchange_small")

    g_ada_w, g_kv_ada_w = _mods_bwd(c_all, dm_cols.reshape(N_DEV, nm), depth, wa, wk)
    res = {}
    res["ada_w"] = _adamw([g_ada_w[None]], ada_w, m_ada_w, v_ada_w, "adamw")
    res["kv_ada_w"] = _adamw([g_kv_ada_w[None]], kv_ada_w, m_kv_ada_w, v_kv_ada_w, "adamw")
    res["a_scale"] = _adamw([s_a_scale[:, l] for l in range(n_a)], a_scale, m_a_scale, v_a_scale, "adamw")
    after = res["ada_w"][1]
    s_b = [None] * n_b
    for layer in range(depth - 1, n_a - 1, -1):
        s_b[layer - n_a] = _push_wait(grad_handles[layer], after, f"grads_wait_{layer}")
        after = s_b[layer - n_a][0]
    res["b_w_in"] = _adamw([sb[0] for sb in s_b], b_w_in, m_b_w_in, v_b_w_in, "adamw")
    res["b_w_out"] = _adamw([sb[1] for sb in s_b], b_w_out, m_b_w_out, v_b_w_out, "adamw")
    s_w_kv, = _push_wait(grad_handles[depth], res["b_w_out"][1], "grads_wait_kv")
    res["w_kv"] = _adamw([s_w_kv], w_kv, m_w_kv, v_w_kv, "adamw")
    after = res["w_kv"][1]
    s_a = [None] * n_a
    for layer in range(n_a - 1, 0, -1):
        s_a[layer] = _push_wait(grad_handles[layer], after, f"grads_wait_{layer}")
        after = s_a[layer][0]
    s_group0, s_out0 = _push_wait(last_handle, after, "grads_wait_last")
    s_a[0] = [None, s_group0, s_out0]
    res["a_w_group"] = _adamw([sa[1] for sa in s_a], a_w_group, m_a_w_group, v_a_w_group, "adamw")
    res["a_w_out"] = _adamw([sa[2] for sa in s_a], a_w_out, m_a_w_out, v_a_w_out, "adamw")
    s_a[0][0], = _push_wait(grad_handles[0], res["a_w_out"][1], "grads_wait_0")
    res["a_w_in"] = _adamw([sa[0] for sa in s_a], a_w_in, m_a_w_in, v_a_w_in, "adamw")
    small_names = ["ada_b", "kv_ada_b", "norm_g", "kv_norm_g", "final_g", "b_rel_bias"]
    small_w = dict(ada_b=ada_b, kv_ada_b=kv_ada_b, norm_g=norm_g, kv_norm_g=kv_norm_g, final_g=final_g,
                   b_rel_bias=b_rel_bias)
    small_m = dict(ada_b=m_ada_b, kv_ada_b=m_kv_ada_b, norm_g=m_norm_g, kv_norm_g=m_kv_norm_g, final_g=m_final_g,
                   b_rel_bias=m_b_rel_bias)
    small_v = dict(ada_b=v_ada_b, kv_ada_b=v_kv_ada_b, norm_g=v_norm_g, kv_norm_g=v_kv_norm_g, final_g=v_final_g,
                   b_rel_bias=v_b_rel_bias)
    sw = _pack([small_w[n] for n in small_names], n_small_pad)
    smm = _pack([small_m[n] for n in small_names], n_small_pad)
    svv = _pack([small_v[n] for n in small_names] + [jnp.ones((n_small_pad - n_small + 1,), F32)], n_small_pad)
    small_out = _adamw([small_all.reshape(N_DEV, 1, n_small_pad)], sw, smm, svv, "adamw")
    off = 0
    for n in small_names:
        size = int(small_w[n].size)
        res[n] = [o[0, off:off + size].reshape(small_w[n].shape) for o in small_out]
        off += size
    loss = small_out[0][0, n_small - 1]

    order = ["ada_w", "ada_b", "norm_g", "a_w_in", "a_w_group", "a_scale", "a_w_out", "kv_norm_g", "kv_ada_w",
             "kv_ada_b", "w_kv", "b_w_in", "b_rel_bias", "b_w_out", "final_g"]
    outs = [loss, grad_x]
    for part in range(4):
        outs += [res[n][part] for n in order]
    return tuple(outs)
```

```python
import functools

import jax
import jax.numpy as jnp
from jax import lax
from jax.experimental import pallas as pl
from jax.experimental.pallas import tpu as pltpu

F32 = jnp.float32
BF16 = jnp.bfloat16

N_DEV = 8
CHUNK = 64
LEFT_CHUNKS = 8
N_HEADS = 16
POOL_WINDOWS = (2, 4, 8, 16)
REL_CLIP = 128
EPS = 1e-6
ADAM_LR = 0.001
ADAM_B1 = 0.9
ADAM_B2 = 0.999
ADAM_EPS = 1e-08
ADAM_WD = 0.01
ADAM_STEP = 10

LANES = 128
TQ = 2 * CHUNK
LEFT = LEFT_CHUNKS * CHUNK
BAND = LEFT + TQ
FRAME_PAD = BAND + TQ
NFRAME = LEFT // TQ + 1
POOL_HALO = 16
REL_PAD = 384
NEG = -1e30
ATTN_GROUP_FWD = 64
ATTN_GROUP = 32
ATTN_GROUP_ACC = 16
ATTN_BATCH = 8
ATTN_BATCH_BWD = 8
VMEM_LIMIT = 56 * 1024 * 1024

NT = (((1,), (1,)), ((), ()))
TN = (((0,), (0,)), ((), ()))

HBM_SPEC = pl.BlockSpec(memory_space=pltpu.HBM)
ANY_SPEC = pl.BlockSpec(memory_space=pl.ANY)
SEM_SPEC = pl.BlockSpec(memory_space=pltpu.SEMAPHORE)


def _cp(*sem):
    return pltpu.CompilerParams(dimension_semantics=sem or None, vmem_limit_bytes=VMEM_LIMIT)


def _sds(shape, dtype):
    return jax.ShapeDtypeStruct(tuple(shape), dtype)


def _row_tile(s, want):
    return min(want, s)


def _vec(d):
    return pl.BlockSpec((1, d), lambda *_: (0, 0))


def _sigmoid(z):
    return 1.0 / (1.0 + jnp.exp(-z))


def _exchange(xs, gather, name):
    n = len(xs)
    out_shapes = []
    for x, ga in zip(xs, gather):
        out_shapes.append(_sds((N_DEV,) + (x.shape if ga else x.shape[1:]), x.dtype))

    def body(*refs):
        ins, outs = refs[:n], refs[n:2 * n]
        send_sems, recv_sems, local_sems = refs[2 * n:]
        mx, my, mc = lax.axis_index("x"), lax.axis_index("y"), lax.axis_index("c")
        me = 4 * mx + 2 * my + mc
        local = []
        for k in range(n):
            src = ins[k] if gather[k] else ins[k].at[me]
            cp = pltpu.make_async_copy(src, outs[k].at[me], local_sems.at[k])
            cp.start()
            local.append(cp)
        sends, recvs = [], []
        for r in range(1, N_DEV):
            px, py, pc = (mx + (r >> 2)) % 2, (my + ((r >> 1) & 1)) % 2, (mc + (r & 1)) % 2
            peer = 4 * px + 2 * py + pc
            for k in range(n):
                sem = k * (N_DEV - 1) + r - 1
                src = ins[k] if gather[k] else ins[k].at[peer]
                send = pltpu.make_async_remote_copy(
                    src_ref=src, dst_ref=outs[k].at[me], send_sem=send_sems.at[sem], recv_sem=recv_sems.at[sem],
                    device_id=(px, py, pc), device_id_type=pl.DeviceIdType.MESH)
                send.start()
                sends.append(send)
                recvs.append(pltpu.make_async_remote_copy(
                    src_ref=src, dst_ref=outs[k].at[peer], send_sem=send_sems.at[sem], recv_sem=recv_sems.at[sem],
                    device_id=(px, py, pc), device_id_type=pl.DeviceIdType.MESH))
        for cp in recvs:
            cp.wait_recv()
        for cp in sends:
            cp.wait_send()
        for cp in local:
            cp.wait()

    return pl.pallas_call(
        body, name=name, out_shape=out_shapes,
        in_specs=[HBM_SPEC] * n, out_specs=[HBM_SPEC] * n,
        scratch_shapes=[pltpu.SemaphoreType.DMA((n * (N_DEV - 1),)), pltpu.SemaphoreType.DMA((n * (N_DEV - 1),)),
                        pltpu.SemaphoreType.DMA((n,))],
    )(*xs)


def _peer(mx, my, mc, r):
    px, py, pc = (mx + (r >> 2)) % 2, (my + ((r >> 1) & 1)) % 2, (mc + (r & 1)) % 2
    return (px, py, pc), 4 * px + 2 * py + pc


def _push_start(xs, gather, dep, name):
    n = len(xs)
    nsem = n * (N_DEV - 1)
    me = 4 * lax.axis_index("x") + 2 * lax.axis_index("y") + lax.axis_index("c")
    lands = []
    for x, ga in zip(xs, gather):
        own = x[None] if ga else lax.dynamic_index_in_dim(x, me, 0, keepdims=True)
        empty = lax.empty((N_DEV,) + own.shape[1:], x.dtype)
        lands.append(lax.dynamic_update_slice(empty, own, (me,) + (0,) * (own.ndim - 1)))

    def body(*refs):
        ins, lands_in = refs[:n], refs[n:2 * n]
        send_sems, recv_sems = refs[2 * n + 1], refs[2 * n + 2]
        token = refs[-1]
        mx, my, mc = lax.axis_index("x"), lax.axis_index("y"), lax.axis_index("c")
        mine = 4 * mx + 2 * my + mc
        for k in range(n):
            for r in range(1, N_DEV):
                dev, peer = _peer(mx, my, mc, r)
                sem = k * (N_DEV - 1) + r - 1
                pltpu.make_async_remote_copy(
                    src_ref=ins[k] if gather[k] else ins[k].at[peer], dst_ref=lands_in[k].at[mine],
                    send_sem=send_sems.at[sem], recv_sem=recv_sems.at[sem],
                    device_id=dev, device_id_type=pl.DeviceIdType.MESH).start()
        token[...] = jnp.zeros_like(token)

    hbm = lambda a: pltpu.HBM(a.shape, a.dtype)
    outs = pl.pallas_call(
        body, name=name,
        out_shape=(pltpu.SemaphoreType.DMA((nsem,)), pltpu.SemaphoreType.DMA((nsem,)),
                   *[hbm(x) for x in xs], *[hbm(a) for a in lands], _sds((8, LANES), F32)),
        in_specs=[HBM_SPEC] * (2 * n) + [ANY_SPEC],
        out_specs=(SEM_SPEC, SEM_SPEC, *[HBM_SPEC] * (2 * n), pl.BlockSpec(memory_space=pltpu.VMEM)),
        input_output_aliases={k: 2 + k for k in range(2 * n)},
        compiler_params=pltpu.CompilerParams(has_side_effects=pltpu.SideEffectType.DATAFLOW_SIDE_EFFECTING),
    )(*[pltpu.with_memory_space_constraint(x, pltpu.HBM) for x in xs],
      *[pltpu.with_memory_space_constraint(a, pltpu.HBM) for a in lands], dep)
    return (outs[0], outs[1], outs[2:2 + n], outs[2 + n:2 + 2 * n], tuple(gather)), outs[-1]


def _push_wait(handle, after, name):
    send_sems, recv_sems, srcs, lands, gather = handle
    n = len(srcs)

    def body(*refs):
        ins, lands_in = refs[:n], refs[n:2 * n]
        send_sems, recv_sems = refs[2 * n], refs[2 * n + 1]
        mx, my, mc = lax.axis_index("x"), lax.axis_index("y"), lax.axis_index("c")
        for k in range(n):
            for r in range(1, N_DEV):
                dev, peer = _peer(mx, my, mc, r)
                sem = k * (N_DEV - 1) + r - 1
                cp = pltpu.make_async_remote_copy(
                    src_ref=ins[k] if gather[k] else ins[k].at[peer], dst_ref=lands_in[k].at[peer],
                    send_sem=send_sems.at[sem], recv_sem=recv_sems.at[sem],
                    device_id=dev, device_id_type=pl.DeviceIdType.MESH)
                cp.wait_send()
                cp.wait_recv()

    hbm = lambda a: pltpu.HBM(a.shape, a.dtype)
    outs = pl.pallas_call(
        body, name=name,
        out_shape=(*[hbm(x) for x in srcs], *[hbm(a) for a in lands]),
        in_specs=[HBM_SPEC] * (2 * n) + [SEM_SPEC, SEM_SPEC, ANY_SPEC],
        out_specs=tuple([HBM_SPEC] * (2 * n)),
        input_output_aliases={k: k for k in range(2 * n)},
        compiler_params=pltpu.CompilerParams(has_side_effects=pltpu.SideEffectType.DATAFLOW_SIDE_EFFECTING),
    )(*srcs, *lands, send_sems, recv_sems, after)
    return list(outs[n:])


def _mods_fwd(c_all, ada_w, ada_b_loc, kv_ada_w, kv_ada_b_loc):
    nl, d, wa = ada_w.shape
    wk = kv_ada_w.shape[1]

    def body(c_ref, w_ref, b_ref, kw_ref, kb_ref, o_ref):
        c = c_ref[...]
        ca = c * _sigmoid(c)
        for l in range(nl):
            o_ref[:, l * wa:(l + 1) * wa] = jnp.dot(
                ca, w_ref[l], preferred_element_type=F32, precision=lax.Precision.HIGHEST) + b_ref[l]
        o_ref[:, nl * wa:] = jnp.dot(
            ca, kw_ref[...], preferred_element_type=F32, precision=lax.Precision.HIGHEST) + kb_ref[...]

    return pl.pallas_call(body, name="mods_fwd", out_shape=_sds((N_DEV, nl * wa + wk), F32),
                          compiler_params=_cp())(c_all, ada_w, ada_b_loc, kv_ada_w, kv_ada_b_loc)


def _mods_bwd(c_all, dm, nl, wa, wk):
    d = c_all.shape[1]

    def body(c_ref, d_ref, gw_ref, gk_ref):
        c = c_ref[...]
        ca = c * _sigmoid(c)
        for l in range(nl):
            gw_ref[l] = lax.dot_general(ca, d_ref[:, l * wa:(l + 1) * wa], TN,
                                        preferred_element_type=F32, precision=lax.Precision.HIGHEST)
        gk_ref[...] = lax.dot_general(ca, d_ref[:, nl * wa:], TN,
                                      preferred_element_type=F32, precision=lax.Precision.HIGHEST)

    return pl.pallas_call(body, name="mods_bwd", out_shape=[_sds((nl, d, wa), F32), _sds((d, wk), F32)],
                          compiler_params=_cp())(c_all, dm)


def _norm_mod(hf, g, shift, scale):
    rs = lax.rsqrt(jnp.mean(hf * hf, axis=-1, keepdims=True) + EPS)
    xhat = hf * rs
    n = xhat * g
    return rs, xhat, n, n * (1.0 + scale) + shift


def _inproj_fwd(h, g, shift, scale, w, name, pad=0):
    s, d = h.shape
    nj, _, wn = w.shape
    half = nj // 2
    tm = _row_tile(s, 512)
    assert pad in (0, tm)
    pb = pad // tm

    def body(h_ref, g_ref, sh_ref, sc_ref, w_ref, u_ref, l_ref, r_ref):
        _, _, _, u = _norm_mod(h_ref[...], g_ref[...], sh_ref[...], sc_ref[...])
        ub = u.astype(BF16)
        u_ref[...] = ub
        for j in range(nj):
            o_ref = l_ref if j < half else r_ref
            jj = j % half
            o_ref[:, jj * wn:(jj + 1) * wn] = jnp.dot(ub, w_ref[j], preferred_element_type=F32).astype(BF16)
        if pb:
            @pl.when(pl.program_id(0) == 0)
            def _():
                l_ref[...] = jnp.zeros_like(l_ref)
                r_ref[...] = jnp.zeros_like(r_ref)

    e = half * wn
    src = lambda i: (jnp.maximum(i - pb, 0), 0)
    return pl.pallas_call(
        body, name=name, grid=(s // tm + pb,),
        in_specs=[pl.BlockSpec((tm, d), src), _vec(d), _vec(d), _vec(d),
                  pl.BlockSpec((nj, d, wn), lambda i: (0, 0, 0))],
        out_specs=[pl.BlockSpec((tm, d), src), pl.BlockSpec((tm, e), lambda i: (i, 0)),
                   pl.BlockSpec((tm, e), lambda i: (i, 0))],
        out_shape=[_sds((s, d), BF16), _sds((s + pad, e), BF16), _sds((s + pad, e), BF16)],
        compiler_params=_cp("arbitrary"),
    )(h, g, shift, scale, w)


def _pool_fwd(val, wg):
    s, e = val.shape
    ng = len(POOL_WINDOWS)
    gw = e // ng
    tm = _row_tile(s, 512)
    hb = tm // POOL_HALO

    def body(v_ref, halo_ref, w_ref, p_ref, m_ref):
        i = pl.program_id(0)
        t = i * tm + lax.broadcasted_iota(jnp.int32, (tm, 1), 0)
        for g, wdw in enumerate(POOL_WINDOWS):
            cols = slice(g * gw, (g + 1) * gw)
            v = v_ref[:, cols].astype(F32)
            halo = jnp.where(i > 0, halo_ref[:, cols].astype(F32), 0.0)
            acc = jnp.concatenate([halo, v], axis=0)
            sh = 1
            while sh < wdw:
                acc = acc + pltpu.roll(acc, sh, 0)
                sh *= 2
            cnt = jnp.minimum(t + 1, wdw).astype(F32)
            pb = (acc[POOL_HALO:, :] / cnt - v).astype(BF16)
            p_ref[:, cols] = pb
            m_ref[:, cols] = jnp.dot(pb, w_ref[:, g].reshape(gw, gw), preferred_element_type=F32).astype(BF16)

    return pl.pallas_call(
        body, name="pool_fwd", grid=(s // tm,),
        in_specs=[pl.BlockSpec((tm, e), lambda i: (i, 0)),
                  pl.BlockSpec((POOL_HALO, e), lambda i: (jnp.maximum(i * hb - 1, 0), 0)),
                  pl.BlockSpec((N_DEV, ng, gw // N_DEV, gw), lambda i: (0, 0, 0, 0))],
        out_specs=[pl.BlockSpec((tm, e), lambda i: (i, 0)), pl.BlockSpec((tm, e), lambda i: (i, 0))],
        out_shape=[_sds((s, e), BF16), _sds((s, e), BF16)],
        compiler_params=_cp("arbitrary"),
    )(val, val, wg)


def _gate_out_fwd(m, z, ascale, w, h, gate, name):
    s, e = m.shape
    d = h.shape[1]
    tm = _row_tile(s, 512)

    def body(m_ref, z_ref, a_ref, w_ref, h_ref, g_ref, gd_ref, y_ref, ho_ref):
        z = z_ref[...].astype(F32)
        gb = ((m_ref[...].astype(F32) * a_ref[...]) * (z * _sigmoid(z))).astype(BF16)
        gd_ref[...] = gb
        y = jnp.dot(gb, w_ref[...].reshape(e, d), preferred_element_type=F32)
        y_ref[...] = y.astype(BF16)
        ho_ref[...] = h_ref[...] + g_ref[...] * y

    return pl.pallas_call(
        body, name=name, grid=(s // tm,),
        in_specs=[pl.BlockSpec((tm, e), lambda i: (i, 0)), pl.BlockSpec((tm, e), lambda i: (i, 0)), _vec(e),
                  pl.BlockSpec((N_DEV, e // N_DEV, d), lambda i: (0, 0, 0)),
                  pl.BlockSpec((tm, d), lambda i: (i, 0)), _vec(d)],
        out_specs=[pl.BlockSpec((tm, e), lambda i: (i, 0)), pl.BlockSpec((tm, d), lambda i: (i, 0)),
                   pl.BlockSpec((tm, d), lambda i: (i, 0))],
        out_shape=[_sds((s, e), BF16), _sds((s, d), BF16), _sds((s, d), F32)],
        compiler_params=_cp("arbitrary"),
    )(m, z, ascale, w, h, gate)


def _rel_onehot(shape, r_axis):
    r = lax.broadcasted_iota(jnp.int32, shape, r_axis)
    j = lax.broadcasted_iota(jnp.int32, shape, 1 - r_axis)
    dist = LEFT - (j - TQ)
    return (jnp.clip(dist, -REL_CLIP, REL_CLIP) + REL_CLIP == r).astype(F32)


def _skew(x, sign):
    row = lax.broadcasted_iota(jnp.int32, x.shape, 0)
    for b in range(TQ.bit_length() - 1):
        amt = (1 << b) if sign > 0 else FRAME_PAD - (1 << b)
        x = jnp.where(((row >> b) & 1) == 1, pltpu.roll(x, amt, 1), x)
    return x


def _bias_frames(rel):
    nh = rel.shape[0]

    def body(r_ref, f_ref, ft_ref):
        fext = jnp.dot(r_ref[...], _rel_onehot((REL_PAD, FRAME_PAD), 0), preferred_element_type=F32,
                       precision=lax.Precision.HIGHEST)
        x = _skew(jnp.broadcast_to(fext, (TQ, FRAME_PAD)), 1)[:, TQ:]
        qc = lax.broadcasted_iota(jnp.int32, (TQ, BAND), 0) // CHUNK
        m = lax.broadcasted_iota(jnp.int32, (TQ, BAND), 1)
        mc = m // CHUNK
        x = jnp.where((mc >= qc) & (mc <= qc + LEFT_CHUNKS), x, NEG)
        for f in range(NFRAME):
            xf = jnp.where(m >= LEFT - f * TQ, x, NEG)
            f_ref[f] = xf
            ft_ref[f] = xf.T

    return pl.pallas_call(
        body, name="bias_frames", grid=(nh,),
        in_specs=[pl.BlockSpec((None, 1, REL_PAD), lambda h: (h, 0, 0))],
        out_specs=[pl.BlockSpec((None, NFRAME, TQ, BAND), lambda h: (h, 0, 0, 0)),
                   pl.BlockSpec((None, NFRAME, BAND, TQ), lambda h: (h, 0, 0, 0))],
        out_shape=[_sds((nh, NFRAME, TQ, BAND), F32), _sds((nh, NFRAME, BAND, TQ), F32)],
        compiler_params=_cp("arbitrary"),
    )(rel)


def _bias_bwd(dft):
    nh = dft.shape[0]

    def body(d_ref, o_ref):
        d = d_ref[0]
        for f in range(1, NFRAME):
            d = d + d_ref[f]
        x = jnp.concatenate([jnp.zeros((TQ, TQ), F32), d.T], axis=1)
        col = jnp.sum(_skew(x, -1), axis=0, keepdims=True)
        o_ref[...] = jnp.dot(col, _rel_onehot((FRAME_PAD, REL_PAD), 1), preferred_element_type=F32,
                             precision=lax.Precision.HIGHEST)

    return pl.pallas_call(
        body, name="bias_bwd", grid=(nh,),
        in_specs=[pl.BlockSpec((None, NFRAME, BAND, TQ), lambda h: (h, 0, 0, 0))],
        out_specs=pl.BlockSpec((None, 1, REL_PAD), lambda h: (h, 0, 0)),
        out_shape=_sds((nh, 1, REL_PAD), F32),
        compiler_params=_cp("arbitrary"),
    )(dft)


def _attn_tiles(s, group):
    nt = s // TQ
    ni = min(group, nt)
    return nt // ni, ni


def _attn_fwd(q, k, v, frame):
    s, e = q.shape
    dh = e // N_HEADS
    ng, ni = _attn_tiles(s, ATTN_GROUP_FWD)
    sm = dh ** -0.5
    nb = min(ATTN_BATCH, ni)

    def body(q_ref, k_ref, v_ref, b_ref, o_ref):
        g0 = pl.program_id(1) * ni
        for tb in range(0, ni, nb):
            ts = range(tb, tb + nb)
            keys = [pl.ds(pl.multiple_of((g0 + t) * TQ, TQ), BAND) for t in ts]
            rows = [pl.ds(t * TQ, TQ) for t in ts]
            scs = [lax.dot_general(q_ref[r, :], k_ref[kk, :], NT, preferred_element_type=F32)
                   for r, kk in zip(rows, keys)]
            ps, ls = [], []
            for t, sc in zip(ts, scs):
                sc = sc * sm + b_ref[jnp.minimum(g0 + t, NFRAME - 1)]
                p = jnp.exp(sc - jnp.max(sc, axis=-1, keepdims=True))
                ls.append(jnp.sum(p, axis=-1, keepdims=True))
                ps.append(p.astype(BF16))
            for r, p, l, kk in zip(rows, ps, ls, keys):
                o = jnp.dot(p, v_ref[kk, :], preferred_element_type=F32)
                o_ref[r, :] = (o / l).astype(BF16)

    return pl.pallas_call(
        body, name="attn_fwd", grid=(N_HEADS, ng),
        in_specs=[pl.BlockSpec((ni * TQ, dh), lambda h, i: (i, h)),
                  pl.BlockSpec((s + LEFT, dh), lambda h, i: (0, h)), pl.BlockSpec((s + LEFT, dh), lambda h, i: (0, h)),
                  pl.BlockSpec((None, NFRAME, TQ, BAND), lambda h, i: (h, 0, 0, 0))],
        out_specs=pl.BlockSpec((ni * TQ, dh), lambda h, i: (i, h)),
        out_shape=_sds((s, e), BF16),
        compiler_params=_cp("arbitrary", "arbitrary"),
    )(q, k, v, frame)


def _final_loss(h, g, target):
    s, d = h.shape
    tm = _row_tile(s, 512)

    def body(h_ref, g_ref, t_ref, dh_ref, dg_ref, l_ref):
        @pl.when(pl.program_id(0) == 0)
        def _():
            dg_ref[...] = jnp.zeros_like(dg_ref)
            l_ref[...] = jnp.zeros_like(l_ref)

        hf = h_ref[...]
        gg = g_ref[...]
        rs = lax.rsqrt(jnp.mean(hf * hf, axis=-1, keepdims=True) + EPS)
        xhat = hf * rs
        diff = xhat * gg - t_ref[...]
        l_ref[...] += 0.5 * jnp.sum(jnp.mean(diff * diff, axis=-1, keepdims=True), axis=0, keepdims=True)
        dout = diff * (1.0 / d)
        dg_ref[...] += jnp.sum(dout * xhat, axis=0, keepdims=True)
        dxh = dout * gg
        dh_ref[...] = rs * (dxh - xhat * jnp.mean(dxh * xhat, axis=-1, keepdims=True))

    return pl.pallas_call(
        body, name="final_loss", grid=(s // tm,),
        in_specs=[pl.BlockSpec((tm, d), lambda i: (i, 0)), _vec(d), pl.BlockSpec((tm, d), lambda i: (i, 0))],
        out_specs=[pl.BlockSpec((tm, d), lambda i: (i, 0)), _vec(d), _vec(LANES)],
        out_shape=[_sds((s, d), F32), _sds((1, d), F32), _sds((1, LANES), F32)],
        compiler_params=_cp("arbitrary"),
    )(h, g, target)


def _out_bwd(dh, y, gate, w, m, z, ascale, name):
    s, d = dh.shape
    e = m.shape[1]
    tm = _row_tile(s, 512)

    def body(dh_ref, y_ref, g_ref, w_ref, m_ref, z_ref, a_ref, dy_ref, dm_ref, dz_ref, dg_ref, da_ref):
        @pl.when(pl.program_id(0) == 0)
        def _():
            dg_ref[...] = jnp.zeros_like(dg_ref)
            da_ref[...] = jnp.zeros_like(da_ref)

        dhf = dh_ref[...]
        dg_ref[...] += jnp.sum(dhf * y_ref[...].astype(F32), axis=0, keepdims=True)
        dyb = (g_ref[...] * dhf).astype(BF16)
        dy_ref[...] = dyb
        dgated = lax.dot_general(dyb, w_ref[...].reshape(e, d), NT, preferred_element_type=F32)
        z = z_ref[...].astype(F32)
        sig = _sigmoid(z)
        mf = m_ref[...].astype(F32)
        a = a_ref[...]
        dms = dgated * (z * sig)
        da_ref[...] += jnp.sum(dms * mf, axis=0, keepdims=True)
        dm_ref[...] = (dms * a).astype(BF16)
        dz_ref[...] = (dgated * (mf * a) * (sig * (1.0 + z * (1.0 - sig)))).astype(BF16)

    return pl.pallas_call(
        body, name=name, grid=(s // tm,),
        in_specs=[pl.BlockSpec((tm, d), lambda i: (i, 0)), pl.BlockSpec((tm, d), lambda i: (i, 0)), _vec(d),
                  pl.BlockSpec((N_DEV, e // N_DEV, d), lambda i: (0, 0, 0)),
                  pl.BlockSpec((tm, e), lambda i: (i, 0)), pl.BlockSpec((tm, e), lambda i: (i, 0)), _vec(e)],
        out_specs=[pl.BlockSpec((tm, d), lambda i: (i, 0)), pl.BlockSpec((tm, e), lambda i: (i, 0)),
                   pl.BlockSpec((tm, e), lambda i: (i, 0)), _vec(d), _vec(e)],
        out_shape=[_sds((s, d), BF16), _sds((s, e), BF16), _sds((s, e), BF16), _sds((1, d), F32), _sds((1, e), F32)],
        compiler_params=_cp("arbitrary"),
    )(dh, y, gate, w, m, z, ascale)


def _pool_bwd(dm, wg):
    s, e = dm.shape
    ng = len(POOL_WINDOWS)
    gw = e // ng
    tm = _row_tile(s, 512)
    hb = tm // POOL_HALO
    nsteps = s // tm

    def body(d_ref, halo_ref, w_ref, o_ref):
        i = pl.program_id(0)
        t = i * tm + lax.broadcasted_iota(jnp.int32, (tm + POOL_HALO, 1), 0)
        for g, wdw in enumerate(POOL_WINDOWS):
            cols = slice(g * gw, (g + 1) * gw)
            dmx = jnp.concatenate([d_ref[:, cols], halo_ref[:, cols]], axis=0)
            dp = lax.dot_general(dmx, w_ref[:, g].reshape(gw, gw), NT, preferred_element_type=F32)
            dp = jnp.where(t < s, dp, 0.0)
            acc = dp / jnp.minimum(t + 1, wdw).astype(F32)
            sh = 1
            while sh < wdw:
                acc = acc + pltpu.roll(acc, tm + POOL_HALO - sh, 0)
                sh *= 2
            o_ref[:, cols] = (acc[:tm, :] - dp[:tm, :]).astype(BF16)

    return pl.pallas_call(
        body, name="pool_bwd", grid=(nsteps,),
        in_specs=[pl.BlockSpec((tm, e), lambda i: (i, 0)),
                  pl.BlockSpec((POOL_HALO, e), lambda i: (jnp.minimum((i + 1) * hb, s // POOL_HALO - 1), 0)),
                  pl.BlockSpec((N_DEV, ng, gw // N_DEV, gw), lambda i: (0, 0, 0, 0))],
        out_specs=pl.BlockSpec((tm, e), lambda i: (i, 0)),
        out_shape=_sds((s, e), BF16),
        compiler_params=_cp("arbitrary"),
    )(dm, dm, wg)


def _attn_bwd(q, k, v, do, frame_t, prev=None):
    s, e = q.shape
    dh = e // N_HEADS
    ng, ni = _attn_tiles(s, ATTN_GROUP if prev is None else ATTN_GROUP_ACC)
    sm = dh ** -0.5
    nleft = LEFT // TQ

    def core(q_ref, k_ref, v_ref, do_ref, b_ref, dq_ref, dk_ref, dv_ref, db_ref, init):
        i = pl.program_id(1)
        g0 = i * ni

        @pl.when(i == 0)
        def _():
            init()
            db_ref[...] = jnp.zeros_like(db_ref)

        keys = [pl.ds(pl.multiple_of((g0 + t) * TQ, TQ), BAND) for t in range(ni)]
        rows = [pl.ds(t * TQ, TQ) for t in range(ni)]
        pbs, dsbs = [], []
        nb = min(ATTN_BATCH_BWD, ni)
        for tb in range(0, ni, nb):
            tiles = range(tb, tb + nb)
            sts = [lax.dot_general(k_ref[keys[t], :], q_ref[rows[t], :], NT, preferred_element_type=F32) for t in tiles]
            dpts = [lax.dot_general(v_ref[keys[t], :], do_ref[rows[t], :], NT, preferred_element_type=F32) for t in tiles]
            dsts = {}
            for t, st, dpt in zip(tiles, sts, dpts):
                st = st * sm + b_ref[jnp.minimum(g0 + t, NFRAME - 1)]
                p = jnp.exp(st - jnp.max(st, axis=0, keepdims=True))
                p = p * (1.0 / jnp.sum(p, axis=0, keepdims=True))
                dst = p * (dpt - jnp.sum(dpt * p, axis=0, keepdims=True))
                pbs.append(p.astype(BF16))
                dsbs.append(dst.astype(BF16))
                dsts[t] = dst
            rest = None
            for t in tiles:
                if t < nleft:
                    db_ref[jnp.minimum(g0 + t, NFRAME - 1)] += dsts[t]
                else:
                    rest = dsts[t] if rest is None else rest + dsts[t]
            if rest is not None:
                db_ref[NFRAME - 1] += rest
            for t in tiles:
                dq = lax.dot_general(dsbs[t], k_ref[keys[t], :], TN, preferred_element_type=F32)
                dq_ref[rows[t], :] = (dq * sm).astype(BF16)
        for r in range(ni + nleft):
            ts = [t for t in range(ni) if 0 <= r - t <= nleft]
            blk = lambda xs: jnp.concatenate([xs[t][(r - t) * TQ:(r - t + 1) * TQ, :] for t in ts], axis=1)
            qrows = slice(ts[0] * TQ, (ts[-1] + 1) * TQ)
            krows = pl.ds(pl.multiple_of((g0 + r) * TQ, TQ), TQ)
            dk_ref[krows, :] += jnp.dot(blk(dsbs), q_ref[qrows, :], preferred_element_type=F32) * sm
            dv_ref[krows, :] += jnp.dot(blk(pbs), do_ref[qrows, :], preferred_element_type=F32)

    tile_spec = pl.BlockSpec((ni * TQ, dh), lambda h, i: (i, h))
    kv_spec = pl.BlockSpec((s + LEFT, dh), lambda h, i: (0, h))
    fr_spec = pl.BlockSpec((None, NFRAME, BAND, TQ), lambda h, i: (h, 0, 0, 0))
    fr_shape = _sds((N_HEADS, NFRAME, BAND, TQ), F32)
    if prev is None:
        def body(q_ref, k_ref, v_ref, do_ref, b_ref, dq_ref, dk_ref, dv_ref, db_ref):
            def init():
                dk_ref[...] = jnp.zeros_like(dk_ref)
                dv_ref[...] = jnp.zeros_like(dv_ref)
            core(q_ref, k_ref, v_ref, do_ref, b_ref, dq_ref, dk_ref, dv_ref, db_ref, init)

        return pl.pallas_call(
            body, name="attn_bwd", grid=(N_HEADS, ng),
            in_specs=[tile_spec, kv_spec, kv_spec, tile_spec, fr_spec],
            out_specs=[tile_spec, kv_spec, kv_spec, fr_spec],
            out_shape=[_sds((s, e), BF16), _sds((s + LEFT, e), F32), _sds((s + LEFT, e), F32), fr_shape],
            compiler_params=_cp("arbitrary", "arbitrary"),
        )(q, k, v, do, frame_t)

    def body_acc(q_ref, k_ref, v_ref, do_ref, b_ref, dkp_hbm, dvp_hbm, dq_ref, dkb_ref, dvb_ref, db_ref,
                 dk_acc, dv_acc, dkp_buf, dvp_buf, sems):
        cols = pl.ds(pl.multiple_of(pl.program_id(0) * dh, dh), dh)
        fetch = [pltpu.make_async_copy(dkp_hbm.at[:, cols], dkp_buf, sems.at[0]),
                 pltpu.make_async_copy(dvp_hbm.at[:, cols], dvp_buf, sems.at[1])]

        def init():
            for cp in fetch:
                cp.start()
            dk_acc[...] = jnp.zeros_like(dk_acc)
            dv_acc[...] = jnp.zeros_like(dv_acc)
        core(q_ref, k_ref, v_ref, do_ref, b_ref, dq_ref, dk_acc, dv_acc, db_ref, init)

        @pl.when(pl.program_id(1) == ng - 1)
        def _():
            for cp in fetch:
                cp.wait()
            dkb_ref[...] = (dk_acc[LEFT:, :] + dkp_buf[LEFT:, :]).astype(BF16)
            dvb_ref[...] = (dv_acc[LEFT:, :] + dvp_buf[LEFT:, :]).astype(BF16)

    out_kv = pl.BlockSpec((s, dh), lambda h, i: (0, h))
    acc = pltpu.VMEM((s + LEFT, dh), F32)
    return pl.pallas_call(
        body_acc, name="attn_bwd_acc", grid=(N_HEADS, ng),
        in_specs=[tile_spec, kv_spec, kv_spec, tile_spec, fr_spec, ANY_SPEC, ANY_SPEC],
        out_specs=[tile_spec, out_kv, out_kv, fr_spec],
        out_shape=[_sds((s, e), BF16), _sds((s, e), BF16), _sds((s, e), BF16), fr_shape],
        scratch_shapes=[acc, acc, acc, acc, pltpu.SemaphoreType.DMA((2,))],
        compiler_params=_cp("arbitrary", "arbitrary"),
    )(q, k, v, do, frame_t, *prev)


def _in_bwd(dl, dr, w, h, g, scale, dres, name):
    s, d = h.shape
    nj, _, wn = w.shape
    half = nj // 2
    e = half * wn
    tm = _row_tile(s, 512)

    def body(dl_ref, dr_ref, w_ref, h_ref, g_ref, sc_ref, res_ref, dh_ref, st_ref):
        @pl.when(pl.program_id(0) == 0)
        def _():
            st_ref[...] = jnp.zeros_like(st_ref)

        du = jnp.zeros((tm, d), F32)
        for j in range(nj):
            src = dl_ref if j < half else dr_ref
            jj = j % half
            du = du + lax.dot_general(src[:, jj * wn:(jj + 1) * wn], w_ref[j], NT, preferred_element_type=F32)
        gg = g_ref[...]
        rs, xhat, n, _ = _norm_mod(h_ref[...], gg, 0.0, 0.0)
        dn = du * (1.0 + sc_ref[...])
        st_ref[0:1, :] += jnp.sum(du, axis=0, keepdims=True)
        st_ref[1:2, :] += jnp.sum(du * n, axis=0, keepdims=True)
        st_ref[2:3, :] += jnp.sum(dn * xhat, axis=0, keepdims=True)
        dxh = dn * gg
        dh_ref[...] = rs * (dxh - xhat * jnp.mean(dxh * xhat, axis=-1, keepdims=True)) + res_ref[...]

    return pl.pallas_call(
        body, name=name, grid=(s // tm,),
        in_specs=[pl.BlockSpec((tm, e), lambda i: (i, 0)), pl.BlockSpec((tm, e), lambda i: (i, 0)),
                  pl.BlockSpec((nj, d, wn), lambda i: (0, 0, 0)),
                  pl.BlockSpec((tm, d), lambda i: (i, 0)), _vec(d), _vec(d), pl.BlockSpec((tm, d), lambda i: (i, 0))],
        out_specs=[pl.BlockSpec((tm, d), lambda i: (i, 0)), pl.BlockSpec((8, d), lambda i: (0, 0))],
        out_shape=[_sds((s, d), F32), _sds((8, d), F32)],
        compiler_params=_cp("arbitrary"),
    )(dl, dr, w, h, g, scale, dres)


def _tn_matmul(x, ys, xw, yw, ymap, nb, out_shape, stage_shape, out_at, name):
    s, xfull = x.shape
    ts = _row_tile(s, 2048)
    ys = list(ys)
    half = nb // len(ys)
    nk = s // ts

    def body(*refs):
        x_ref, y_refs = refs[0], refs[1:1 + len(ys)]
        o_hbm, xt_ref, acc_ref, stage_ref, sem = refs[1 + len(ys):]
        k, b = pl.program_id(0), pl.program_id(1)

        @pl.when(b == 0)
        def _():
            xt_ref[...] = x_ref[...].T

        for n, y_ref in enumerate(y_refs):
            @pl.when((b >= n * half) & (b < (n + 1) * half))
            def _():
                xt = xt_ref[...] if xw == xfull else xt_ref[pl.ds(pl.multiple_of(b * xw, xw), xw), :]
                part = jnp.dot(xt, y_ref[...], preferred_element_type=F32)

                @pl.when(k == 0)
                def _():
                    acc_ref[b] = part

                @pl.when(k > 0)
                def _():
                    acc_ref[b] += part

        @pl.when(k == nk - 1)
        def _():
            stage_ref[...] = acc_ref[b].astype(BF16).reshape(stage_shape)
            cps = [pltpu.make_async_copy(src, dst, sem.at[i]) for i, (src, dst) in enumerate(out_at(o_hbm, stage_ref, b))]
            for cp in cps:
                cp.start()
            for cp in cps:
                cp.wait()

    in_specs = [pl.BlockSpec((ts, xfull), lambda k, b: (k, 0))]
    for n in range(len(ys)):
        in_specs.append(pl.BlockSpec(
            (ts, yw), lambda k, b, n=n: (k, ymap(jnp.clip(b - n * half, 0, half - 1)))))
    return pl.pallas_call(
        body, name=name, grid=(nk, nb), in_specs=in_specs,
        out_specs=HBM_SPEC, out_shape=_sds(out_shape, BF16),
        scratch_shapes=[pltpu.VMEM((xfull, ts), BF16), pltpu.VMEM((nb, xw, yw), F32), pltpu.VMEM(stage_shape, BF16),
                        pltpu.SemaphoreType.DMA((2,))],
        compiler_params=_cp("arbitrary", "arbitrary"),
    )(x, *ys)


def _grad_w_in(u, dl, dr, name):
    d = u.shape[1]
    wn = 2 * dl.shape[1] // N_DEV
    halves = lambda o, st, b: [(st.at[:, pl.ds(0, wn)], o.at[2 * b]), (st.at[:, pl.ds(wn, wn)], o.at[2 * b + 1])]
    return _tn_matmul(u, (dl, dr), d, 2 * wn, lambda b: b, N_DEV // 2, (N_DEV, d, wn), (d, 2 * wn), halves, name)


def _grad_w_out(gated, dy, name):
    s, e = gated.shape
    d = dy.shape[1]
    ts = _row_tile(s, 1024)
    nk = s // ts

    def body(x_ref, y_ref, o_ref, acc_ref):
        k = pl.program_id(0)
        part = lax.dot_general(x_ref[...], y_ref[...], TN, preferred_element_type=F32)

        @pl.when(k == 0)
        def _():
            acc_ref[...] = part

        @pl.when(k > 0)
        def _():
            acc_ref[...] += part

        @pl.when(k == nk - 1)
        def _():
            o_ref[...] = acc_ref[...].astype(BF16).reshape(o_ref.shape)

    return pl.pallas_call(
        body, name=name, grid=(nk,),
        in_specs=[pl.BlockSpec((ts, e), lambda k: (k, 0)), pl.BlockSpec((ts, d), lambda k: (k, 0))],
        out_specs=pl.BlockSpec((N_DEV, e // N_DEV, d), lambda k: (0, 0, 0)),
        out_shape=_sds((N_DEV, e // N_DEV, d), BF16),
        scratch_shapes=[pltpu.VMEM((e, d), F32)],
        compiler_params=_cp("arbitrary"),
    )(gated, dy)


def _grad_w_group(pooled, dm, name):
    s, e = pooled.shape
    ng = len(POOL_WINDOWS)
    gw = e // ng
    ts = _row_tile(s, 2048)
    nk = s // ts

    def body(x_ref, y_ref, o_ref, acc_ref):
        k = pl.program_id(0)
        for g in range(ng):
            cols = slice(g * gw, (g + 1) * gw)
            part = lax.dot_general(x_ref[:, cols], y_ref[:, cols], TN, preferred_element_type=F32)

            @pl.when(k == 0)
            def _():
                acc_ref[g] = part

            @pl.when(k > 0)
            def _():
                acc_ref[g] += part

        @pl.when(k == nk - 1)
        def _():
            for g in range(ng):
                o_ref[:, g] = acc_ref[g].astype(BF16).reshape(N_DEV, gw // N_DEV, gw)

    return pl.pallas_call(
        body, name=name, grid=(nk,),
        in_specs=[pl.BlockSpec((ts, e), lambda k: (k, 0)), pl.BlockSpec((ts, e), lambda k: (k, 0))],
        out_specs=pl.BlockSpec((N_DEV, ng, gw // N_DEV, gw), lambda k: (0, 0, 0, 0)),
        out_shape=_sds((N_DEV, ng, gw // N_DEV, gw), BF16),
        scratch_shapes=[pltpu.VMEM((ng, gw, gw), F32)],
        compiler_params=_cp("arbitrary"),
    )(pooled, dm)


def _adamw(staged, w, m, v, name):
    shape = w.shape
    nl = len(staged)
    n = staged[0].shape[0]
    cdim = shape[-1]
    total = 1
    for a in shape[:-1]:
        total *= a
    rows = total // nl
    sts = [st.reshape(n, rows, cdim) for st in staged]
    tr = rows if rows * cdim <= 128 * 1024 else max(8, (128 * 1024 // cdim) // 8 * 8)
    while rows % tr:
        tr -= 8
    nblk = rows // tr

    def body(*refs):
        s_refs = refs[:nl]
        w_ref, m_ref, v_ref, g_ref, d_ref, mo_ref, vo_ref = refs[nl:]
        for ll in range(nl):
            @pl.when(pl.program_id(0) == ll)
            def _():
                g = s_refs[ll][0].astype(F32)
                for j in range(1, n):
                    g = g + s_refs[ll][j].astype(F32)
                mn = ADAM_B1 * m_ref[...] + (1.0 - ADAM_B1) * g
                vn = ADAM_B2 * v_ref[...] + (1.0 - ADAM_B2) * (g * g)
                m_hat = mn / (1.0 - ADAM_B1 ** ADAM_STEP)
                v_hat = vn / (1.0 - ADAM_B2 ** ADAM_STEP)
                g_ref[...] = g
                d_ref[...] = -ADAM_LR * (m_hat / (jnp.sqrt(v_hat) + ADAM_EPS) + ADAM_WD * w_ref[...])
                mo_ref[...] = mn
                vo_ref[...] = vn

    blk = pl.BlockSpec((None, tr, cdim), lambda l, i: (l, i, 0))
    st_specs = [pl.BlockSpec((n, tr, cdim), lambda l, i, ll=ll: (0, jnp.clip(i + (l - ll) * nblk, 0, nblk - 1), 0))
                for ll in range(nl)]
    outs = pl.pallas_call(
        body, name=name, grid=(nl, nblk),
        in_specs=st_specs + [blk, blk, blk],
        out_specs=[blk] * 4, out_shape=[_sds((nl, rows, cdim), F32)] * 4,
        compiler_params=_cp("arbitrary", "arbitrary"),
    )(*sts, w.reshape(nl, rows, cdim), m.reshape(nl, rows, cdim), v.reshape(nl, rows, cdim))
    return [o.reshape(shape) for o in outs]


def _pack(parts, total):
    flat = jnp.concatenate([p.reshape(-1) for p in parts])
    return jnp.pad(flat, (0, total - flat.shape[0])).reshape(1, total)


def kernel(x, c, ada_w, ada_b, norm_g, a_w_in, a_w_group, a_scale, a_w_out, kv_norm_g, kv_ada_w, kv_ada_b, w_kv, b_w_in, b_rel_bias, b_w_out, final_g, loss_target, m_ada_w, m_ada_b, m_norm_g, m_a_w_in, m_a_w_group, m_a_scale, m_a_w_out, m_kv_norm_g, m_kv_ada_w, m_kv_ada_b, m_w_kv, m_b_w_in, m_b_rel_bias, m_b_w_out, m_final_g, v_ada_w, v_ada_b, v_norm_g, v_a_w_in, v_a_w_group, v_a_scale, v_a_w_out, v_kv_norm_g, v_kv_ada_w, v_kv_ada_b, v_w_kv, v_b_w_in, v_b_rel_bias, v_b_w_out, v_final_g):
    s, d = x.shape[1], x.shape[2]
    depth, _, wa = ada_w.shape
    wk = kv_ada_w.shape[1]
    n_a, n_b = a_w_in.shape[0], b_w_in.shape[0]
    e = a_w_out.shape[1] * N_DEV
    nrel = b_rel_bias.shape[-1]
    me = 4 * lax.axis_index("x") + 2 * lax.axis_index("y") + lax.axis_index("c")
    h0 = x[0]
    target = loss_target[0]

    bf = lambda a: a.astype(BF16)
    tie = lambda val, tok: lax.optimization_barrier((val, tok))[0]

    c_all = _exchange([c.reshape(1, 1, d)], [True], "gather_c")[0].reshape(N_DEV, d)
    ada_b_loc = lax.dynamic_slice(ada_b, (0, me * wa), (depth, wa)).reshape(depth, 1, wa)
    kv_ada_b_loc = lax.dynamic_slice(kv_ada_b, (me * wk,), (wk,)).reshape(1, wk)
    mods_cols = _mods_fwd(c_all, ada_w, ada_b_loc, kv_ada_w, kv_ada_b_loc)
    nm = depth * wa + wk
    mods_me, a_scale_all = _exchange([mods_cols.reshape(N_DEV, 1, nm), a_scale], [False, True], "exchange_mods")
    groups = [[bf(a_w_in[l]), bf(a_w_group[l]), bf(a_w_out[l])] for l in range(n_a)]
    groups[0] = groups[0][1:]
    groups.append([bf(w_kv), bf(b_w_in[0]), bf(b_w_out[0])])
    groups += [[bf(b_w_in[l]), bf(b_w_out[l])] for l in range(1, n_b)]
    handles = []
    first_handle, token = _push_start([bf(a_w_in[0])], [True], mods_me, "gather_start_first")
    for gi, grp in enumerate(groups):
        hd, token = _push_start(grp, [True] * len(grp), token, f"gather_start_{gi}")
        handles.append(hd)
    mods_me = mods_me.reshape(N_DEV, nm)
    mods = mods_me[:, :depth * wa].reshape(N_DEV, depth, wa).transpose(1, 0, 2).reshape(depth, 3, 1, d)
    kv_mods = mods_me[:, depth * wa:].reshape(2, 1, d)
    a_scale_full = a_scale_all.transpose(1, 0, 2).reshape(n_a, 1, e)
    ones_e = jnp.ones((1, e), F32)

    saved = []
    h = h0
    k = v = hk = wkv = None
    frames = [_bias_frames(tie(jnp.pad(b_rel_bias[bi], ((0, 0), (0, REL_PAD - nrel))), token).reshape(N_HEADS, 1, REL_PAD))
              for bi in range(n_b)]
    after = frames[-1][1]
    for layer in range(depth):
        shift, scale, gate = mods[layer, 0], mods[layer, 1], mods[layer, 2]
        g = norm_g[layer].reshape(1, d)
        if layer == 0:
            w_in, = _push_wait(first_handle, after, "gather_wait_first")
            u, left, z = _inproj_fwd(h, g, shift, scale, w_in, "inproj_fwd")
            w_group, w_out = _push_wait(handles[0], u, "gather_wait_0")
        elif layer < n_a:
            w_in, w_group, w_out = _push_wait(handles[layer], after, f"gather_wait_{layer}")
            u, left, z = _inproj_fwd(h, g, shift, scale, w_in, "inproj_fwd")
        if layer < n_a:
            pooled, mixed = _pool_fwd(left, w_group)
            gated, y, hn = _gate_out_fwd(mixed, z, a_scale_full[layer], w_out, h, gate, "gate_out_fwd")
            saved.append(dict(h=h, u=u, left=left, z=z, pooled=pooled, m=mixed, gated=gated, y=y,
                              w_in=w_in, w_group=w_group, w_out=w_out))
        else:
            bi = layer - n_a
            got = _push_wait(handles[layer], after, f"gather_wait_{layer}")
            if bi == 0:
                wkv, w_in, w_out = got
                hk, k, v = _inproj_fwd(h, kv_norm_g.reshape(1, d), kv_mods[0], kv_mods[1], wkv, "inproj_kv", pad=LEFT)
            else:
                w_in, w_out = got
            u, left, z = _inproj_fwd(h, g, shift, scale, w_in, "inproj_fwd")
            frame, frame_t = frames[bi]
            att = _attn_fwd(left, k, v, frame)
            gated, y, hn = _gate_out_fwd(att, z, ones_e, w_out, h, gate, "gate_out_fwd")
            saved.append(dict(h=h, u=u, left=left, z=z, m=att, gated=gated, y=y, frame_t=frame_t,
                              w_in=w_in, w_out=w_out))
        h = hn
        after = h
    dh, d_final_g, loss_part = _final_loss(h, final_g.reshape(1, d), target)

    d_mods = [None] * depth
    d_norm_g = [None] * depth
    d_rel = [None] * n_b
    d_ascale = [None] * n_a
    assert n_b == 2
    dk = dv = None
    tie = lambda val, tok: lax.optimization_barrier((val, tok))[0]
    grad_handles = [None] * (depth + 1)
    for layer in range(depth - 1, n_a - 1, -1):
        bi = layer - n_a
        sv = saved[layer]
        scale, gate = mods[layer, 1], mods[layer, 2]
        dy, datt, dz, d_gate, _ = _out_bwd(dh, sv["y"], gate, sv["w_out"], sv["m"], sv["z"], ones_e, "out_bwd")
        dq, dk, dv, dframe_t = _attn_bwd(sv["left"], k, v, datt, sv["frame_t"], None if bi == n_b - 1 else (dk, dv))
        d_rel[bi] = _bias_bwd(dframe_t).reshape(N_HEADS, REL_PAD)[:, :nrel]
        g_out = _grad_w_out(sv["gated"], dy, "grad_w_out")
        g_in = _grad_w_in(sv["u"], dq, dz, "grad_w_in")
        grad_handles[layer], tok = _push_start([g_in, g_out], [False, False], d_gate, f"grads_start_{layer}")
        dh, st = _in_bwd(tie(dq, tok), dz, sv["w_in"], sv["h"], norm_g[layer].reshape(1, d), scale, dh, "in_bwd")
        d_mods[layer] = jnp.concatenate([st[0], st[1], d_gate[0]])
        d_norm_g[layer] = st[2]
    dkb, dvb = dk, dv
    h_kv = saved[n_a]["h"]
    g_w_kv = _grad_w_in(hk, dkb, dvb, "grad_w_in")
    grad_handles[depth], tok = _push_start([g_w_kv], [False], dkb, "grads_start_kv")
    dh, st = _in_bwd(tie(dkb, tok), dvb, wkv, h_kv, kv_norm_g.reshape(1, d), kv_mods[1], dh, "in_bwd")
    d_kv_mods = jnp.concatenate([st[0], st[1]])
    d_kv_norm_g = st[2]
    for layer in range(n_a - 1, -1, -1):
        sv = saved[layer]
        scale, gate = mods[layer, 1], mods[layer, 2]
        dy, dm, dz, d_gate, da = _out_bwd(dh, sv["y"], gate, sv["w_out"], sv["m"], sv["z"], a_scale_full[layer],
                                          "out_bwd")
        d_ascale[layer] = da.reshape(N_DEV, e // N_DEV)
        g_group = _grad_w_group(sv["pooled"], dm, "grad_w_group")
        g_out = _grad_w_out(sv["gated"], dy, "grad_w_out")
        if layer == 0:
            last_handle, tok = _push_start([g_group, g_out], [False] * 2, d_gate, "grads_start_last")
            dm = tie(dm, tok)
        dval = _pool_bwd(dm, sv["w_group"])
        g_in = _grad_w_in(sv["u"], dval, dz, "grad_w_in")
        late = [g_in] if layer == 0 else [g_in, g_group, g_out]
        grad_handles[layer], tok = _push_start(late, [False] * len(late), d_gate, f"grads_start_{layer}")
        dh, st = _in_bwd(tie(dval, tok), dz, sv["w_in"], sv["h"], norm_g[layer].reshape(1, d), scale, dh, "in_bwd")
        d_mods[layer] = jnp.concatenate([st[0], st[1], d_gate[0]])
        d_norm_g[layer] = st[2]
    grad_x = dh.reshape(1, s, d)

    d_mods = jnp.stack(d_mods)
    dm_slots = jnp.concatenate(
        [d_mods.reshape(depth, N_DEV, wa).transpose(1, 0, 2).reshape(N_DEV, depth * wa), d_kv_mods.reshape(N_DEV, wk)],
        axis=1).reshape(N_DEV, 1, nm)
    small_parts = [d_mods, d_kv_mods, jnp.stack(d_norm_g), d_kv_norm_g, d_final_g, jnp.stack(d_rel), loss_part[0, :1]]
    n_small = sum(int(p.size) for p in small_parts)
    n_small_pad = -(-n_small // LANES) * LANES
    small = _pack(small_parts, n_small_pad).reshape(1, 1, n_small_pad)
    d_ascale_slots = jnp.stack(d_ascale, axis=1)
    s_a_scale, dm_cols, small_all = _exchange([d_ascale_slots, dm_slots, small], [False, False, True],
                                              "exchange_small")

    g_ada_w, g_kv_ada_w = _mods_bwd(c_all, dm_cols.reshape(N_DEV, nm), depth, wa, wk)
    res = {}
    res["ada_w"] = _adamw([g_ada_w[None]], ada_w, m_ada_w, v_ada_w, "adamw")
    res["kv_ada_w"] = _adamw([g_kv_ada_w[None]], kv_ada_w, m_kv_ada_w, v_kv_ada_w, "adamw")
    res["a_scale"] = _adamw([s_a_scale[:, l] for l in range(n_a)], a_scale, m_a_scale, v_a_scale, "adamw")
    after = res["ada_w"][1]
    s_b = [None] * n_b
    for layer in range(depth - 1, n_a - 1, -1):
        s_b[layer - n_a] = _push_wait(grad_handles[layer], after, f"grads_wait_{layer}")
        after = s_b[layer - n_a][0]
    res["b_w_in"] = _adamw([sb[0] for sb in s_b], b_w_in, m_b_w_in, v_b_w_in, "adamw")
    res["b_w_out"] = _adamw([sb[1] for sb in s_b], b_w_out, m_b_w_out, v_b_w_out, "adamw")
    s_w_kv, = _push_wait(grad_handles[depth], res["b_w_out"][1], "grads_wait_kv")
    res["w_kv"] = _adamw([s_w_kv], w_kv, m_w_kv, v_w_kv, "adamw")
    after = res["w_kv"][1]
    s_a = [None] * n_a
    for layer in range(n_a - 1, 0, -1):
        s_a[layer] = _push_wait(grad_handles[layer], after, f"grads_wait_{layer}")
        after = s_a[layer][0]
    s_group0, s_out0 = _push_wait(last_handle, after, "grads_wait_last")
    s_a[0] = [None, s_group0, s_out0]
    res["a_w_group"] = _adamw([sa[1] for sa in s_a], a_w_group, m_a_w_group, v_a_w_group, "adamw")
    res["a_w_out"] = _adamw([sa[2] for sa in s_a], a_w_out, m_a_w_out, v_a_w_out, "adamw")
    s_a[0][0], = _push_wait(grad_handles[0], res["a_w_out"][1], "grads_wait_0")
    res["a_w_in"] = _adamw([sa[0] for sa in s_a], a_w_in, m_a_w_in, v_a_w_in, "adamw")
    small_names = ["ada_b", "kv_ada_b", "norm_g", "kv_norm_g", "final_g", "b_rel_bias"]
    small_w = dict(ada_b=ada_b, kv_ada_b=kv_ada_b, norm_g=norm_g, kv_norm_g=kv_norm_g, final_g=final_g,
                   b_rel_bias=b_rel_bias)
    small_m = dict(ada_b=m_ada_b, kv_ada_b=m_kv_ada_b, norm_g=m_norm_g, kv_norm_g=m_kv_norm_g, final_g=m_final_g,
                   b_rel_bias=m_b_rel_bias)
    small_v = dict(ada_b=v_ada_b, kv_ada_b=v_kv_ada_b, norm_g=v_norm_g, kv_norm_g=v_kv_norm_g, final_g=v_final_g,
                   b_rel_bias=v_b_rel_bias)
    sw = _pack([small_w[n] for n in small_names], n_small_pad)
    smm = _pack([small_m[n] for n in small_names], n_small_pad)
    svv = _pack([small_v[n] for n in small_names] + [jnp.ones((n_small_pad - n_small + 1,), F32)], n_small_pad)
    small_out = _adamw([small_all.reshape(N_DEV, 1, n_small_pad)], sw, smm, svv, "adamw")
    off = 0
    for n in small_names:
        size = int(small_w[n].size)
        res[n] = [o[0, off:off + size].reshape(small_w[n].shape) for o in small_out]
        off += size
    loss = small_out[0][0, n_small - 1]

    order = ["ada_w", "ada_b", "norm_g", "a_w_in", "a_w_group", "a_scale", "a_w_out", "kv_norm_g", "kv_ada_w",
             "kv_ada_b", "w_kv", "b_w_in", "b_rel_bias", "b_w_out", "final_g"]
    outs = [loss, grad_x]
    for part in range(4):
        outs += [res[n][part] for n in order]
    return tuple(outs)
```

```python
import functools

import jax
import jax.numpy as jnp
from jax import lax
from jax.experimental import pallas as pl
from jax.experimental.pallas import tpu as pltpu

F32 = jnp.float32
BF16 = jnp.bfloat16

N_DEV = 8
CHUNK = 64
LEFT_CHUNKS = 8
N_HEADS = 16
POOL_WINDOWS = (2, 4, 8, 16)
REL_CLIP = 128
EPS = 1e-6
ADAM_LR = 0.001
ADAM_B1 = 0.9
ADAM_B2 = 0.999
ADAM_EPS = 1e-08
ADAM_WD = 0.01
ADAM_STEP = 10

LANES = 128
TQ = 2 * CHUNK
LEFT = LEFT_CHUNKS * CHUNK
BAND = LEFT + TQ
FRAME_PAD = BAND + TQ
NFRAME = LEFT // TQ + 1
POOL_HALO = 16
REL_PAD = 384
NEG = -1e30
ATTN_GROUP_FWD = 64
ATTN_GROUP = 32
ATTN_GROUP_ACC = 16
ATTN_BATCH = 8
ATTN_BATCH_BWD = 8
VMEM_LIMIT = 56 * 1024 * 1024

NT = (((1,), (1,)), ((), ()))
TN = (((0,), (0,)), ((), ()))

HBM_SPEC = pl.BlockSpec(memory_space=pltpu.HBM)
ANY_SPEC = pl.BlockSpec(memory_space=pl.ANY)
SEM_SPEC = pl.BlockSpec(memory_space=pltpu.SEMAPHORE)


def _cp(*sem):
    return pltpu.CompilerParams(dimension_semantics=sem or None, vmem_limit_bytes=VMEM_LIMIT)


def _sds(shape, dtype):
    return jax.ShapeDtypeStruct(tuple(shape), dtype)


def _row_tile(s, want):
    return min(want, s)


def _vec(d):
    return pl.BlockSpec((1, d), lambda *_: (0, 0))


def _sigmoid(z):
    return 1.0 / (1.0 + jnp.exp(-z))


def _exchange(xs, gather, name):
    n = len(xs)
    out_shapes = []
    for x, ga in zip(xs, gather):
        out_shapes.append(_sds((N_DEV,) + (x.shape if ga else x.shape[1:]), x.dtype))

    def body(*refs):
        ins, outs = refs[:n], refs[n:2 * n]
        send_sems, recv_sems, local_sems = refs[2 * n:]
        mx, my, mc = lax.axis_index("x"), lax.axis_index("y"), lax.axis_index("c")
        me = 4 * mx + 2 * my + mc
        local = []
        for k in range(n):
            src = ins[k] if gather[k] else ins[k].at[me]
            cp = pltpu.make_async_copy(src, outs[k].at[me], local_sems.at[k])
            cp.start()
            local.append(cp)
        sends, recvs = [], []
        for r in range(1, N_DEV):
            px, py, pc = (mx + (r >> 2)) % 2, (my + ((r >> 1) & 1)) % 2, (mc + (r & 1)) % 2
            peer = 4 * px + 2 * py + pc
            for k in range(n):
                sem = k * (N_DEV - 1) + r - 1
                src = ins[k] if gather[k] else ins[k].at[peer]
                send = pltpu.make_async_remote_copy(
                    src_ref=src, dst_ref=outs[k].at[me], send_sem=send_sems.at[sem], recv_sem=recv_sems.at[sem],
                    device_id=(px, py, pc), device_id_type=pl.DeviceIdType.MESH)
                send.start()
                sends.append(send)
                recvs.append(pltpu.make_async_remote_copy(
                    src_ref=src, dst_ref=outs[k].at[peer], send_sem=send_sems.at[sem], recv_sem=recv_sems.at[sem],
                    device_id=(px, py, pc), device_id_type=pl.DeviceIdType.MESH))
        for cp in recvs:
            cp.wait_recv()
        for cp in sends:
            cp.wait_send()
        for cp in local:
            cp.wait()

    return pl.pallas_call(
        body, name=name, out_shape=out_shapes,
        in_specs=[HBM_SPEC] * n, out_specs=[HBM_SPEC] * n,
        scratch_shapes=[pltpu.SemaphoreType.DMA((n * (N_DEV - 1),)), pltpu.SemaphoreType.DMA((n * (N_DEV - 1),)),
                        pltpu.SemaphoreType.DMA((n,))],
    )(*xs)


def _peer(mx, my, mc, r):
    px, py, pc = (mx + (r >> 2)) % 2, (my + ((r >> 1) & 1)) % 2, (mc + (r & 1)) % 2
    return (px, py, pc), 4 * px + 2 * py + pc


def _push_start(xs, gather, dep, name):
    n = len(xs)
    nsem = n * (N_DEV - 1)
    me = 4 * lax.axis_index("x") + 2 * lax.axis_index("y") + lax.axis_index("c")
    lands = []
    for x, ga in zip(xs, gather):
        own = x[None] if ga else lax.dynamic_index_in_dim(x, me, 0, keepdims=True)
        empty = lax.empty((N_DEV,) + own.shape[1:], x.dtype)
        lands.append(lax.dynamic_update_slice(empty, own, (me,) + (0,) * (own.ndim - 1)))

    def body(*refs):
        ins, lands_in = refs[:n], refs[n:2 * n]
        send_sems, recv_sems = refs[2 * n + 1], refs[2 * n + 2]
        token = refs[-1]
        mx, my, mc = lax.axis_index("x"), lax.axis_index("y"), lax.axis_index("c")
        mine = 4 * mx + 2 * my + mc
        for k in range(n):
            for r in range(1, N_DEV):
                dev, peer = _peer(mx, my, mc, r)
                sem = k * (N_DEV - 1) + r - 1
                pltpu.make_async_remote_copy(
                    src_ref=ins[k] if gather[k] else ins[k].at[peer], dst_ref=lands_in[k].at[mine],
                    send_sem=send_sems.at[sem], recv_sem=recv_sems.at[sem],
                    device_id=dev, device_id_type=pl.DeviceIdType.MESH).start()
        token[...] = jnp.zeros_like(token)

    hbm = lambda a: pltpu.HBM(a.shape, a.dtype)
    outs = pl.pallas_call(
        body, name=name,
        out_shape=(pltpu.SemaphoreType.DMA((nsem,)), pltpu.SemaphoreType.DMA((nsem,)),
                   *[hbm(x) for x in xs], *[hbm(a) for a in lands], _sds((8, LANES), F32)),
        in_specs=[HBM_SPEC] * (2 * n) + [ANY_SPEC],
        out_specs=(SEM_SPEC, SEM_SPEC, *[HBM_SPEC] * (2 * n), pl.BlockSpec(memory_space=pltpu.VMEM)),
        input_output_aliases={k: 2 + k for k in range(2 * n)},
        compiler_params=pltpu.CompilerParams(has_side_effects=pltpu.SideEffectType.DATAFLOW_SIDE_EFFECTING),
    )(*[pltpu.with_memory_space_constraint(x, pltpu.HBM) for x in xs],
      *[pltpu.with_memory_space_constraint(a, pltpu.HBM) for a in lands], dep)
    return (outs[0], outs[1], outs[2:2 + n], outs[2 + n:2 + 2 * n], tuple(gather)), outs[-1]


def _push_wait(handle, after, name):
    send_sems, recv_sems, srcs, lands, gather = handle
    n = len(srcs)

    def body(*refs):
        ins, lands_in = refs[:n], refs[n:2 * n]
        send_sems, recv_sems = refs[2 * n], refs[2 * n + 1]
        mx, my, mc = lax.axis_index("x"), lax.axis_index("y"), lax.axis_index("c")
        for k in range(n):
            for r in range(1, N_DEV):
                dev, peer = _peer(mx, my, mc, r)
                sem = k * (N_DEV - 1) + r - 1
                cp = pltpu.make_async_remote_copy(
                    src_ref=ins[k] if gather[k] else ins[k].at[peer], dst_ref=lands_in[k].at[peer],
                    send_sem=send_sems.at[sem], recv_sem=recv_sems.at[sem],
                    device_id=dev, device_id_type=pl.DeviceIdType.MESH)
                cp.wait_send()
                cp.wait_recv()

    hbm = lambda a: pltpu.HBM(a.shape, a.dtype)
    outs = pl.pallas_call(
        body, name=name,
        out_shape=(*[hbm(x) for x in srcs], *[hbm(a) for a in lands]),
        in_specs=[HBM_SPEC] * (2 * n) + [SEM_SPEC, SEM_SPEC, ANY_SPEC],
        out_specs=tuple([HBM_SPEC] * (2 * n)),
        input_output_aliases={k: k for k in range(2 * n)},
        compiler_params=pltpu.CompilerParams(has_side_effects=pltpu.SideEffectType.DATAFLOW_SIDE_EFFECTING),
    )(*srcs, *lands, send_sems, recv_sems, after)
    return list(outs[n:])


def _mods_fwd(c_all, ada_w, ada_b_loc, kv_ada_w, kv_ada_b_loc):
    nl, d, wa = ada_w.shape
    wk = kv_ada_w.shape[1]

    def body(c_ref, w_ref, b_ref, kw_ref, kb_ref, o_ref):
        c = c_ref[...]
        ca = c * _sigmoid(c)
        for l in range(nl):
            o_ref[:, l * wa:(l + 1) * wa] = jnp.dot(
                ca, w_ref[l], preferred_element_type=F32, precision=lax.Precision.HIGHEST) + b_ref[l]
        o_ref[:, nl * wa:] = jnp.dot(
            ca, kw_ref[...], preferred_element_type=F32, precision=lax.Precision.HIGHEST) + kb_ref[...]

    return pl.pallas_call(body, name="mods_fwd", out_shape=_sds((N_DEV, nl * wa + wk), F32),
                          compiler_params=_cp())(c_all, ada_w, ada_b_loc, kv_ada_w, kv_ada_b_loc)


def _mods_bwd(c_all, dm, nl, wa, wk):
    d = c_all.shape[1]

    def body(c_ref, d_ref, gw_ref, gk_ref):
        c = c_ref[...]
        ca = c * _sigmoid(c)
        for l in range(nl):
            gw_ref[l] = lax.dot_general(ca, d_ref[:, l * wa:(l + 1) * wa], TN,
                                        preferred_element_type=F32, precision=lax.Precision.HIGHEST)
        gk_ref[...] = lax.dot_general(ca, d_ref[:, nl * wa:], TN,
                                      preferred_element_type=F32, precision=lax.Precision.HIGHEST)

    return pl.pallas_call(body, name="mods_bwd", out_shape=[_sds((nl, d, wa), F32), _sds((d, wk), F32)],
                          compiler_params=_cp())(c_all, dm)


def _norm_mod(hf, g, shift, scale):
    rs = lax.rsqrt(jnp.mean(hf * hf, axis=-1, keepdims=True) + EPS)
    xhat = hf * rs
    n = xhat * g
    return rs, xhat, n, n * (1.0 + scale) + shift


def _inproj_fwd(h, g, shift, scale, w, name, pad=0):
    s, d = h.shape
    nj, _, wn = w.shape
    half = nj // 2
    tm = _row_tile(s, 512)
    assert pad in (0, tm)
    pb = pad // tm

    def body(h_ref, g_ref, sh_ref, sc_ref, w_ref, u_ref, l_ref, r_ref):
        _, _, _, u = _norm_mod(h_ref[...], g_ref[...], sh_ref[...], sc_ref[...])
        ub = u.astype(BF16)
        u_ref[...] = ub
        for j in range(nj):
            o_ref = l_ref if j < half else r_ref
            jj = j % half
            o_ref[:, jj * wn:(jj + 1) * wn] = jnp.dot(ub, w_ref[j], preferred_element_type=F32).astype(BF16)
        if pb:
            @pl.when(pl.program_id(0) == 0)
            def _():
                l_ref[...] = jnp.zeros_like(l_ref)
                r_ref[...] = jnp.zeros_like(r_ref)

    e = half * wn
    src = lambda i: (jnp.maximum(i - pb, 0), 0)
    return pl.pallas_call(
        body, name=name, grid=(s // tm + pb,),
        in_specs=[pl.BlockSpec((tm, d), src), _vec(d), _vec(d), _vec(d),
                  pl.BlockSpec((nj, d, wn), lambda i: (0, 0, 0))],
        out_specs=[pl.BlockSpec((tm, d), src), pl.BlockSpec((tm, e), lambda i: (i, 0)),
                   pl.BlockSpec((tm, e), lambda i: (i, 0))],
        out_shape=[_sds((s, d), BF16), _sds((s + pad, e), BF16), _sds((s + pad, e), BF16)],
        compiler_params=_cp("arbitrary"),
    )(h, g, shift, scale, w)


def _pool_fwd(val, wg):
    s, e = val.shape
    ng = len(POOL_WINDOWS)
    gw = e // ng
    tm = _row_tile(s, 512)
    hb = tm // POOL_HALO

    def body(v_ref, halo_ref, w_ref, p_ref, m_ref):
        i = pl.program_id(0)
        t = i * tm + lax.broadcasted_iota(jnp.int32, (tm, 1), 0)
        for g, wdw in enumerate(POOL_WINDOWS):
            cols = slice(g * gw, (g + 1) * gw)
            v = v_ref[:, cols].astype(F32)
            halo = jnp.where(i > 0, halo_ref[:, cols].astype(F32), 0.0)
            acc = jnp.concatenate([halo, v], axis=0)
            sh = 1
            while sh < wdw:
                acc = acc + pltpu.roll(acc, sh, 0)
                sh *= 2
            cnt = jnp.minimum(t + 1, wdw).astype(F32)
            pb = (acc[POOL_HALO:, :] / cnt - v).astype(BF16)
            p_ref[:, cols] = pb
            m_ref[:, cols] = jnp.dot(pb, w_ref[:, g].reshape(gw, gw), preferred_element_type=F32).astype(BF16)

    return pl.pallas_call(
        body, name="pool_fwd", grid=(s // tm,),
        in_specs=[pl.BlockSpec((tm, e), lambda i: (i, 0)),
                  pl.BlockSpec((POOL_HALO, e), lambda i: (jnp.maximum(i * hb - 1, 0), 0)),
                  pl.BlockSpec((N_DEV, ng, gw // N_DEV, gw), lambda i: (0, 0, 0, 0))],
        out_specs=[pl.BlockSpec((tm, e), lambda i: (i, 0)), pl.BlockSpec((tm, e), lambda i: (i, 0))],
        out_shape=[_sds((s, e), BF16), _sds((s, e), BF16)],
        compiler_params=_cp("arbitrary"),
    )(val, val, wg)


def _gate_out_fwd(m, z, ascale, w, h, gate, name):
    s, e = m.shape
    d = h.shape[1]
    tm = _row_tile(s, 512)

    def body(m_ref, z_ref, a_ref, w_ref, h_ref, g_ref, gd_ref, ho_ref):
        z = z_ref[...].astype(F32)
        gb = ((m_ref[...].astype(F32) * a_ref[...]) * (z * _sigmoid(z))).astype(BF16)
        gd_ref[...] = gb
        y = jnp.dot(gb, w_ref[...].reshape(e, d), preferred_element_type=F32)
        ho_ref[...] = h_ref[...] + g_ref[...] * y

    return pl.pallas_call(
        body, name=name, grid=(s // tm,),
        in_specs=[pl.BlockSpec((tm, e), lambda i: (i, 0)), pl.BlockSpec((tm, e), lambda i: (i, 0)), _vec(e),
                  pl.BlockSpec((N_DEV, e // N_DEV, d), lambda i: (0, 0, 0)),
                  pl.BlockSpec((tm, d), lambda i: (i, 0)), _vec(d)],
        out_specs=[pl.BlockSpec((tm, e), lambda i: (i, 0)), pl.BlockSpec((tm, d), lambda i: (i, 0))],
        out_shape=[_sds((s, e), BF16), _sds((s, d), F32)],
        compiler_params=_cp("arbitrary"),
    )(m, z, ascale, w, h, gate)


def _rel_onehot(shape, r_axis):
    r = lax.broadcasted_iota(jnp.int32, shape, r_axis)
    j = lax.broadcasted_iota(jnp.int32, shape, 1 - r_axis)
    dist = LEFT - (j - TQ)
    return (jnp.clip(dist, -REL_CLIP, REL_CLIP) + REL_CLIP == r).astype(F32)


def _skew(x, sign):
    row = lax.broadcasted_iota(jnp.int32, x.shape, 0)
    for b in range(TQ.bit_length() - 1):
        amt = (1 << b) if sign > 0 else FRAME_PAD - (1 << b)
        x = jnp.where(((row >> b) & 1) == 1, pltpu.roll(x, amt, 1), x)
    return x


def _bias_frames(rel):
    nh = rel.shape[0]

    def body(r_ref, f_ref, ft_ref):
        fext = jnp.dot(r_ref[...], _rel_onehot((REL_PAD, FRAME_PAD), 0), preferred_element_type=F32,
                       precision=lax.Precision.HIGHEST)
        x = _skew(jnp.broadcast_to(fext, (TQ, FRAME_PAD)), 1)[:, TQ:]
        qc = lax.broadcasted_iota(jnp.int32, (TQ, BAND), 0) // CHUNK
        m = lax.broadcasted_iota(jnp.int32, (TQ, BAND), 1)
        mc = m // CHUNK
        x = jnp.where((mc >= qc) & (mc <= qc + LEFT_CHUNKS), x, NEG)
        for f in range(NFRAME):
            xf = jnp.where(m >= LEFT - f * TQ, x, NEG)
            f_ref[f] = xf
            ft_ref[f] = xf.T

    return pl.pallas_call(
        body, name="bias_frames", grid=(nh,),
        in_specs=[pl.BlockSpec((None, 1, REL_PAD), lambda h: (h, 0, 0))],
        out_specs=[pl.BlockSpec((None, NFRAME, TQ, BAND), lambda h: (h, 0, 0, 0)),
                   pl.BlockSpec((None, NFRAME, BAND, TQ), lambda h: (h, 0, 0, 0))],
        out_shape=[_sds((nh, NFRAME, TQ, BAND), F32), _sds((nh, NFRAME, BAND, TQ), F32)],
        compiler_params=_cp("arbitrary"),
    )(rel)


def _bias_bwd(dft):
    nh = dft.shape[0]

    def body(d_ref, o_ref):
        d = d_ref[0]
        for f in range(1, NFRAME):
            d = d + d_ref[f]
        x = jnp.concatenate([jnp.zeros((TQ, TQ), F32), d.T], axis=1)
        col = jnp.sum(_skew(x, -1), axis=0, keepdims=True)
        o_ref[...] = jnp.dot(col, _rel_onehot((FRAME_PAD, REL_PAD), 1), preferred_element_type=F32,
                             precision=lax.Precision.HIGHEST)

    return pl.pallas_call(
        body, name="bias_bwd", grid=(nh,),
        in_specs=[pl.BlockSpec((None, NFRAME, BAND, TQ), lambda h: (h, 0, 0, 0))],
        out_specs=pl.BlockSpec((None, 1, REL_PAD), lambda h: (h, 0, 0)),
        out_shape=_sds((nh, 1, REL_PAD), F32),
        compiler_params=_cp("arbitrary"),
    )(dft)


def _attn_tiles(s, group):
    nt = s // TQ
    ni = min(group, nt)
    return nt // ni, ni


def _attn_fwd(q, k, v, frame):
    s, e = q.shape
    dh = e // N_HEADS
    ng, ni = _attn_tiles(s, ATTN_GROUP_FWD)
    sm = dh ** -0.5
    nb = min(ATTN_BATCH, ni)

    def body(q_ref, k_ref, v_ref, b_ref, o_ref):
        g0 = pl.program_id(1) * ni
        for tb in range(0, ni, nb):
            ts = range(tb, tb + nb)
            keys = [pl.ds(pl.multiple_of((g0 + t) * TQ, TQ), BAND) for t in ts]
            rows = [pl.ds(t * TQ, TQ) for t in ts]
            scs = [lax.dot_general(q_ref[r, :], k_ref[kk, :], NT, preferred_element_type=F32)
                   for r, kk in zip(rows, keys)]
            ps, ls = [], []
            for t, sc in zip(ts, scs):
                sc = sc * sm + b_ref[jnp.minimum(g0 + t, NFRAME - 1)]
                p = jnp.exp(sc - jnp.max(sc, axis=-1, keepdims=True))
                ls.append(jnp.sum(p, axis=-1, keepdims=True))
                ps.append(p.astype(BF16))
            for r, p, l, kk in zip(rows, ps, ls, keys):
                o = jnp.dot(p, v_ref[kk, :], preferred_element_type=F32)
                o_ref[r, :] = (o / l).astype(BF16)

    return pl.pallas_call(
        body, name="attn_fwd", grid=(N_HEADS, ng),
        in_specs=[pl.BlockSpec((ni * TQ, dh), lambda h, i: (i, h)),
                  pl.BlockSpec((s + LEFT, dh), lambda h, i: (0, h)), pl.BlockSpec((s + LEFT, dh), lambda h, i: (0, h)),
                  pl.BlockSpec((None, NFRAME, TQ, BAND), lambda h, i: (h, 0, 0, 0))],
        out_specs=pl.BlockSpec((ni * TQ, dh), lambda h, i: (i, h)),
        out_shape=_sds((s, e), BF16),
        compiler_params=_cp("arbitrary", "arbitrary"),
    )(q, k, v, frame)


def _final_loss(h, g, target):
    s, d = h.shape
    tm = _row_tile(s, 512)

    def body(h_ref, g_ref, t_ref, dh_ref, dg_ref, l_ref):
        @pl.when(pl.program_id(0) == 0)
        def _():
            dg_ref[...] = jnp.zeros_like(dg_ref)
            l_ref[...] = jnp.zeros_like(l_ref)

        hf = h_ref[...]
        gg = g_ref[...]
        rs = lax.rsqrt(jnp.mean(hf * hf, axis=-1, keepdims=True) + EPS)
        xhat = hf * rs
        diff = xhat * gg - t_ref[...]
        l_ref[...] += 0.5 * jnp.sum(jnp.mean(diff * diff, axis=-1, keepdims=True), axis=0, keepdims=True)
        dout = diff * (1.0 / d)
        dg_ref[...] += jnp.sum(dout * xhat, axis=0, keepdims=True)
        dxh = dout * gg
        dh_ref[...] = rs * (dxh - xhat * jnp.mean(dxh * xhat, axis=-1, keepdims=True))

    return pl.pallas_call(
        body, name="final_loss", grid=(s // tm,),
        in_specs=[pl.BlockSpec((tm, d), lambda i: (i, 0)), _vec(d), pl.BlockSpec((tm, d), lambda i: (i, 0))],
        out_specs=[pl.BlockSpec((tm, d), lambda i: (i, 0)), _vec(d), _vec(LANES)],
        out_shape=[_sds((s, d), F32), _sds((1, d), F32), _sds((1, LANES), F32)],
        compiler_params=_cp("arbitrary"),
    )(h, g, target)


def _out_bwd(dh, gate, w, m, z, ascale, name):
    s, d = dh.shape
    e = m.shape[1]
    tm = _row_tile(s, 512)

    def body(dh_ref, g_ref, w_ref, m_ref, z_ref, a_ref, dhb_ref, dm_ref, dz_ref, da_ref):
        @pl.when(pl.program_id(0) == 0)
        def _():
            da_ref[...] = jnp.zeros_like(da_ref)

        dhf = dh_ref[...]
        dhb_ref[...] = dhf.astype(BF16)
        dyb = (g_ref[...] * dhf).astype(BF16)
        dgated = lax.dot_general(dyb, w_ref[...].reshape(e, d), NT, preferred_element_type=F32)
        z = z_ref[...].astype(F32)
        sig = _sigmoid(z)
        mf = m_ref[...].astype(F32)
        a = a_ref[...]
        dms = dgated * (z * sig)
        da_ref[...] += jnp.sum(dms * mf, axis=0, keepdims=True)
        dm_ref[...] = (dms * a).astype(BF16)
        dz_ref[...] = (dgated * (mf * a) * (sig * (1.0 + z * (1.0 - sig)))).astype(BF16)

    return pl.pallas_call(
        body, name=name, grid=(s // tm,),
        in_specs=[pl.BlockSpec((tm, d), lambda i: (i, 0)), _vec(d),
                  pl.BlockSpec((N_DEV, e // N_DEV, d), lambda i: (0, 0, 0)),
                  pl.BlockSpec((tm, e), lambda i: (i, 0)), pl.BlockSpec((tm, e), lambda i: (i, 0)), _vec(e)],
        out_specs=[pl.BlockSpec((tm, d), lambda i: (i, 0)), pl.BlockSpec((tm, e), lambda i: (i, 0)),
                   pl.BlockSpec((tm, e), lambda i: (i, 0)), _vec(e)],
        out_shape=[_sds((s, d), BF16), _sds((s, e), BF16), _sds((s, e), BF16), _sds((1, e), F32)],
        compiler_params=_cp("arbitrary"),
    )(dh, gate, w, m, z, ascale)


def _pool_bwd(dm, wg):
    s, e = dm.shape
    ng = len(POOL_WINDOWS)
    gw = e // ng
    tm = _row_tile(s, 512)
    hb = tm // POOL_HALO
    nsteps = s // tm

    def body(d_ref, halo_ref, w_ref, o_ref):
        i = pl.program_id(0)
        t = i * tm + lax.broadcasted_iota(jnp.int32, (tm + POOL_HALO, 1), 0)
        for g, wdw in enumerate(POOL_WINDOWS):
            cols = slice(g * gw, (g + 1) * gw)
            dmx = jnp.concatenate([d_ref[:, cols], halo_ref[:, cols]], axis=0)
            dp = lax.dot_general(dmx, w_ref[:, g].reshape(gw, gw), NT, preferred_element_type=F32)
            dp = jnp.where(t < s, dp, 0.0)
            acc = dp / jnp.minimum(t + 1, wdw).astype(F32)
            sh = 1
            while sh < wdw:
                acc = acc + pltpu.roll(acc, tm + POOL_HALO - sh, 0)
                sh *= 2
            o_ref[:, cols] = (acc[:tm, :] - dp[:tm, :]).astype(BF16)

    return pl.pallas_call(
        body, name="pool_bwd", grid=(nsteps,),
        in_specs=[pl.BlockSpec((tm, e), lambda i: (i, 0)),
                  pl.BlockSpec((POOL_HALO, e), lambda i: (jnp.minimum((i + 1) * hb, s // POOL_HALO - 1), 0)),
                  pl.BlockSpec((N_DEV, ng, gw // N_DEV, gw), lambda i: (0, 0, 0, 0))],
        out_specs=pl.BlockSpec((tm, e), lambda i: (i, 0)),
        out_shape=_sds((s, e), BF16),
        compiler_params=_cp("arbitrary"),
    )(dm, dm, wg)


def _attn_bwd(q, k, v, do, frame_t, prev=None):
    s, e = q.shape
    dh = e // N_HEADS
    ng, ni = _attn_tiles(s, ATTN_GROUP if prev is None else ATTN_GROUP_ACC)
    sm = dh ** -0.5
    nleft = LEFT // TQ

    def core(q_ref, k_ref, v_ref, do_ref, b_ref, dq_ref, dk_ref, dv_ref, db_ref, init):
        i = pl.program_id(1)
        g0 = i * ni

        @pl.when(i == 0)
        def _():
            init()
            db_ref[...] = jnp.zeros_like(db_ref)

        keys = [pl.ds(pl.multiple_of((g0 + t) * TQ, TQ), BAND) for t in range(ni)]
        rows = [pl.ds(t * TQ, TQ) for t in range(ni)]
        pbs, dsbs = [], []
        nb = min(ATTN_BATCH_BWD, ni)
        for tb in range(0, ni, nb):
            tiles = range(tb, tb + nb)
            sts = [lax.dot_general(k_ref[keys[t], :], q_ref[rows[t], :], NT, preferred_element_type=F32) for t in tiles]
            dpts = [lax.dot_general(v_ref[keys[t], :], do_ref[rows[t], :], NT, preferred_element_type=F32) for t in tiles]
            dsts = {}
            for t, st, dpt in zip(tiles, sts, dpts):
                st = st * sm + b_ref[jnp.minimum(g0 + t, NFRAME - 1)]
                p = jnp.exp(st - jnp.max(st, axis=0, keepdims=True))
                p = p * (1.0 / jnp.sum(p, axis=0, keepdims=True))
                dst = p * (dpt - jnp.sum(dpt * p, axis=0, keepdims=True))
                pbs.append(p.astype(BF16))
                dsbs.append(dst.astype(BF16))
                dsts[t] = dst
            rest = None
            for t in tiles:
                if t < nleft:
                    db_ref[jnp.minimum(g0 + t, NFRAME - 1)] += dsts[t]
                else:
                    rest = dsts[t] if rest is None else rest + dsts[t]
            if rest is not None:
                db_ref[NFRAME - 1] += rest
            for t in tiles:
                dq = lax.dot_general(dsbs[t], k_ref[keys[t], :], TN, preferred_element_type=F32)
                dq_ref[rows[t], :] = (dq * sm).astype(BF16)
        for r in range(ni + nleft):
            ts = [t for t in range(ni) if 0 <= r - t <= nleft]
            blk = lambda xs: jnp.concatenate([xs[t][(r - t) * TQ:(r - t + 1) * TQ, :] for t in ts], axis=1)
            qrows = slice(ts[0] * TQ, (ts[-1] + 1) * TQ)
            krows = pl.ds(pl.multiple_of((g0 + r) * TQ, TQ), TQ)
            dk_ref[krows, :] += jnp.dot(blk(dsbs), q_ref[qrows, :], preferred_element_type=F32) * sm
            dv_ref[krows, :] += jnp.dot(blk(pbs), do_ref[qrows, :], preferred_element_type=F32)

    tile_spec = pl.BlockSpec((ni * TQ, dh), lambda h, i: (i, h))
    kv_spec = pl.BlockSpec((s + LEFT, dh), lambda h, i: (0, h))
    fr_spec = pl.BlockSpec((None, NFRAME, BAND, TQ), lambda h, i: (h, 0, 0, 0))
    fr_shape = _sds((N_HEADS, NFRAME, BAND, TQ), F32)
    if prev is None:
        def body(q_ref, k_ref, v_ref, do_ref, b_ref, dq_ref, dk_ref, dv_ref, db_ref):
            def init():
                dk_ref[...] = jnp.zeros_like(dk_ref)
                dv_ref[...] = jnp.zeros_like(dv_ref)
            core(q_ref, k_ref, v_ref, do_ref, b_ref, dq_ref, dk_ref, dv_ref, db_ref, init)

        return pl.pallas_call(
            body, name="attn_bwd", grid=(N_HEADS, ng),
            in_specs=[tile_spec, kv_spec, kv_spec, tile_spec, fr_spec],
            out_specs=[tile_spec, kv_spec, kv_spec, fr_spec],
            out_shape=[_sds((s, e), BF16), _sds((s + LEFT, e), F32), _sds((s + LEFT, e), F32), fr_shape],
            compiler_params=_cp("arbitrary", "arbitrary"),
        )(q, k, v, do, frame_t)

    def body_acc(q_ref, k_ref, v_ref, do_ref, b_ref, dkp_hbm, dvp_hbm, dq_ref, dkb_ref, dvb_ref, db_ref,
                 dk_acc, dv_acc, dkp_buf, dvp_buf, sems):
        cols = pl.ds(pl.multiple_of(pl.program_id(0) * dh, dh), dh)
        fetch = [pltpu.make_async_copy(dkp_hbm.at[:, cols], dkp_buf, sems.at[0]),
                 pltpu.make_async_copy(dvp_hbm.at[:, cols], dvp_buf, sems.at[1])]

        def init():
            for cp in fetch:
                cp.start()
            dk_acc[...] = jnp.zeros_like(dk_acc)
            dv_acc[...] = jnp.zeros_like(dv_acc)
        core(q_ref, k_ref, v_ref, do_ref, b_ref, dq_ref, dk_acc, dv_acc, db_ref, init)

        @pl.when(pl.program_id(1) == ng - 1)
        def _():
            for cp in fetch:
                cp.wait()
            dkb_ref[...] = (dk_acc[LEFT:, :] + dkp_buf[LEFT:, :]).astype(BF16)
            dvb_ref[...] = (dv_acc[LEFT:, :] + dvp_buf[LEFT:, :]).astype(BF16)

    out_kv = pl.BlockSpec((s, dh), lambda h, i: (0, h))
    acc = pltpu.VMEM((s + LEFT, dh), F32)
    return pl.pallas_call(
        body_acc, name="attn_bwd_acc", grid=(N_HEADS, ng),
        in_specs=[tile_spec, kv_spec, kv_spec, tile_spec, fr_spec, ANY_SPEC, ANY_SPEC],
        out_specs=[tile_spec, out_kv, out_kv, fr_spec],
        out_shape=[_sds((s, e), BF16), _sds((s, e), BF16), _sds((s, e), BF16), fr_shape],
        scratch_shapes=[acc, acc, acc, acc, pltpu.SemaphoreType.DMA((2,))],
        compiler_params=_cp("arbitrary", "arbitrary"),
    )(q, k, v, do, frame_t, *prev)


def _in_bwd(dl, dr, w, h, g, scale, dres, name):
    s, d = h.shape
    nj, _, wn = w.shape
    half = nj // 2
    e = half * wn
    tm = _row_tile(s, 512)

    def body(dl_ref, dr_ref, w_ref, h_ref, g_ref, sc_ref, res_ref, dh_ref, st_ref):
        @pl.when(pl.program_id(0) == 0)
        def _():
            st_ref[...] = jnp.zeros_like(st_ref)

        du = jnp.zeros((tm, d), F32)
        for j in range(nj):
            src = dl_ref if j < half else dr_ref
            jj = j % half
            du = du + lax.dot_general(src[:, jj * wn:(jj + 1) * wn], w_ref[j], NT, preferred_element_type=F32)
        gg = g_ref[...]
        rs, xhat, n, _ = _norm_mod(h_ref[...], gg, 0.0, 0.0)
        dn = du * (1.0 + sc_ref[...])
        st_ref[0:1, :] += jnp.sum(du, axis=0, keepdims=True)
        st_ref[1:2, :] += jnp.sum(du * n, axis=0, keepdims=True)
        st_ref[2:3, :] += jnp.sum(dn * xhat, axis=0, keepdims=True)
        dxh = dn * gg
        dh_ref[...] = rs * (dxh - xhat * jnp.mean(dxh * xhat, axis=-1, keepdims=True)) + res_ref[...]

    return pl.pallas_call(
        body, name=name, grid=(s // tm,),
        in_specs=[pl.BlockSpec((tm, e), lambda i: (i, 0)), pl.BlockSpec((tm, e), lambda i: (i, 0)),
                  pl.BlockSpec((nj, d, wn), lambda i: (0, 0, 0)),
                  pl.BlockSpec((tm, d), lambda i: (i, 0)), _vec(d), _vec(d), pl.BlockSpec((tm, d), lambda i: (i, 0))],
        out_specs=[pl.BlockSpec((tm, d), lambda i: (i, 0)), pl.BlockSpec((8, d), lambda i: (0, 0))],
        out_shape=[_sds((s, d), F32), _sds((8, d), F32)],
        compiler_params=_cp("arbitrary"),
    )(dl, dr, w, h, g, scale, dres)


def _tn_matmul(x, ys, xw, yw, ymap, nb, out_shape, stage_shape, out_at, name):
    s, xfull = x.shape
    ts = _row_tile(s, 2048)
    ys = list(ys)
    half = nb // len(ys)
    nk = s // ts

    def body(*refs):
        x_ref, y_refs = refs[0], refs[1:1 + len(ys)]
        o_hbm, xt_ref, acc_ref, stage_ref, sem = refs[1 + len(ys):]
        k, b = pl.program_id(0), pl.program_id(1)

        @pl.when(b == 0)
        def _():
            xt_ref[...] = x_ref[...].T

        for n, y_ref in enumerate(y_refs):
            @pl.when((b >= n * half) & (b < (n + 1) * half))
            def _():
                xt = xt_ref[...] if xw == xfull else xt_ref[pl.ds(pl.multiple_of(b * xw, xw), xw), :]
                part = jnp.dot(xt, y_ref[...], preferred_element_type=F32)

                @pl.when(k == 0)
                def _():
                    acc_ref[b] = part

                @pl.when(k > 0)
                def _():
                    acc_ref[b] += part

        @pl.when(k == nk - 1)
        def _():
            stage_ref[...] = acc_ref[b].astype(BF16).reshape(stage_shape)
            cps = [pltpu.make_async_copy(src, dst, sem.at[i]) for i, (src, dst) in enumerate(out_at(o_hbm, stage_ref, b))]
            for cp in cps:
                cp.start()
            for cp in cps:
                cp.wait()

    in_specs = [pl.BlockSpec((ts, xfull), lambda k, b: (k, 0))]
    for n in range(len(ys)):
        in_specs.append(pl.BlockSpec(
            (ts, yw), lambda k, b, n=n: (k, ymap(jnp.clip(b - n * half, 0, half - 1)))))
    return pl.pallas_call(
        body, name=name, grid=(nk, nb), in_specs=in_specs,
        out_specs=HBM_SPEC, out_shape=_sds(out_shape, BF16),
        scratch_shapes=[pltpu.VMEM((xfull, ts), BF16), pltpu.VMEM((nb, xw, yw), F32), pltpu.VMEM(stage_shape, BF16),
                        pltpu.SemaphoreType.DMA((2,))],
        compiler_params=_cp("arbitrary", "arbitrary"),
    )(x, *ys)


def _grad_w_in(u, dl, dr, name):
    d = u.shape[1]
    wn = 2 * dl.shape[1] // N_DEV
    halves = lambda o, st, b: [(st.at[:, pl.ds(0, wn)], o.at[2 * b]), (st.at[:, pl.ds(wn, wn)], o.at[2 * b + 1])]
    return _tn_matmul(u, (dl, dr), d, 2 * wn, lambda b: b, N_DEV // 2, (N_DEV, d, wn), (d, 2 * wn), halves, name)


def _grad_w_out(gated, dhb, w, gate, name):
    s, e = gated.shape
    d = dhb.shape[1]
    ts = _row_tile(s, 1024)
    nk = s // ts

    def body(x_ref, y_ref, w_ref, g_ref, o_ref, dg_ref, acc_ref):
        k = pl.program_id(0)
        part = lax.dot_general(x_ref[...], y_ref[...], TN, preferred_element_type=F32)

        @pl.when(k == 0)
        def _():
            acc_ref[...] = part

        @pl.when(k > 0)
        def _():
            acc_ref[...] += part

        @pl.when(k == nk - 1)
        def _():
            gm = acc_ref[...]
            dg_ref[...] = jnp.sum(gm * w_ref[...].reshape(e, d).astype(F32), axis=0, keepdims=True)
            o_ref[...] = (gm * g_ref[...]).astype(BF16).reshape(o_ref.shape)

    wspec = pl.BlockSpec((N_DEV, e // N_DEV, d), lambda k: (0, 0, 0))
    return pl.pallas_call(
        body, name=name, grid=(nk,),
        in_specs=[pl.BlockSpec((ts, e), lambda k: (k, 0)), pl.BlockSpec((ts, d), lambda k: (k, 0)), wspec, _vec(d)],
        out_specs=[wspec, _vec(d)],
        out_shape=[_sds((N_DEV, e // N_DEV, d), BF16), _sds((1, d), F32)],
        scratch_shapes=[pltpu.VMEM((e, d), F32)],
        compiler_params=_cp("arbitrary"),
    )(gated, dhb, w, gate)


def _grad_w_group(pooled, dm, name):
    s, e = pooled.shape
    ng = len(POOL_WINDOWS)
    gw = e // ng
    ts = _row_tile(s, 2048)
    nk = s // ts

    def body(x_ref, y_ref, o_ref, acc_ref):
        k = pl.program_id(0)
        for g in range(ng):
            cols = slice(g * gw, (g + 1) * gw)
            part = lax.dot_general(x_ref[:, cols], y_ref[:, cols], TN, preferred_element_type=F32)

            @pl.when(k == 0)
            def _():
                acc_ref[g] = part

            @pl.when(k > 0)
            def _():
                acc_ref[g] += part

        @pl.when(k == nk - 1)
        def _():
            for g in range(ng):
                o_ref[:, g] = acc_ref[g].astype(BF16).reshape(N_DEV, gw // N_DEV, gw)

    return pl.pallas_call(
        body, name=name, grid=(nk,),
        in_specs=[pl.BlockSpec((ts, e), lambda k: (k, 0)), pl.BlockSpec((ts, e), lambda k: (k, 0))],
        out_specs=pl.BlockSpec((N_DEV, ng, gw // N_DEV, gw), lambda k: (0, 0, 0, 0)),
        out_shape=_sds((N_DEV, ng, gw // N_DEV, gw), BF16),
        scratch_shapes=[pltpu.VMEM((ng, gw, gw), F32)],
        compiler_params=_cp("arbitrary"),
    )(pooled, dm)


def _adamw(staged, w, m, v, name):
    shape = w.shape
    nl = len(staged)
    n = staged[0].shape[0]
    cdim = shape[-1]
    total = 1
    for a in shape[:-1]:
        total *= a
    rows = total // nl
    sts = [st.reshape(n, rows, cdim) for st in staged]
    tr = rows if rows * cdim <= 128 * 1024 else max(8, (128 * 1024 // cdim) // 8 * 8)
    while rows % tr:
        tr -= 8
    nblk = rows // tr

    def body(*refs):
        s_refs = refs[:nl]
        w_ref, m_ref, v_ref, g_ref, d_ref, mo_ref, vo_ref = refs[nl:]
        for ll in range(nl):
            @pl.when(pl.program_id(0) == ll)
            def _():
                g = s_refs[ll][0].astype(F32)
                for j in range(1, n):
                    g = g + s_refs[ll][j].astype(F32)
                mn = ADAM_B1 * m_ref[...] + (1.0 - ADAM_B1) * g
                vn = ADAM_B2 * v_ref[...] + (1.0 - ADAM_B2) * (g * g)
                m_hat = mn / (1.0 - ADAM_B1 ** ADAM_STEP)
                v_hat = vn / (1.0 - ADAM_B2 ** ADAM_STEP)
                g_ref[...] = g
                d_ref[...] = -ADAM_LR * (m_hat / (jnp.sqrt(v_hat) + ADAM_EPS) + ADAM_WD * w_ref[...])
                mo_ref[...] = mn
                vo_ref[...] = vn

    blk = pl.BlockSpec((None, tr, cdim), lambda l, i: (l, i, 0))
    st_specs = [pl.BlockSpec((n, tr, cdim), lambda l, i, ll=ll: (0, jnp.clip(i + (l - ll) * nblk, 0, nblk - 1), 0))
                for ll in range(nl)]
    outs = pl.pallas_call(
        body, name=name, grid=(nl, nblk),
        in_specs=st_specs + [blk, blk, blk],
        out_specs=[blk] * 4, out_shape=[_sds((nl, rows, cdim), F32)] * 4,
        compiler_params=_cp("arbitrary", "arbitrary"),
    )(*sts, w.reshape(nl, rows, cdim), m.reshape(nl, rows, cdim), v.reshape(nl, rows, cdim))
    return [o.reshape(shape) for o in outs]


def _pack(parts, total):
    flat = jnp.concatenate([p.reshape(-1) for p in parts])
    return jnp.pad(flat, (0, total - flat.shape[0])).reshape(1, total)


def kernel(x, c, ada_w, ada_b, norm_g, a_w_in, a_w_group, a_scale, a_w_out, kv_norm_g, kv_ada_w, kv_ada_b, w_kv, b_w_in, b_rel_bias, b_w_out, final_g, loss_target, m_ada_w, m_ada_b, m_norm_g, m_a_w_in, m_a_w_group, m_a_scale, m_a_w_out, m_kv_norm_g, m_kv_ada_w, m_kv_ada_b, m_w_kv, m_b_w_in, m_b_rel_bias, m_b_w_out, m_final_g, v_ada_w, v_ada_b, v_norm_g, v_a_w_in, v_a_w_group, v_a_scale, v_a_w_out, v_kv_norm_g, v_kv_ada_w, v_kv_ada_b, v_w_kv, v_b_w_in, v_b_rel_bias, v_b_w_out, v_final_g):
    s, d = x.shape[1], x.shape[2]
    depth, _, wa = ada_w.shape
    wk = kv_ada_w.shape[1]
    n_a, n_b = a_w_in.shape[0], b_w_in.shape[0]
    e = a_w_out.shape[1] * N_DEV
    nrel = b_rel_bias.shape[-1]
    me = 4 * lax.axis_index("x") + 2 * lax.axis_index("y") + lax.axis_index("c")
    h0 = x[0]
    target = loss_target[0]

    bf = lambda a: a.astype(BF16)
    tie = lambda val, tok: lax.optimization_barrier((val, tok))[0]

    c_all = _exchange([c.reshape(1, 1, d)], [True], "gather_c")[0].reshape(N_DEV, d)
    ada_b_loc = lax.dynamic_slice(ada_b, (0, me * wa), (depth, wa)).reshape(depth, 1, wa)
    kv_ada_b_loc = lax.dynamic_slice(kv_ada_b, (me * wk,), (wk,)).reshape(1, wk)
    mods_cols = _mods_fwd(c_all, ada_w, ada_b_loc, kv_ada_w, kv_ada_b_loc)
    nm = depth * wa + wk
    mods_me, a_scale_all = _exchange([mods_cols.reshape(N_DEV, 1, nm), a_scale], [False, True], "exchange_mods")
    groups = [[bf(a_w_in[l]), bf(a_w_group[l]), bf(a_w_out[l])] for l in range(n_a)]
    groups[0] = groups[0][1:]
    groups.append([bf(w_kv), bf(b_w_in[0]), bf(b_w_out[0])])
    groups += [[bf(b_w_in[l]), bf(b_w_out[l])] for l in range(1, n_b)]
    handles = []
    first_handle, token = _push_start([bf(a_w_in[0])], [True], mods_me, "gather_start_first")
    for gi, grp in enumerate(groups):
        hd, token = _push_start(grp, [True] * len(grp), token, f"gather_start_{gi}")
        handles.append(hd)
    mods_me = mods_me.reshape(N_DEV, nm)
    mods = mods_me[:, :depth * wa].reshape(N_DEV, depth, wa).transpose(1, 0, 2).reshape(depth, 3, 1, d)
    kv_mods = mods_me[:, depth * wa:].reshape(2, 1, d)
    a_scale_full = a_scale_all.transpose(1, 0, 2).reshape(n_a, 1, e)
    ones_e = jnp.ones((1, e), F32)

    saved = []
    h = h0
    k = v = hk = wkv = None
    frames = [_bias_frames(tie(jnp.pad(b_rel_bias[bi], ((0, 0), (0, REL_PAD - nrel))), token).reshape(N_HEADS, 1, REL_PAD))
              for bi in range(n_b)]
    after = frames[-1][1]
    for layer in range(depth):
        shift, scale, gate = mods[layer, 0], mods[layer, 1], mods[layer, 2]
        g = norm_g[layer].reshape(1, d)
        if layer == 0:
            w_in, = _push_wait(first_handle, after, "gather_wait_first")
            u, left, z = _inproj_fwd(h, g, shift, scale, w_in, "inproj_fwd")
            w_group, w_out = _push_wait(handles[0], u, "gather_wait_0")
        elif layer < n_a:
            w_in, w_group, w_out = _push_wait(handles[layer], after, f"gather_wait_{layer}")
            u, left, z = _inproj_fwd(h, g, shift, scale, w_in, "inproj_fwd")
        if layer < n_a:
            pooled, mixed = _pool_fwd(left, w_group)
            gated, hn = _gate_out_fwd(mixed, z, a_scale_full[layer], w_out, h, gate, "gate_out_fwd")
            saved.append(dict(h=h, u=u, left=left, z=z, pooled=pooled, m=mixed, gated=gated,
                              w_in=w_in, w_group=w_group, w_out=w_out))
        else:
            bi = layer - n_a
            got = _push_wait(handles[layer], after, f"gather_wait_{layer}")
            if bi == 0:
                wkv, w_in, w_out = got
                hk, k, v = _inproj_fwd(h, kv_norm_g.reshape(1, d), kv_mods[0], kv_mods[1], wkv, "inproj_kv", pad=LEFT)
            else:
                w_in, w_out = got
            u, left, z = _inproj_fwd(h, g, shift, scale, w_in, "inproj_fwd")
            frame, frame_t = frames[bi]
            att = _attn_fwd(left, k, v, frame)
            gated, hn = _gate_out_fwd(att, z, ones_e, w_out, h, gate, "gate_out_fwd")
            saved.append(dict(h=h, u=u, left=left, z=z, m=att, gated=gated, frame_t=frame_t,
                              w_in=w_in, w_out=w_out))
        h = hn
        after = h
    dh, d_final_g, loss_part = _final_loss(h, final_g.reshape(1, d), target)

    d_mods = [None] * depth
    d_norm_g = [None] * depth
    d_rel = [None] * n_b
    d_ascale = [None] * n_a
    assert n_b == 2
    dk = dv = None
    tie = lambda val, tok: lax.optimization_barrier((val, tok))[0]
    grad_handles = [None] * (depth + 1)
    for layer in range(depth - 1, n_a - 1, -1):
        bi = layer - n_a
        sv = saved[layer]
        scale, gate = mods[layer, 1], mods[layer, 2]
        dhb, datt, dz, _ = _out_bwd(dh, gate, sv["w_out"], sv["m"], sv["z"], ones_e, "out_bwd")
        dq, dk, dv, dframe_t = _attn_bwd(sv["left"], k, v, datt, sv["frame_t"], None if bi == n_b - 1 else (dk, dv))
        d_rel[bi] = _bias_bwd(dframe_t).reshape(N_HEADS, REL_PAD)[:, :nrel]
        g_out, d_gate = _grad_w_out(sv["gated"], dhb, sv["w_out"], gate, "grad_w_out")
        g_in = _grad_w_in(sv["u"], dq, dz, "grad_w_in")
        grad_handles[layer], tok = _push_start([g_in, g_out], [False, False], d_gate, f"grads_start_{layer}")
        dh, st = _in_bwd(tie(dq, tok), dz, sv["w_in"], sv["h"], norm_g[layer].reshape(1, d), scale, dh, "in_bwd")
        d_mods[layer] = jnp.concatenate([st[0], st[1], d_gate[0]])
        d_norm_g[layer] = st[2]
    dkb, dvb = dk, dv
    h_kv = saved[n_a]["h"]
    g_w_kv = _grad_w_in(hk, dkb, dvb, "grad_w_in")
    grad_handles[depth], tok = _push_start([g_w_kv], [False], dkb, "grads_start_kv")
    dh, st = _in_bwd(tie(dkb, tok), dvb, wkv, h_kv, kv_norm_g.reshape(1, d), kv_mods[1], dh, "in_bwd")
    d_kv_mods = jnp.concatenate([st[0], st[1]])
    d_kv_norm_g = st[2]
    for layer in range(n_a - 1, -1, -1):
        sv = saved[layer]
        scale, gate = mods[layer, 1], mods[layer, 2]
        dhb, dm, dz, da = _out_bwd(dh, gate, sv["w_out"], sv["m"], sv["z"], a_scale_full[layer], "out_bwd")
        d_ascale[layer] = da.reshape(N_DEV, e // N_DEV)
        g_group = _grad_w_group(sv["pooled"], dm, "grad_w_group")
        g_out, d_gate = _grad_w_out(sv["gated"], dhb, sv["w_out"], gate, "grad_w_out")
        if layer == 0:
            last_handle, tok = _push_start([g_group, g_out], [False] * 2, d_gate, "grads_start_last")
            dm = tie(dm, tok)
        dval = _pool_bwd(dm, sv["w_group"])
        g_in = _grad_w_in(sv["u"], dval, dz, "grad_w_in")
        late = [g_in] if layer == 0 else [g_in, g_group, g_out]
        grad_handles[layer], tok = _push_start(late, [False] * len(late), d_gate, f"grads_start_{layer}")
        dh, st = _in_bwd(tie(dval, tok), dz, sv["w_in"], sv["h"], norm_g[layer].reshape(1, d), scale, dh, "in_bwd")
        d_mods[layer] = jnp.concatenate([st[0], st[1], d_gate[0]])
        d_norm_g[layer] = st[2]
    grad_x = dh.reshape(1, s, d)

    d_mods = jnp.stack(d_mods)
    dm_slots = jnp.concatenate(
        [d_mods.reshape(depth, N_DEV, wa).transpose(1, 0, 2).reshape(N_DEV, depth * wa), d_kv_mods.reshape(N_DEV, wk)],
        axis=1).reshape(N_DEV, 1, nm)
    small_parts = [d_mods, d_kv_mods, jnp.stack(d_norm_g), d_kv_norm_g, d_final_g, jnp.stack(d_rel), loss_part[0, :1]]
    n_small = sum(int(p.size) for p in small_parts)
    n_small_pad = -(-n_small // LANES) * LANES
    small = _pack(small_parts, n_small_pad).reshape(1, 1, n_small_pad)
    d_ascale_slots = jnp.stack(d_ascale, axis=1)
    s_a_scale, dm_cols, small_all = _exchange([d_ascale_slots, dm_slots, small], [False, False, True],
                                              "exchange_small")

    g_ada_w, g_kv_ada_w = _mods_bwd(c_all, dm_cols.reshape(N_DEV, nm), depth, wa, wk)
    res = {}
    res["ada_w"] = _adamw([g_ada_w[None]], ada_w, m_ada_w, v_ada_w, "adamw")
    res["kv_ada_w"] = _adamw([g_kv_ada_w[None]], kv_ada_w, m_kv_ada_w, v_kv_ada_w, "adamw")
    res["a_scale"] = _adamw([s_a_scale[:, l] for l in range(n_a)], a_scale, m_a_scale, v_a_scale, "adamw")
    after = res["ada_w"][1]
    s_b = [None] * n_b
    for layer in range(depth - 1, n_a - 1, -1):
        s_b[layer - n_a] = _push_wait(grad_handles[layer], after, f"grads_wait_{layer}")
        after = s_b[layer - n_a][0]
    res["b_w_in"] = _adamw([sb[0] for sb in s_b], b_w_in, m_b_w_in, v_b_w_in, "adamw")
    res["b_w_out"] = _adamw([sb[1] for sb in s_b], b_w_out, m_b_w_out, v_b_w_out, "adamw")
    s_w_kv, = _push_wait(grad_handles[depth], res["b_w_out"][1], "grads_wait_kv")
    res["w_kv"] = _adamw([s_w_kv], w_kv, m_w_kv, v_w_kv, "adamw")
    after = res["w_kv"][1]
    s_a = [None] * n_a
    for layer in range(n_a - 1, 0, -1):
        s_a[layer] = _push_wait(grad_handles[layer], after, f"grads_wait_{layer}")
        after = s_a[layer][0]
    s_group0, s_out0 = _push_wait(last_handle, after, "grads_wait_last")
    s_a[0] = [None, s_group0, s_out0]
    res["a_w_group"] = _adamw([sa[1] for sa in s_a], a_w_group, m_a_w_group, v_a_w_group, "adamw")
    res["a_w_out"] = _adamw([sa[2] for sa in s_a], a_w_out, m_a_w_out, v_a_w_out, "adamw")
    s_a[0][0], = _push_wait(grad_handles[0], res["a_w_out"][1], "grads_wait_0")
    res["a_w_in"] = _adamw([sa[0] for sa in s_a], a_w_in, m_a_w_in, v_a_w_in, "adamw")
    small_names = ["ada_b", "kv_ada_b", "norm_g", "kv_norm_g", "final_g", "b_rel_bias"]
    small_w = dict(ada_b=ada_b, kv_ada_b=kv_ada_b, norm_g=norm_g, kv_norm_g=kv_norm_g, final_g=final_g,
                   b_rel_bias=b_rel_bias)
    small_m = dict(ada_b=m_ada_b, kv_ada_b=m_kv_ada_b, norm_g=m_norm_g, kv_norm_g=m_kv_norm_g, final_g=m_final_g,
                   b_rel_bias=m_b_rel_bias)
    small_v = dict(ada_b=v_ada_b, kv_ada_b=v_kv_ada_b, norm_g=v_norm_g, kv_norm_g=v_kv_norm_g, final_g=v_final_g,
                   b_rel_bias=v_b_rel_bias)
    sw = _pack([small_w[n] for n in small_names], n_small_pad)
    smm = _pack([small_m[n] for n in small_names], n_small_pad)
    svv = _pack([small_v[n] for n in small_names] + [jnp.ones((n_small_pad - n_small + 1,), F32)], n_small_pad)
    small_out = _adamw([small_all.reshape(N_DEV, 1, n_small_pad)], sw, smm, svv, "adamw")
    off = 0
    for n in small_names:
        size = int(small_w[n].size)
        res[n] = [o[0, off:off + size].reshape(small_w[n].shape) for o in small_out]
        off += size
    loss = small_out[0][0, n_small - 1]

    order = ["ada_w", "ada_b", "norm_g", "a_w_in", "a_w_group", "a_scale", "a_w_out", "kv_norm_g", "kv_ada_w",
             "kv_ada_b", "w_kv", "b_w_in", "b_rel_bias", "b_w_out", "final_g"]
    outs = [loss, grad_x]
    for part in range(4):
        outs += [res[n][part] for n in order]
    return tuple(outs)
```

```python
import functools

import jax
import jax.numpy as jnp
from jax import lax
from jax.experimental import pallas as pl
from jax.experimental.pallas import tpu as pltpu

F32 = jnp.float32
BF16 = jnp.bfloat16

N_DEV = 8
CHUNK = 64
LEFT_CHUNKS = 8
N_HEADS = 16
POOL_WINDOWS = (2, 4, 8, 16)
REL_CLIP = 128
EPS = 1e-6
ADAM_LR = 0.001
ADAM_B1 = 0.9
ADAM_B2 = 0.999
ADAM_EPS = 1e-08
ADAM_WD = 0.01
ADAM_STEP = 10

LANES = 128
TQ = 2 * CHUNK
LEFT = LEFT_CHUNKS * CHUNK
BAND = LEFT + TQ
FRAME_PAD = BAND + TQ
NFRAME = LEFT // TQ + 1
POOL_HALO = 16
REL_PAD = 384
NEG = -1e30
ATTN_GROUP_FWD = 64
ATTN_GROUP = 32
ATTN_GROUP_ACC = 16
ATTN_BATCH = 8
ATTN_BATCH_BWD = 8
VMEM_LIMIT = 56 * 1024 * 1024

NT = (((1,), (1,)), ((), ()))
TN = (((0,), (0,)), ((), ()))

HBM_SPEC = pl.BlockSpec(memory_space=pltpu.HBM)
ANY_SPEC = pl.BlockSpec(memory_space=pl.ANY)
SEM_SPEC = pl.BlockSpec(memory_space=pltpu.SEMAPHORE)


def _cp(*sem):
    return pltpu.CompilerParams(dimension_semantics=sem or None, vmem_limit_bytes=VMEM_LIMIT)


def _sds(shape, dtype):
    return jax.ShapeDtypeStruct(tuple(shape), dtype)


def _row_tile(s, want):
    return min(want, s)


def _vec(d):
    return pl.BlockSpec((1, d), lambda *_: (0, 0))


def _sigmoid(z):
    return 1.0 / (1.0 + jnp.exp(-z))


def _exchange(xs, gather, name):
    n = len(xs)
    out_shapes = []
    for x, ga in zip(xs, gather):
        out_shapes.append(_sds((N_DEV,) + (x.shape if ga else x.shape[1:]), x.dtype))

    def body(*refs):
        ins, outs = refs[:n], refs[n:2 * n]
        send_sems, recv_sems, local_sems = refs[2 * n:]
        mx, my, mc = lax.axis_index("x"), lax.axis_index("y"), lax.axis_index("c")
        me = 4 * mx + 2 * my + mc
        local = []
        for k in range(n):
            src = ins[k] if gather[k] else ins[k].at[me]
            cp = pltpu.make_async_copy(src, outs[k].at[me], local_sems.at[k])
            cp.start()
            local.append(cp)
        sends, recvs = [], []
        for r in range(1, N_DEV):
            px, py, pc = (mx + (r >> 2)) % 2, (my + ((r >> 1) & 1)) % 2, (mc + (r & 1)) % 2
            peer = 4 * px + 2 * py + pc
            for k in range(n):
                sem = k * (N_DEV - 1) + r - 1
                src = ins[k] if gather[k] else ins[k].at[peer]
                send = pltpu.make_async_remote_copy(
                    src_ref=src, dst_ref=outs[k].at[me], send_sem=send_sems.at[sem], recv_sem=recv_sems.at[sem],
                    device_id=(px, py, pc), device_id_type=pl.DeviceIdType.MESH)
                send.start()
                sends.append(send)
                recvs.append(pltpu.make_async_remote_copy(
                    src_ref=src, dst_ref=outs[k].at[peer], send_sem=send_sems.at[sem], recv_sem=recv_sems.at[sem],
                    device_id=(px, py, pc), device_id_type=pl.DeviceIdType.MESH))
        for cp in recvs:
            cp.wait_recv()
        for cp in sends:
            cp.wait_send()
        for cp in local:
            cp.wait()

    return pl.pallas_call(
        body, name=name, out_shape=out_shapes,
        in_specs=[HBM_SPEC] * n, out_specs=[HBM_SPEC] * n,
        scratch_shapes=[pltpu.SemaphoreType.DMA((n * (N_DEV - 1),)), pltpu.SemaphoreType.DMA((n * (N_DEV - 1),)),
                        pltpu.SemaphoreType.DMA((n,))],
    )(*xs)


def _peer(mx, my, mc, r):
    px, py, pc = (mx + (r >> 2)) % 2, (my + ((r >> 1) & 1)) % 2, (mc + (r & 1)) % 2
    return (px, py, pc), 4 * px + 2 * py + pc


def _push_start(xs, gather, dep, name):
    n = len(xs)
    nsem = n * (N_DEV - 1)
    me = 4 * lax.axis_index("x") + 2 * lax.axis_index("y") + lax.axis_index("c")
    lands = []
    for x, ga in zip(xs, gather):
        own = x[None] if ga else lax.dynamic_index_in_dim(x, me, 0, keepdims=True)
        empty = lax.empty((N_DEV,) + own.shape[1:], x.dtype)
        lands.append(lax.dynamic_update_slice(empty, own, (me,) + (0,) * (own.ndim - 1)))

    def body(*refs):
        ins, lands_in = refs[:n], refs[n:2 * n]
        send_sems, recv_sems = refs[2 * n + 1], refs[2 * n + 2]
        token = refs[-1]
        mx, my, mc = lax.axis_index("x"), lax.axis_index("y"), lax.axis_index("c")
        mine = 4 * mx + 2 * my + mc
        for k in range(n):
            for r in range(1, N_DEV):
                dev, peer = _peer(mx, my, mc, r)
                sem = k * (N_DEV - 1) + r - 1
                pltpu.make_async_remote_copy(
                    src_ref=ins[k] if gather[k] else ins[k].at[peer], dst_ref=lands_in[k].at[mine],
                    send_sem=send_sems.at[sem], recv_sem=recv_sems.at[sem],
                    device_id=dev, device_id_type=pl.DeviceIdType.MESH).start()
        token[...] = jnp.zeros_like(token)

    hbm = lambda a: pltpu.HBM(a.shape, a.dtype)
    outs = pl.pallas_call(
        body, name=name,
        out_shape=(pltpu.SemaphoreType.DMA((nsem,)), pltpu.SemaphoreType.DMA((nsem,)),
                   *[hbm(x) for x in xs], *[hbm(a) for a in lands], _sds((8, LANES), F32)),
        in_specs=[HBM_SPEC] * (2 * n) + [ANY_SPEC],
        out_specs=(SEM_SPEC, SEM_SPEC, *[HBM_SPEC] * (2 * n), pl.BlockSpec(memory_space=pltpu.VMEM)),
        input_output_aliases={k: 2 + k for k in range(2 * n)},
        compiler_params=pltpu.CompilerParams(has_side_effects=pltpu.SideEffectType.DATAFLOW_SIDE_EFFECTING),
    )(*[pltpu.with_memory_space_constraint(x, pltpu.HBM) for x in xs],
      *[pltpu.with_memory_space_constraint(a, pltpu.HBM) for a in lands], dep)
    return (outs[0], outs[1], outs[2:2 + n], outs[2 + n:2 + 2 * n], tuple(gather)), outs[-1]


def _push_wait(handle, after, name):
    send_sems, recv_sems, srcs, lands, gather = handle
    n = len(srcs)

    def body(*refs):
        ins, lands_in = refs[:n], refs[n:2 * n]
        send_sems, recv_sems = refs[2 * n], refs[2 * n + 1]
        mx, my, mc = lax.axis_index("x"), lax.axis_index("y"), lax.axis_index("c")
        for k in range(n):
            for r in range(1, N_DEV):
                dev, peer = _peer(mx, my, mc, r)
                sem = k * (N_DEV - 1) + r - 1
                cp = pltpu.make_async_remote_copy(
                    src_ref=ins[k] if gather[k] else ins[k].at[peer], dst_ref=lands_in[k].at[peer],
                    send_sem=send_sems.at[sem], recv_sem=recv_sems.at[sem],
                    device_id=dev, device_id_type=pl.DeviceIdType.MESH)
                cp.wait_send()
                cp.wait_recv()

    hbm = lambda a: pltpu.HBM(a.shape, a.dtype)
    outs = pl.pallas_call(
        body, name=name,
        out_shape=(*[hbm(x) for x in srcs], *[hbm(a) for a in lands]),
        in_specs=[HBM_SPEC] * (2 * n) + [SEM_SPEC, SEM_SPEC, ANY_SPEC],
        out_specs=tuple([HBM_SPEC] * (2 * n)),
        input_output_aliases={k: k for k in range(2 * n)},
        compiler_params=pltpu.CompilerParams(has_side_effects=pltpu.SideEffectType.DATAFLOW_SIDE_EFFECTING),
    )(*srcs, *lands, send_sems, recv_sems, after)
    return list(outs[n:])


def _mods_fwd(c_all, ada_w, ada_b_loc, kv_ada_w, kv_ada_b_loc):
    nl, d, wa = ada_w.shape
    wk = kv_ada_w.shape[1]

    def body(c_ref, w_ref, b_ref, kw_ref, kb_ref, o_ref):
        c = c_ref[...]
        ca = c * _sigmoid(c)
        for l in range(nl):
            o_ref[:, l * wa:(l + 1) * wa] = jnp.dot(
                ca, w_ref[l], preferred_element_type=F32, precision=lax.Precision.HIGHEST) + b_ref[l]
        o_ref[:, nl * wa:] = jnp.dot(
            ca, kw_ref[...], preferred_element_type=F32, precision=lax.Precision.HIGHEST) + kb_ref[...]

    return pl.pallas_call(body, name="mods_fwd", out_shape=_sds((N_DEV, nl * wa + wk), F32),
                          compiler_params=_cp())(c_all, ada_w, ada_b_loc, kv_ada_w, kv_ada_b_loc)


def _mods_bwd(c_all, dm, nl, wa, wk):
    d = c_all.shape[1]

    def body(c_ref, d_ref, gw_ref, gk_ref):
        c = c_ref[...]
        ca = c * _sigmoid(c)
        for l in range(nl):
            gw_ref[l] = lax.dot_general(ca, d_ref[:, l * wa:(l + 1) * wa], TN,
                                        preferred_element_type=F32, precision=lax.Precision.HIGHEST)
        gk_ref[...] = lax.dot_general(ca, d_ref[:, nl * wa:], TN,
                                      preferred_element_type=F32, precision=lax.Precision.HIGHEST)

    return pl.pallas_call(body, name="mods_bwd", out_shape=[_sds((nl, d, wa), F32), _sds((d, wk), F32)],
                          compiler_params=_cp())(c_all, dm)


def _norm_mod(hf, g, shift, scale):
    rs = lax.rsqrt(jnp.mean(hf * hf, axis=-1, keepdims=True) + EPS)
    xhat = hf * rs
    n = xhat * g
    return rs, xhat, n, n * (1.0 + scale) + shift


def _inproj_fwd(h, g, shift, scale, w, name, pad=0):
    s, d = h.shape
    nj, _, wn = w.shape
    half = nj // 2
    tm = _row_tile(s, 512)
    assert pad in (0, tm)
    pb = pad // tm

    def body(h_ref, g_ref, sh_ref, sc_ref, w_ref, u_ref, l_ref, r_ref):
        _, _, _, u = _norm_mod(h_ref[...], g_ref[...], sh_ref[...], sc_ref[...])
        ub = u.astype(BF16)
        u_ref[...] = ub
        for j in range(nj):
            o_ref = l_ref if j < half else r_ref
            jj = j % half
            o_ref[:, jj * wn:(jj + 1) * wn] = jnp.dot(ub, w_ref[j], preferred_element_type=F32).astype(BF16)
        if pb:
            @pl.when(pl.program_id(0) == 0)
            def _():
                l_ref[...] = jnp.zeros_like(l_ref)
                r_ref[...] = jnp.zeros_like(r_ref)

    e = half * wn
    src = lambda i: (jnp.maximum(i - pb, 0), 0)
    return pl.pallas_call(
        body, name=name, grid=(s // tm + pb,),
        in_specs=[pl.BlockSpec((tm, d), src), _vec(d), _vec(d), _vec(d),
                  pl.BlockSpec((nj, d, wn), lambda i: (0, 0, 0))],
        out_specs=[pl.BlockSpec((tm, d), src), pl.BlockSpec((tm, e), lambda i: (i, 0)),
                   pl.BlockSpec((tm, e), lambda i: (i, 0))],
        out_shape=[_sds((s, d), BF16), _sds((s + pad, e), BF16), _sds((s + pad, e), BF16)],
        compiler_params=_cp("arbitrary"),
    )(h, g, shift, scale, w)


def _pool_fwd(val, wg):
    s, e = val.shape
    ng = len(POOL_WINDOWS)
    gw = e // ng
    tm = _row_tile(s, 512)
    hb = tm // POOL_HALO

    def body(v_ref, halo_ref, w_ref, p_ref, m_ref):
        i = pl.program_id(0)
        t = i * tm + lax.broadcasted_iota(jnp.int32, (tm, 1), 0)
        for g, wdw in enumerate(POOL_WINDOWS):
            cols = slice(g * gw, (g + 1) * gw)
            v = v_ref[:, cols].astype(F32)
            halo = jnp.where(i > 0, halo_ref[:, cols].astype(F32), 0.0)
            acc = jnp.concatenate([halo, v], axis=0)
            sh = 1
            while sh < wdw:
                acc = acc + pltpu.roll(acc, sh, 0)
                sh *= 2
            cnt = jnp.minimum(t + 1, wdw).astype(F32)
            pb = (acc[POOL_HALO:, :] / cnt - v).astype(BF16)
            p_ref[:, cols] = pb
            m_ref[:, cols] = jnp.dot(pb, w_ref[:, g].reshape(gw, gw), preferred_element_type=F32).astype(BF16)

    return pl.pallas_call(
        body, name="pool_fwd", grid=(s // tm,),
        in_specs=[pl.BlockSpec((tm, e), lambda i: (i, 0)),
                  pl.BlockSpec((POOL_HALO, e), lambda i: (jnp.maximum(i * hb - 1, 0), 0)),
                  pl.BlockSpec((N_DEV, ng, gw // N_DEV, gw), lambda i: (0, 0, 0, 0))],
        out_specs=[pl.BlockSpec((tm, e), lambda i: (i, 0)), pl.BlockSpec((tm, e), lambda i: (i, 0))],
        out_shape=[_sds((s, e), BF16), _sds((s, e), BF16)],
        compiler_params=_cp("arbitrary"),
    )(val, val, wg)


def _gate_out_fwd(m, z, ascale, w, h, gate, name):
    s, e = m.shape
    d = h.shape[1]
    tm = _row_tile(s, 512)

    def body(m_ref, z_ref, a_ref, w_ref, h_ref, g_ref, gd_ref, y_ref, ho_ref):
        z = z_ref[...].astype(F32)
        gb = ((m_ref[...].astype(F32) * a_ref[...]) * (z * _sigmoid(z))).astype(BF16)
        gd_ref[...] = gb
        y = jnp.dot(gb, w_ref[...].reshape(e, d), preferred_element_type=F32)
        y_ref[...] = y.astype(BF16)
        ho_ref[...] = h_ref[...] + g_ref[...] * y

    return pl.pallas_call(
        body, name=name, grid=(s // tm,),
        in_specs=[pl.BlockSpec((tm, e), lambda i: (i, 0)), pl.BlockSpec((tm, e), lambda i: (i, 0)), _vec(e),
                  pl.BlockSpec((N_DEV, e // N_DEV, d), lambda i: (0, 0, 0)),
                  pl.BlockSpec((tm, d), lambda i: (i, 0)), _vec(d)],
        out_specs=[pl.BlockSpec((tm, e), lambda i: (i, 0)), pl.BlockSpec((tm, d), lambda i: (i, 0)),
                   pl.BlockSpec((tm, d), lambda i: (i, 0))],
        out_shape=[_sds((s, e), BF16), _sds((s, d), BF16), _sds((s, d), F32)],
        compiler_params=_cp("arbitrary"),
    )(m, z, ascale, w, h, gate)


def _rel_onehot(shape, r_axis):
    r = lax.broadcasted_iota(jnp.int32, shape, r_axis)
    j = lax.broadcasted_iota(jnp.int32, shape, 1 - r_axis)
    dist = LEFT - (j - TQ)
    return (jnp.clip(dist, -REL_CLIP, REL_CLIP) + REL_CLIP == r).astype(F32)


def _skew(x, sign):
    row = lax.broadcasted_iota(jnp.int32, x.shape, 0)
    for b in range(TQ.bit_length() - 1):
        amt = (1 << b) if sign > 0 else FRAME_PAD - (1 << b)
        x = jnp.where(((row >> b) & 1) == 1, pltpu.roll(x, amt, 1), x)
    return x


def _bias_frames(rel):
    nh = rel.shape[0]

    def body(r_ref, f_ref, ft_ref, fx_ref):
        @pl.when(pl.program_id(0) == 0)
        def _():
            onehot = _rel_onehot((REL_PAD, FRAME_PAD), 0)
            for hh in range(nh):
                fx_ref[hh] = jnp.dot(r_ref[hh], onehot, preferred_element_type=F32, precision=lax.Precision.HIGHEST)

        fext = fx_ref[pl.program_id(0)]
        x = _skew(jnp.broadcast_to(fext, (TQ, FRAME_PAD)), 1)[:, TQ:]
        qc = lax.broadcasted_iota(jnp.int32, (TQ, BAND), 0) // CHUNK
        m = lax.broadcasted_iota(jnp.int32, (TQ, BAND), 1)
        mc = m // CHUNK
        x = jnp.where((mc >= qc) & (mc <= qc + LEFT_CHUNKS), x, NEG)
        for f in range(NFRAME):
            xf = jnp.where(m >= LEFT - f * TQ, x, NEG)
            f_ref[f] = xf
            ft_ref[f] = xf.T

    return pl.pallas_call(
        body, name="bias_frames", grid=(nh,),
        in_specs=[pl.BlockSpec((nh, 1, REL_PAD), lambda h: (0, 0, 0))],
        out_specs=[pl.BlockSpec((None, NFRAME, TQ, BAND), lambda h: (h, 0, 0, 0)),
                   pl.BlockSpec((None, NFRAME, BAND, TQ), lambda h: (h, 0, 0, 0))],
        out_shape=[_sds((nh, NFRAME, TQ, BAND), F32), _sds((nh, NFRAME, BAND, TQ), F32)],
        scratch_shapes=[pltpu.VMEM((nh, 1, FRAME_PAD), F32)],
        compiler_params=_cp("arbitrary"),
    )(rel)


def _bias_bwd(dft):
    nh = dft.shape[0]

    def body(d_ref, o_ref, cs_ref):
        d = d_ref[0]
        for f in range(1, NFRAME):
            d = d + d_ref[f]
        x = jnp.concatenate([jnp.zeros((TQ, TQ), F32), d.T], axis=1)
        cs_ref[pl.program_id(0)] = jnp.sum(_skew(x, -1), axis=0, keepdims=True)

        @pl.when(pl.program_id(0) == nh - 1)
        def _():
            onehot = _rel_onehot((FRAME_PAD, REL_PAD), 1)
            for hh in range(nh):
                o_ref[hh] = jnp.dot(cs_ref[hh], onehot, preferred_element_type=F32, precision=lax.Precision.HIGHEST)

    return pl.pallas_call(
        body, name="bias_bwd", grid=(nh,),
        in_specs=[pl.BlockSpec((None, NFRAME, BAND, TQ), lambda h: (h, 0, 0, 0))],
        out_specs=pl.BlockSpec((nh, 1, REL_PAD), lambda h: (0, 0, 0)),
        out_shape=_sds((nh, 1, REL_PAD), F32),
        scratch_shapes=[pltpu.VMEM((nh, 1, FRAME_PAD), F32)],
        compiler_params=_cp("arbitrary"),
    )(dft)


def _attn_tiles(s, group):
    nt = s // TQ
    ni = min(group, nt)
    return nt // ni, ni


def _attn_fwd(q, k, v, frame):
    s, e = q.shape
    dh = e // N_HEADS
    ng, ni = _attn_tiles(s, ATTN_GROUP_FWD)
    sm = dh ** -0.5
    nb = min(ATTN_BATCH, ni)

    def body(q_ref, k_ref, v_ref, b_ref, o_ref):
        g0 = pl.program_id(1) * ni
        for tb in range(0, ni, nb):
            ts = range(tb, tb + nb)
            keys = [pl.ds(pl.multiple_of((g0 + t) * TQ, TQ), BAND) for t in ts]
            rows = [pl.ds(t * TQ, TQ) for t in ts]
            scs = [lax.dot_general(q_ref[r, :], k_ref[kk, :], NT, preferred_element_type=F32)
                   for r, kk in zip(rows, keys)]
            ps, ls = [], []
            for t, sc in zip(ts, scs):
                sc = sc * sm + b_ref[jnp.minimum(g0 + t, NFRAME - 1)]
                p = jnp.exp(sc - jnp.max(sc, axis=-1, keepdims=True))
                ls.append(jnp.sum(p, axis=-1, keepdims=True))
                ps.append(p.astype(BF16))
            for r, p, l, kk in zip(rows, ps, ls, keys):
                o = jnp.dot(p, v_ref[kk, :], preferred_element_type=F32)
                o_ref[r, :] = (o / l).astype(BF16)

    return pl.pallas_call(
        body, name="attn_fwd", grid=(N_HEADS, ng),
        in_specs=[pl.BlockSpec((ni * TQ, dh), lambda h, i: (i, h)),
                  pl.BlockSpec((s + LEFT, dh), lambda h, i: (0, h)), pl.BlockSpec((s + LEFT, dh), lambda h, i: (0, h)),
                  pl.BlockSpec((None, NFRAME, TQ, BAND), lambda h, i: (h, 0, 0, 0))],
        out_specs=pl.BlockSpec((ni * TQ, dh), lambda h, i: (i, h)),
        out_shape=_sds((s, e), BF16),
        compiler_params=_cp("arbitrary", "arbitrary"),
    )(q, k, v, frame)


def _final_loss(h, g, target):
    s, d = h.shape
    tm = _row_tile(s, 512)

    def body(h_ref, g_ref, t_ref, dh_ref, dg_ref, l_ref):
        @pl.when(pl.program_id(0) == 0)
        def _():
            dg_ref[...] = jnp.zeros_like(dg_ref)
            l_ref[...] = jnp.zeros_like(l_ref)

        hf = h_ref[...]
        gg = g_ref[...]
        rs = lax.rsqrt(jnp.mean(hf * hf, axis=-1, keepdims=True) + EPS)
        xhat = hf * rs
        diff = xhat * gg - t_ref[...]
        l_ref[...] += 0.5 * jnp.sum(jnp.mean(diff * diff, axis=-1, keepdims=True), axis=0, keepdims=True)
        dout = diff * (1.0 / d)
        dg_ref[...] += jnp.sum(dout * xhat, axis=0, keepdims=True)
        dxh = dout * gg
        dh_ref[...] = rs * (dxh - xhat * jnp.mean(dxh * xhat, axis=-1, keepdims=True))

    return pl.pallas_call(
        body, name="final_loss", grid=(s // tm,),
        in_specs=[pl.BlockSpec((tm, d), lambda i: (i, 0)), _vec(d), pl.BlockSpec((tm, d), lambda i: (i, 0))],
        out_specs=[pl.BlockSpec((tm, d), lambda i: (i, 0)), _vec(d), _vec(LANES)],
        out_shape=[_sds((s, d), F32), _sds((1, d), F32), _sds((1, LANES), F32)],
        compiler_params=_cp("arbitrary"),
    )(h, g, target)


def _out_bwd(dh, y, gate, w, m, z, ascale, name):
    s, d = dh.shape
    e = m.shape[1]
    tm = _row_tile(s, 512)

    def body(dh_ref, y_ref, g_ref, w_ref, m_ref, z_ref, a_ref, dy_ref, dm_ref, dz_ref, dg_ref, da_ref):
        @pl.when(pl.program_id(0) == 0)
        def _():
            dg_ref[...] = jnp.zeros_like(dg_ref)
            da_ref[...] = jnp.zeros_like(da_ref)

        dhf = dh_ref[...]
        dg_ref[...] += jnp.sum(dhf * y_ref[...].astype(F32), axis=0, keepdims=True)
        dyb = (g_ref[...] * dhf).astype(BF16)
        dy_ref[...] = dyb
        dgated = lax.dot_general(dyb, w_ref[...].reshape(e, d), NT, preferred_element_type=F32)
        z = z_ref[...].astype(F32)
        sig = _sigmoid(z)
        mf = m_ref[...].astype(F32)
        a = a_ref[...]
        dms = dgated * (z * sig)
        da_ref[...] += jnp.sum(dms * mf, axis=0, keepdims=True)
        dm_ref[...] = (dms * a).astype(BF16)
        dz_ref[...] = (dgated * (mf * a) * (sig * (1.0 + z * (1.0 - sig)))).astype(BF16)

    return pl.pallas_call(
        body, name=name, grid=(s // tm,),
        in_specs=[pl.BlockSpec((tm, d), lambda i: (i, 0)), pl.BlockSpec((tm, d), lambda i: (i, 0)), _vec(d),
                  pl.BlockSpec((N_DEV, e // N_DEV, d), lambda i: (0, 0, 0)),
                  pl.BlockSpec((tm, e), lambda i: (i, 0)), pl.BlockSpec((tm, e), lambda i: (i, 0)), _vec(e)],
        out_specs=[pl.BlockSpec((tm, d), lambda i: (i, 0)), pl.BlockSpec((tm, e), lambda i: (i, 0)),
                   pl.BlockSpec((tm, e), lambda i: (i, 0)), _vec(d), _vec(e)],
        out_shape=[_sds((s, d), BF16), _sds((s, e), BF16), _sds((s, e), BF16), _sds((1, d), F32), _sds((1, e), F32)],
        compiler_params=_cp("arbitrary"),
    )(dh, y, gate, w, m, z, ascale)


def _pool_bwd(dm, wg):
    s, e = dm.shape
    ng = len(POOL_WINDOWS)
    gw = e // ng
    tm = _row_tile(s, 512)
    hb = tm // POOL_HALO
    nsteps = s // tm

    def body(d_ref, halo_ref, w_ref, o_ref):
        i = pl.program_id(0)
        t = i * tm + lax.broadcasted_iota(jnp.int32, (tm + POOL_HALO, 1), 0)
        for g, wdw in enumerate(POOL_WINDOWS):
            cols = slice(g * gw, (g + 1) * gw)
            dmx = jnp.concatenate([d_ref[:, cols], halo_ref[:, cols]], axis=0)
            dp = lax.dot_general(dmx, w_ref[:, g].reshape(gw, gw), NT, preferred_element_type=F32)
            dp = jnp.where(t < s, dp, 0.0)
            acc = dp / jnp.minimum(t + 1, wdw).astype(F32)
            sh = 1
            while sh < wdw:
                acc = acc + pltpu.roll(acc, tm + POOL_HALO - sh, 0)
                sh *= 2
            o_ref[:, cols] = (acc[:tm, :] - dp[:tm, :]).astype(BF16)

    return pl.pallas_call(
        body, name="pool_bwd", grid=(nsteps,),
        in_specs=[pl.BlockSpec((tm, e), lambda i: (i, 0)),
                  pl.BlockSpec((POOL_HALO, e), lambda i: (jnp.minimum((i + 1) * hb, s // POOL_HALO - 1), 0)),
                  pl.BlockSpec((N_DEV, ng, gw // N_DEV, gw), lambda i: (0, 0, 0, 0))],
        out_specs=pl.BlockSpec((tm, e), lambda i: (i, 0)),
        out_shape=_sds((s, e), BF16),
        compiler_params=_cp("arbitrary"),
    )(dm, dm, wg)


def _attn_bwd(q, k, v, do, frame_t, prev=None):
    s, e = q.shape
    dh = e // N_HEADS
    ng, ni = _attn_tiles(s, ATTN_GROUP if prev is None else ATTN_GROUP_ACC)
    sm = dh ** -0.5
    nleft = LEFT // TQ

    def core(q_ref, k_ref, v_ref, do_ref, b_ref, dq_ref, dk_ref, dv_ref, db_ref, init):
        i = pl.program_id(1)
        g0 = i * ni

        @pl.when(i == 0)
        def _():
            init()
            db_ref[...] = jnp.zeros_like(db_ref)

        keys = [pl.ds(pl.multiple_of((g0 + t) * TQ, TQ), BAND) for t in range(ni)]
        rows = [pl.ds(t * TQ, TQ) for t in range(ni)]
        pbs, dsbs = [], []
        nb = min(ATTN_BATCH_BWD, ni)
        for tb in range(0, ni, nb):
            tiles = range(tb, tb + nb)
            sts = [lax.dot_general(k_ref[keys[t], :], q_ref[rows[t], :], NT, preferred_element_type=F32) for t in tiles]
            dpts = [lax.dot_general(v_ref[keys[t], :], do_ref[rows[t], :], NT, preferred_element_type=F32) for t in tiles]
            dsts = {}
            for t, st, dpt in zip(tiles, sts, dpts):
                st = st * sm + b_ref[jnp.minimum(g0 + t, NFRAME - 1)]
                p = jnp.exp(st - jnp.max(st, axis=0, keepdims=True))
                p = p * (1.0 / jnp.sum(p, axis=0, keepdims=True))
                dst = p * (dpt - jnp.sum(dpt * p, axis=0, keepdims=True))
                pbs.append(p.astype(BF16))
                dsbs.append(dst.astype(BF16))
                dsts[t] = dst
            rest = None
            for t in tiles:
                if t < nleft:
                    db_ref[jnp.minimum(g0 + t, NFRAME - 1)] += dsts[t]
                else:
                    rest = dsts[t] if rest is None else rest + dsts[t]
            if rest is not None:
                db_ref[NFRAME - 1] += rest
            for t in tiles:
                dq = lax.dot_general(dsbs[t], k_ref[keys[t], :], TN, preferred_element_type=F32)
                dq_ref[rows[t], :] = (dq * sm).astype(BF16)
        for r in range(ni + nleft):
            ts = [t for t in range(ni) if 0 <= r - t <= nleft]
            blk = lambda xs: jnp.concatenate([xs[t][(r - t) * TQ:(r - t + 1) * TQ, :] for t in ts], axis=1)
            qrows = slice(ts[0] * TQ, (ts[-1] + 1) * TQ)
            krows = pl.ds(pl.multiple_of((g0 + r) * TQ, TQ), TQ)
            dk_ref[krows, :] += jnp.dot(blk(dsbs), q_ref[qrows, :], preferred_element_type=F32) * sm
            dv_ref[krows, :] += jnp.dot(blk(pbs), do_ref[qrows, :], preferred_element_type=F32)

    tile_spec = pl.BlockSpec((ni * TQ, dh), lambda h, i: (i, h))
    kv_spec = pl.BlockSpec((s + LEFT, dh), lambda h, i: (0, h))
    fr_spec = pl.BlockSpec((None, NFRAME, BAND, TQ), lambda h, i: (h, 0, 0, 0))
    fr_shape = _sds((N_HEADS, NFRAME, BAND, TQ), F32)
    if prev is None:
        def body(q_ref, k_ref, v_ref, do_ref, b_ref, dq_ref, dk_ref, dv_ref, db_ref):
            def init():
                dk_ref[...] = jnp.zeros_like(dk_ref)
                dv_ref[...] = jnp.zeros_like(dv_ref)
            core(q_ref, k_ref, v_ref, do_ref, b_ref, dq_ref, dk_ref, dv_ref, db_ref, init)

        return pl.pallas_call(
            body, name="attn_bwd", grid=(N_HEADS, ng),
            in_specs=[tile_spec, kv_spec, kv_spec, tile_spec, fr_spec],
            out_specs=[tile_spec, kv_spec, kv_spec, fr_spec],
            out_shape=[_sds((s, e), BF16), _sds((s + LEFT, e), F32), _sds((s + LEFT, e), F32), fr_shape],
            compiler_params=_cp("arbitrary", "arbitrary"),
        )(q, k, v, do, frame_t)

    def body_acc(q_ref, k_ref, v_ref, do_ref, b_ref, dkp_hbm, dvp_hbm, dq_ref, dkb_ref, dvb_ref, db_ref,
                 dk_acc, dv_acc, dkp_buf, dvp_buf, sems):
        cols = pl.ds(pl.multiple_of(pl.program_id(0) * dh, dh), dh)
        fetch = [pltpu.make_async_copy(dkp_hbm.at[:, cols], dkp_buf, sems.at[0]),
                 pltpu.make_async_copy(dvp_hbm.at[:, cols], dvp_buf, sems.at[1])]

        def init():
            for cp in fetch:
                cp.start()
            dk_acc[...] = jnp.zeros_like(dk_acc)
            dv_acc[...] = jnp.zeros_like(dv_acc)
        core(q_ref, k_ref, v_ref, do_ref, b_ref, dq_ref, dk_acc, dv_acc, db_ref, init)

        @pl.when(pl.program_id(1) == ng - 1)
        def _():
            for cp in fetch:
                cp.wait()
            dkb_ref[...] = (dk_acc[LEFT:, :] + dkp_buf[LEFT:, :]).astype(BF16)
            dvb_ref[...] = (dv_acc[LEFT:, :] + dvp_buf[LEFT:, :]).astype(BF16)

    out_kv = pl.BlockSpec((s, dh), lambda h, i: (0, h))
    acc = pltpu.VMEM((s + LEFT, dh), F32)
    return pl.pallas_call(
        body_acc, name="attn_bwd_acc", grid=(N_HEADS, ng),
        in_specs=[tile_spec, kv_spec, kv_spec, tile_spec, fr_spec, ANY_SPEC, ANY_SPEC],
        out_specs=[tile_spec, out_kv, out_kv, fr_spec],
        out_shape=[_sds((s, e), BF16), _sds((s, e), BF16), _sds((s, e), BF16), fr_shape],
        scratch_shapes=[acc, acc, acc, acc, pltpu.SemaphoreType.DMA((2,))],
        compiler_params=_cp("arbitrary", "arbitrary"),
    )(q, k, v, do, frame_t, *prev)


def _in_bwd(dl, dr, w, h, g, scale, dres, name):
    s, d = h.shape
    nj, _, wn = w.shape
    half = nj // 2
    e = half * wn
    tm = _row_tile(s, 512)

    def body(dl_ref, dr_ref, w_ref, h_ref, g_ref, sc_ref, res_ref, dh_ref, st_ref):
        @pl.when(pl.program_id(0) == 0)
        def _():
            st_ref[...] = jnp.zeros_like(st_ref)

        du = jnp.zeros((tm, d), F32)
        for j in range(nj):
            src = dl_ref if j < half else dr_ref
            jj = j % half
            du = du + lax.dot_general(src[:, jj * wn:(jj + 1) * wn], w_ref[j], NT, preferred_element_type=F32)
        gg = g_ref[...]
        rs, xhat, n, _ = _norm_mod(h_ref[...], gg, 0.0, 0.0)
        dn = du * (1.0 + sc_ref[...])
        st_ref[0:1, :] += jnp.sum(du, axis=0, keepdims=True)
        st_ref[1:2, :] += jnp.sum(du * n, axis=0, keepdims=True)
        st_ref[2:3, :] += jnp.sum(dn * xhat, axis=0, keepdims=True)
        dxh = dn * gg
        dh_ref[...] = rs * (dxh - xhat * jnp.mean(dxh * xhat, axis=-1, keepdims=True)) + res_ref[...]

    return pl.pallas_call(
        body, name=name, grid=(s // tm,),
        in_specs=[pl.BlockSpec((tm, e), lambda i: (i, 0)), pl.BlockSpec((tm, e), lambda i: (i, 0)),
                  pl.BlockSpec((nj, d, wn), lambda i: (0, 0, 0)),
                  pl.BlockSpec((tm, d), lambda i: (i, 0)), _vec(d), _vec(d), pl.BlockSpec((tm, d), lambda i: (i, 0))],
        out_specs=[pl.BlockSpec((tm, d), lambda i: (i, 0)), pl.BlockSpec((8, d), lambda i: (0, 0))],
        out_shape=[_sds((s, d), F32), _sds((8, d), F32)],
        compiler_params=_cp("arbitrary"),
    )(dl, dr, w, h, g, scale, dres)


def _tn_matmul(x, ys, xw, yw, ymap, nb, out_shape, stage_shape, out_at, name):
    s, xfull = x.shape
    ts = _row_tile(s, 2048)
    ys = list(ys)
    half = nb // len(ys)
    nk = s // ts

    def body(*refs):
        x_ref, y_refs = refs[0], refs[1:1 + len(ys)]
        o_hbm, xt_ref, acc_ref, stage_ref, sem = refs[1 + len(ys):]
        k, b = pl.program_id(0), pl.program_id(1)

        @pl.when(b == 0)
        def _():
            xt_ref[...] = x_ref[...].T

        for n, y_ref in enumerate(y_refs):
            @pl.when((b >= n * half) & (b < (n + 1) * half))
            def _():
                xt = xt_ref[...] if xw == xfull else xt_ref[pl.ds(pl.multiple_of(b * xw, xw), xw), :]
                part = jnp.dot(xt, y_ref[...], preferred_element_type=F32)

                @pl.when(k == 0)
                def _():
                    acc_ref[b] = part

                @pl.when(k > 0)
                def _():
                    acc_ref[b] += part

        @pl.when(k == nk - 1)
        def _():
            stage_ref[...] = acc_ref[b].astype(BF16).reshape(stage_shape)
            cps = [pltpu.make_async_copy(src, dst, sem.at[i]) for i, (src, dst) in enumerate(out_at(o_hbm, stage_ref, b))]
            for cp in cps:
                cp.start()
            for cp in cps:
                cp.wait()

    in_specs = [pl.BlockSpec((ts, xfull), lambda k, b: (k, 0))]
    for n in range(len(ys)):
        in_specs.append(pl.BlockSpec(
            (ts, yw), lambda k, b, n=n: (k, ymap(jnp.clip(b - n * half, 0, half - 1)))))
    return pl.pallas_call(
        body, name=name, grid=(nk, nb), in_specs=in_specs,
        out_specs=HBM_SPEC, out_shape=_sds(out_shape, BF16),
        scratch_shapes=[pltpu.VMEM((xfull, ts), BF16), pltpu.VMEM((nb, xw, yw), F32), pltpu.VMEM(stage_shape, BF16),
                        pltpu.SemaphoreType.DMA((2,))],
        compiler_params=_cp("arbitrary", "arbitrary"),
    )(x, *ys)


def _grad_w_in(u, dl, dr, name):
    d = u.shape[1]
    wn = 2 * dl.shape[1] // N_DEV
    halves = lambda o, st, b: [(st.at[:, pl.ds(0, wn)], o.at[2 * b]), (st.at[:, pl.ds(wn, wn)], o.at[2 * b + 1])]
    return _tn_matmul(u, (dl, dr), d, 2 * wn, lambda b: b, N_DEV // 2, (N_DEV, d, wn), (d, 2 * wn), halves, name)


def _grad_w_out(gated, dy, name):
    s, e = gated.shape
    d = dy.shape[1]
    ts = _row_tile(s, 1024)
    nk = s // ts

    def body(x_ref, y_ref, o_ref, acc_ref):
        k = pl.program_id(0)
        part = lax.dot_general(x_ref[...], y_ref[...], TN, preferred_element_type=F32)

        @pl.when(k == 0)
        def _():
            acc_ref[...] = part

        @pl.when(k > 0)
        def _():
            acc_ref[...] += part

        @pl.when(k == nk - 1)
        def _():
            o_ref[...] = acc_ref[...].astype(BF16).reshape(o_ref.shape)

    return pl.pallas_call(
        body, name=name, grid=(nk,),
        in_specs=[pl.BlockSpec((ts, e), lambda k: (k, 0)), pl.BlockSpec((ts, d), lambda k: (k, 0))],
        out_specs=pl.BlockSpec((N_DEV, e // N_DEV, d), lambda k: (0, 0, 0)),
        out_shape=_sds((N_DEV, e // N_DEV, d), BF16),
        scratch_shapes=[pltpu.VMEM((e, d), F32)],
        compiler_params=_cp("arbitrary"),
    )(gated, dy)


def _grad_w_group(pooled, dm, name):
    s, e = pooled.shape
    ng = len(POOL_WINDOWS)
    gw = e // ng
    ts = _row_tile(s, 2048)
    nk = s // ts

    def body(x_ref, y_ref, o_ref, acc_ref):
        k = pl.program_id(0)
        for g in range(ng):
            cols = slice(g * gw, (g + 1) * gw)
            part = lax.dot_general(x_ref[:, cols], y_ref[:, cols], TN, preferred_element_type=F32)

            @pl.when(k == 0)
            def _():
                acc_ref[g] = part

            @pl.when(k > 0)
            def _():
                acc_ref[g] += part

        @pl.when(k == nk - 1)
        def _():
            for g in range(ng):
                o_ref[:, g] = acc_ref[g].astype(BF16).reshape(N_DEV, gw // N_DEV, gw)

    return pl.pallas_call(
        body, name=name, grid=(nk,),
        in_specs=[pl.BlockSpec((ts, e), lambda k: (k, 0)), pl.BlockSpec((ts, e), lambda k: (k, 0))],
        out_specs=pl.BlockSpec((N_DEV, ng, gw // N_DEV, gw), lambda k: (0, 0, 0, 0)),
        out_shape=_sds((N_DEV, ng, gw // N_DEV, gw), BF16),
        scratch_shapes=[pltpu.VMEM((ng, gw, gw), F32)],
        compiler_params=_cp("arbitrary"),
    )(pooled, dm)


def _adamw(staged, w, m, v, name):
    shape = w.shape
    nl = len(staged)
    n = staged[0].shape[0]
    cdim = shape[-1]
    total = 1
    for a in shape[:-1]:
        total *= a
    rows = total // nl
    sts = [st.reshape(n, rows, cdim) for st in staged]
    tr = rows if rows * cdim <= 128 * 1024 else max(8, (128 * 1024 // cdim) // 8 * 8)
    while rows % tr:
        tr -= 8
    nblk = rows // tr

    def body(*refs):
        s_refs = refs[:nl]
        w_ref, m_ref, v_ref, g_ref, d_ref, mo_ref, vo_ref = refs[nl:]
        for ll in range(nl):
            @pl.when(pl.program_id(0) == ll)
            def _():
                g = s_refs[ll][0].astype(F32)
                for j in range(1, n):
                    g = g + s_refs[ll][j].astype(F32)
                mn = ADAM_B1 * m_ref[...] + (1.0 - ADAM_B1) * g
                vn = ADAM_B2 * v_ref[...] + (1.0 - ADAM_B2) * (g * g)
                m_hat = mn / (1.0 - ADAM_B1 ** ADAM_STEP)
                v_hat = vn / (1.0 - ADAM_B2 ** ADAM_STEP)
                g_ref[...] = g
                d_ref[...] = -ADAM_LR * (m_hat / (jnp.sqrt(v_hat) + ADAM_EPS) + ADAM_WD * w_ref[...])
                mo_ref[...] = mn
                vo_ref[...] = vn

    blk = pl.BlockSpec((None, tr, cdim), lambda l, i: (l, i, 0))
    st_specs = [pl.BlockSpec((n, tr, cdim), lambda l, i, ll=ll: (0, jnp.clip(i + (l - ll) * nblk, 0, nblk - 1), 0))
                for ll in range(nl)]
    outs = pl.pallas_call(
        body, name=name, grid=(nl, nblk),
        in_specs=st_specs + [blk, blk, blk],
        out_specs=[blk] * 4, out_shape=[_sds((nl, rows, cdim), F32)] * 4,
        compiler_params=_cp("arbitrary", "arbitrary"),
    )(*sts, w.reshape(nl, rows, cdim), m.reshape(nl, rows, cdim), v.reshape(nl, rows, cdim))
    return [o.reshape(shape) for o in outs]


def _pack(parts, total):
    flat = jnp.concatenate([p.reshape(-1) for p in parts])
    return jnp.pad(flat, (0, total - flat.shape[0])).reshape(1, total)


def kernel(x, c, ada_w, ada_b, norm_g, a_w_in, a_w_group, a_scale, a_w_out, kv_norm_g, kv_ada_w, kv_ada_b, w_kv, b_w_in, b_rel_bias, b_w_out, final_g, loss_target, m_ada_w, m_ada_b, m_norm_g, m_a_w_in, m_a_w_group, m_a_scale, m_a_w_out, m_kv_norm_g, m_kv_ada_w, m_kv_ada_b, m_w_kv, m_b_w_in, m_b_rel_bias, m_b_w_out, m_final_g, v_ada_w, v_ada_b, v_norm_g, v_a_w_in, v_a_w_group, v_a_scale, v_a_w_out, v_kv_norm_g, v_kv_ada_w, v_kv_ada_b, v_w_kv, v_b_w_in, v_b_rel_bias, v_b_w_out, v_final_g):
    s, d = x.shape[1], x.shape[2]
    depth, _, wa = ada_w.shape
    wk = kv_ada_w.shape[1]
    n_a, n_b = a_w_in.shape[0], b_w_in.shape[0]
    e = a_w_out.shape[1] * N_DEV
    nrel = b_rel_bias.shape[-1]
    me = 4 * lax.axis_index("x") + 2 * lax.axis_index("y") + lax.axis_index("c")
    h0 = x[0]
    target = loss_target[0]

    bf = lambda a: a.astype(BF16)
    tie = lambda val, tok: lax.optimization_barrier((val, tok))[0]

    c_all = _exchange([c.reshape(1, 1, d)], [True], "gather_c")[0].reshape(N_DEV, d)
    ada_b_loc = lax.dynamic_slice(ada_b, (0, me * wa), (depth, wa)).reshape(depth, 1, wa)
    kv_ada_b_loc = lax.dynamic_slice(kv_ada_b, (me * wk,), (wk,)).reshape(1, wk)
    mods_cols = _mods_fwd(c_all, ada_w, ada_b_loc, kv_ada_w, kv_ada_b_loc)
    nm = depth * wa + wk
    mods_me, a_scale_all = _exchange([mods_cols.reshape(N_DEV, 1, nm), a_scale], [False, True], "exchange_mods")
    groups = [[bf(a_w_in[l]), bf(a_w_group[l]), bf(a_w_out[l])] for l in range(n_a)]
    groups[0] = groups[0][1:]
    groups.append([bf(w_kv), bf(b_w_in[0]), bf(b_w_out[0])])
    groups += [[bf(b_w_in[l]), bf(b_w_out[l])] for l in range(1, n_b)]
    handles = []
    first_handle, token = _push_start([bf(a_w_in[0])], [True], mods_me, "gather_start_first")
    for gi, grp in enumerate(groups):
        hd, token = _push_start(grp, [True] * len(grp), token, f"gather_start_{gi}")
        handles.append(hd)
    mods_me = mods_me.reshape(N_DEV, nm)
    mods = mods_me[:, :depth * wa].reshape(N_DEV, depth, wa).transpose(1, 0, 2).reshape(depth, 3, 1, d)
    kv_mods = mods_me[:, depth * wa:].reshape(2, 1, d)
    a_scale_full = a_scale_all.transpose(1, 0, 2).reshape(n_a, 1, e)
    ones_e = jnp.ones((1, e), F32)

    saved = []
    h = h0
    k = v = hk = wkv = None
    frames = [_bias_frames(tie(jnp.pad(b_rel_bias[bi], ((0, 0), (0, REL_PAD - nrel))), token).reshape(N_HEADS, 1, REL_PAD))
              for bi in range(n_b)]
    after = frames[-1][1]
    for layer in range(depth):
        shift, scale, gate = mods[layer, 0], mods[layer, 1], mods[layer, 2]
        g = norm_g[layer].reshape(1, d)
        if layer == 0:
            w_in, = _push_wait(first_handle, after, "gather_wait_first")
            u, left, z = _inproj_fwd(h, g, shift, scale, w_in, "inproj_fwd")
            w_group, w_out = _push_wait(handles[0], u, "gather_wait_0")
        elif layer < n_a:
            w_in, w_group, w_out = _push_wait(handles[layer], after, f"gather_wait_{layer}")
            u, left, z = _inproj_fwd(h, g, shift, scale, w_in, "inproj_fwd")
        if layer < n_a:
            pooled, mixed = _pool_fwd(left, w_group)
            gated, y, hn = _gate_out_fwd(mixed, z, a_scale_full[layer], w_out, h, gate, "gate_out_fwd")
            saved.append(dict(h=h, u=u, left=left, z=z, pooled=pooled, m=mixed, gated=gated, y=y,
                              w_in=w_in, w_group=w_group, w_out=w_out))
        else:
            bi = layer - n_a
            got = _push_wait(handles[layer], after, f"gather_wait_{layer}")
            if bi == 0:
                wkv, w_in, w_out = got
                hk, k, v = _inproj_fwd(h, kv_norm_g.reshape(1, d), kv_mods[0], kv_mods[1], wkv, "inproj_kv", pad=LEFT)
            else:
                w_in, w_out = got
            u, left, z = _inproj_fwd(h, g, shift, scale, w_in, "inproj_fwd")
            frame, frame_t = frames[bi]
            att = _attn_fwd(left, k, v, frame)
            gated, y, hn = _gate_out_fwd(att, z, ones_e, w_out, h, gate, "gate_out_fwd")
            saved.append(dict(h=h, u=u, left=left, z=z, m=att, gated=gated, y=y, frame_t=frame_t,
                              w_in=w_in, w_out=w_out))
        h = hn
        after = h
    dh, d_final_g, loss_part = _final_loss(h, final_g.reshape(1, d), target)

    d_mods = [None] * depth
    d_norm_g = [None] * depth
    d_rel = [None] * n_b
    d_ascale = [None] * n_a
    assert n_b == 2
    dk = dv = None
    tie = lambda val, tok: lax.optimization_barrier((val, tok))[0]
    grad_handles = [None] * (depth + 1)
    for layer in range(depth - 1, n_a - 1, -1):
        bi = layer - n_a
        sv = saved[layer]
        scale, gate = mods[layer, 1], mods[layer, 2]
        dy, datt, dz, d_gate, _ = _out_bwd(dh, sv["y"], gate, sv["w_out"], sv["m"], sv["z"], ones_e, "out_bwd")
        dq, dk, dv, dframe_t = _attn_bwd(sv["left"], k, v, datt, sv["frame_t"], None if bi == n_b - 1 else (dk, dv))
        d_rel[bi] = _bias_bwd(dframe_t).reshape(N_HEADS, REL_PAD)[:, :nrel]
        g_out = _grad_w_out(sv["gated"], dy, "grad_w_out")
        g_in = _grad_w_in(sv["u"], dq, dz, "grad_w_in")
        grad_handles[layer], tok = _push_start([g_in, g_out], [False, False], d_gate, f"grads_start_{layer}")
        dh, st = _in_bwd(tie(dq, tok), dz, sv["w_in"], sv["h"], norm_g[layer].reshape(1, d), scale, dh, "in_bwd")
        d_mods[layer] = jnp.concatenate([st[0], st[1], d_gate[0]])
        d_norm_g[layer] = st[2]
    dkb, dvb = dk, dv
    h_kv = saved[n_a]["h"]
    g_w_kv = _grad_w_in(hk, dkb, dvb, "grad_w_in")
    grad_handles[depth], tok = _push_start([g_w_kv], [False], dkb, "grads_start_kv")
    dh, st = _in_bwd(tie(dkb, tok), dvb, wkv, h_kv, kv_norm_g.reshape(1, d), kv_mods[1], dh, "in_bwd")
    d_kv_mods = jnp.concatenate([st[0], st[1]])
    d_kv_norm_g = st[2]
    for layer in range(n_a - 1, -1, -1):
        sv = saved[layer]
        scale, gate = mods[layer, 1], mods[layer, 2]
        dy, dm, dz, d_gate, da = _out_bwd(dh, sv["y"], gate, sv["w_out"], sv["m"], sv["z"], a_scale_full[layer],
                                          "out_bwd")
        d_ascale[layer] = da.reshape(N_DEV, e // N_DEV)
        g_group = _grad_w_group(sv["pooled"], dm, "grad_w_group")
        g_out = _grad_w_out(sv["gated"], dy, "grad_w_out")
        if layer == 0:
            last_handle, tok = _push_start([g_group, g_out], [False] * 2, d_gate, "grads_start_last")
            dm = tie(dm, tok)
        dval = _pool_bwd(dm, sv["w_group"])
        g_in = _grad_w_in(sv["u"], dval, dz, "grad_w_in")
        late = [g_in] if layer == 0 else [g_in, g_group, g_out]
        grad_handles[layer], tok = _push_start(late, [False] * len(late), d_gate, f"grads_start_{layer}")
        dh, st = _in_bwd(tie(dval, tok), dz, sv["w_in"], sv["h"], norm_g[layer].reshape(1, d), scale, dh, "in_bwd")
        d_mods[layer] = jnp.concatenate([st[0], st[1], d_gate[0]])
        d_norm_g[layer] = st[2]
    grad_x = dh.reshape(1, s, d)

    d_mods = jnp.stack(d_mods)
    dm_slots = jnp.concatenate(
        [d_mods.reshape(depth, N_DEV, wa).transpose(1, 0, 2).reshape(N_DEV, depth * wa), d_kv_mods.reshape(N_DEV, wk)],
        axis=1).reshape(N_DEV, 1, nm)
    small_parts = [d_mods, d_kv_mods, jnp.stack(d_norm_g), d_kv_norm_g, d_final_g, jnp.stack(d_rel), loss_part[0, :1]]
    n_small = sum(int(p.size) for p in small_parts)
    n_small_pad = -(-n_small // LANES) * LANES
    small = _pack(small_parts, n_small_pad).reshape(1, 1, n_small_pad)
    d_ascale_slots = jnp.stack(d_ascale, axis=1)
    s_a_scale, dm_cols, small_all = _exchange([d_ascale_slots, dm_slots, small], [False, False, True],
                                              "exchange_small")

    g_ada_w, g_kv_ada_w = _mods_bwd(c_all, dm_cols.reshape(N_DEV, nm), depth, wa, wk)
    res = {}
    res["ada_w"] = _adamw([g_ada_w[None]], ada_w, m_ada_w, v_ada_w, "adamw")
    res["kv_ada_w"] = _adamw([g_kv_ada_w[None]], kv_ada_w, m_kv_ada_w, v_kv_ada_w, "adamw")
    res["a_scale"] = _adamw([s_a_scale[:, l] for l in range(n_a)], a_scale, m_a_scale, v_a_scale, "adamw")
    after = res["ada_w"][1]
    s_b = [None] * n_b
    for layer in range(depth - 1, n_a - 1, -1):
        s_b[layer - n_a] = _push_wait(grad_handles[layer], after, f"grads_wait_{layer}")
        after = s_b[layer - n_a][0]
    res["b_w_in"] = _adamw([sb[0] for sb in s_b], b_w_in, m_b_w_in, v_b_w_in, "adamw")
    res["b_w_out"] = _adamw([sb[1] for sb in s_b], b_w_out, m_b_w_out, v_b_w_out, "adamw")
    s_w_kv, = _push_wait(grad_handles[depth], res["b_w_out"][1], "grads_wait_kv")
    res["w_kv"] = _adamw([s_w_kv], w_kv, m_w_kv, v_w_kv, "adamw")
    after = res["w_kv"][1]
    s_a = [None] * n_a
    for layer in range(n_a - 1, 0, -1):
        s_a[layer] = _push_wait(grad_handles[layer], after, f"grads_wait_{layer}")
        after = s_a[layer][0]
    s_group0, s_out0 = _push_wait(last_handle, after, "grads_wait_last")
    s_a[0] = [None, s_group0, s_out0]
    res["a_w_group"] = _adamw([sa[1] for sa in s_a], a_w_group, m_a_w_group, v_a_w_group, "adamw")
    res["a_w_out"] = _adamw([sa[2] for sa in s_a], a_w_out, m_a_w_out, v_a_w_out, "adamw")
    s_a[0][0], = _push_wait(grad_handles[0], res["a_w_out"][1], "grads_wait_0")
    res["a_w_in"] = _adamw([sa[0] for sa in s_a], a_w_in, m_a_w_in, v_a_w_in, "adamw")
    small_names = ["ada_b", "kv_ada_b", "norm_g", "kv_norm_g", "final_g", "b_rel_bias"]
    small_w = dict(ada_b=ada_b, kv_ada_b=kv_ada_b, norm_g=norm_g, kv_norm_g=kv_norm_g, final_g=final_g,
                   b_rel_bias=b_rel_bias)
    small_m = dict(ada_b=m_ada_b, kv_ada_b=m_kv_ada_b, norm_g=m_norm_g, kv_norm_g=m_kv_norm_g, final_g=m_final_g,
                   b_rel_bias=m_b_rel_bias)
    small_v = dict(ada_b=v_ada_b, kv_ada_b=v_kv_ada_b, norm_g=v_norm_g, kv_norm_g=v_kv_norm_g, final_g=v_final_g,
                   b_rel_bias=v_b_rel_bias)
    sw = _pack([small_w[n] for n in small_names], n_small_pad)
    smm = _pack([small_m[n] for n in small_names], n_small_pad)
    svv = _pack([small_v[n] for n in small_names] + [jnp.ones((n_small_pad - n_small + 1,), F32)], n_small_pad)
    small_out = _adamw([small_all.reshape(N_DEV, 1, n_small_pad)], sw, smm, svv, "adamw")
    off = 0
    for n in small_names:
        size = int(small_w[n].size)
        res[n] = [o[0, off:off + size].reshape(small_w[n].shape) for o in small_out]
        off += size
    loss = small_out[0][0, n_small - 1]

    order = ["ada_w", "ada_b", "norm_g", "a_w_in", "a_w_group", "a_scale", "a_w_out", "kv_norm_g", "kv_ada_w",
             "kv_ada_b", "w_kv", "b_w_in", "b_rel_bias", "b_w_out", "final_g"]
    outs = [loss, grad_x]
    for part in range(4):
        outs += [res[n][part] for n in order]
    return tuple(outs)
```
